```python
import math
import jax
import jax.numpy as jnp
from jax import lax
import numpy as np

D_MODEL = 1024
BATCH = 16
SEQ = 256
DEPTH = 4
DEC_BATCH = 2
DEC_SEQ = 4096
PAST_LEN = 512

GRID_W = 64
N_AB = (DEPTH + 1) // 2
N_HY = DEPTH // 2
CHUNK = 64
EPS = 1e-6
A_HEADS = 4
A_DK = 128
A_DV = 128
A_QK_W = A_HEADS * A_DK
A_V_W = A_HEADS * A_DV
A_CONV_W = 2 * A_QK_W + A_V_W
SHORT_K = 3
B_HEADS = 4
B_DK = 128
B_DV = 128
B_QK_W = B_HEADS * B_DK
B_V_W = B_HEADS * B_DV
MIX_W = A_V_W + B_V_W
AB_IN = A_CONV_W + A_V_W + 4 * A_HEADS + 2 * B_QK_W + 2 * B_V_W
HY_SHORT_K = 3
HY_EMB = 33
HY_BANDS = (HY_EMB - 1) // 2
HY_FW = 64
HY_TARGET = 1e-2
HY_FAST_PCT = 0.3
HY_SLOW_PCT = 1.5
D_FF = 2816
FFN_K = 3

kernel_name = "hybrid_flow_trunk_prefix_ctx"


def rmsnorm(x, g):
    xf = x.astype(jnp.float32)
    y = xf * lax.rsqrt(jnp.mean(xf * xf, axis=-1, keepdims=True) + EPS)
    return (y * g.astype(jnp.float32)).astype(x.dtype)


def head_groupnorm(x, g):
    xf = x.astype(jnp.float32)
    mu = jnp.mean(xf, axis=-1, keepdims=True)
    var = jnp.mean(jnp.square(xf - mu), axis=-1, keepdims=True)
    return (xf - mu) * lax.rsqrt(var + EPS) * g.astype(jnp.float32)


def l2norm(x):
    xf = x.astype(jnp.float32)
    return xf * lax.rsqrt(jnp.sum(xf * xf, axis=-1, keepdims=True) + EPS)


def modulate(h, shift, scale):
    return h * (1.0 + scale) + shift


def split_cols(x, sizes):
    idx = np.cumsum(sizes)[:-1].tolist()
    return jnp.split(x, idx, axis=-1)


def heads(x, n):
    b, l, _ = x.shape
    return x.reshape(b, l, n, -1).transpose(0, 2, 1, 3)


def flip_if(t, rev):
    return jnp.flip(t, axis=2) if rev else t


def dwconv1d(x, w):
    k = w.shape[0]
    p = k // 2
    l = x.shape[1]
    xp = jnp.pad(x, ((0, 0), (p, p), (0, 0)))
    out = xp[:, 0:l] * w[0]
    for i in range(1, k):
        out = out + xp[:, i:i + l] * w[i]
    return out


def dwconv2d_grid(x, w):
    b, l, ch = x.shape
    rows = l // GRID_W
    xg = jnp.pad(x.reshape(b, rows, GRID_W, ch), ((0, 0), (1, 1), (1, 1), (0, 0)))
    out = xg[:, 0:rows, 0:GRID_W] * w[0, 0]
    for i in range(3):
        for j in range(3):
            if i == 0 and j == 0:
                continue
            out = out + xg[:, i:i + rows, j:j + GRID_W] * w[i, j]
    return out.reshape(b, l, ch)


def gated_delta_chunked(q, k, v, g, beta, s0):
    f32 = jnp.float32
    b, h, l, dk = q.shape
    dv = v.shape[-1]
    n = l // CHUNK
    q = q.astype(f32).reshape(b, h, n, CHUNK, dk)
    k = k.astype(f32).reshape(b, h, n, CHUNK, dk)
    v = v.astype(f32).reshape(b, h, n, CHUNK, dv)
    g = g.astype(f32).reshape(b, h, n, CHUNK)
    beta = beta.astype(f32).reshape(b, h, n, CHUNK)
    gc = jnp.cumsum(g, axis=-1)
    tril = jnp.tril(jnp.ones((CHUNK, CHUNK), bool))
    stril = jnp.tril(jnp.ones((CHUNK, CHUNK), bool), -1)
    diff = gc[..., :, None] - gc[..., None, :]
    dmask = jnp.where(tril, jnp.exp(jnp.where(tril, diff, 0.0)), 0.0)
    kb = k * beta[..., None]
    lmat = jnp.where(stril, jnp.einsum('bhncd,bhnsd->bhncs', kb, k) * dmask, 0.0)
    eye = jnp.eye(CHUNK, dtype=f32)
    tmat = lax.linalg.triangular_solve(eye + lmat, jnp.broadcast_to(eye, lmat.shape),
                                       left_side=True, lower=True)
    u = tmat @ (v * beta[..., None])
    w = tmat @ (kb * jnp.exp(gc)[..., None])
    attn = jnp.where(tril, jnp.einsum('bhncd,bhnsd->bhncs', q, k) * dmask, 0.0)
    qd = q * jnp.exp(gc)[..., None]
    kd = k * jnp.exp(gc[..., -1:] - gc)[..., None]
    gl = jnp.exp(gc[..., -1])

    def step(s, xs):
        u_i, w_i, attn_i, qd_i, kd_i, gl_i = xs
        v_new = u_i - w_i @ s
        o_i = qd_i @ s + attn_i @ v_new
        s = s * gl_i[..., None, None] + jnp.swapaxes(kd_i, -1, -2) @ v_new
        return s, o_i

    xs = tuple(jnp.moveaxis(t, 2, 0) for t in (u, w, attn, qd, kd, gl))
    s_fin, o = lax.scan(step, s0.astype(f32), xs)
    o = jnp.moveaxis(o, 0, 2).reshape(b, h, l, dv)
    return o, s_fin


def retention_chunked(q, k, v, log_gamma, r0):
    f32 = jnp.float32
    b, h, l, dk = q.shape
    dv = v.shape[-1]
    n = l // CHUNK
    q = q.astype(f32).reshape(b, h, n, CHUNK, dk)
    k = k.astype(f32).reshape(b, h, n, CHUNK, dk)
    v = v.astype(f32).reshape(b, h, n, CHUNK, dv)
    lg = log_gamma.astype(f32)[:, None]
    idx = jnp.arange(CHUNK, dtype=f32)
    diff = idx[:, None] - idx[None, :]
    dmat = jnp.where(diff >= 0, jnp.exp(jnp.maximum(diff, 0.0)[None] * lg[..., None]), 0.0)
    cross_decay = jnp.exp((idx + 1.0)[None] * lg)
    state_decay = jnp.exp((CHUNK - 1.0 - idx)[None] * lg)
    chunk_decay = jnp.exp(CHUNK * lg[:, 0])
    inner = jnp.einsum('bhncd,bhnsd->bhncs', q, k) * dmat[None, :, None]
    o_inner = jnp.einsum('bhncs,bhnsv->bhncv', inner, v)
    kv = jnp.einsum('bhncd,bhncv->bhndv', k * state_decay[None, :, None, :, None], v)

    def step(r, kv_i):
        return r * chunk_decay[None, :, None, None] + kv_i, r

    r_fin, r_prev = lax.scan(step, r0.astype(f32), jnp.moveaxis(kv, 2, 0))
    r_prev = jnp.moveaxis(r_prev, 0, 2)
    o = o_inner + jnp.einsum('bhncd,bhndv->bhncv', q * cross_decay[None, :, None, :, None], r_prev)
    return o.reshape(b, h, l, dv), r_fin


def mixer_ab(h, w_in, conv_a, a_log, dt_bias, norm_a, norm_b, ret_decay, w_out, s_delta0, s_ret0):
    f32 = jnp.float32
    bsz, l, _ = h.shape
    proj = h @ w_in
    qkv_a, z_a, a_raw, b_raw, q_b, k_b, v_b, g_b = split_cols(
        proj, [A_CONV_W, A_V_W, 2 * A_HEADS, 2 * A_HEADS, B_QK_W, B_QK_W, B_V_W, B_V_W])
    qkv_a = jax.nn.silu(dwconv1d(qkv_a, conv_a))
    q_a, k_a, v_a = split_cols(qkv_a, [A_QK_W, A_QK_W, A_V_W])
    q_a = l2norm(heads(q_a, A_HEADS)) * (A_DK ** -0.5)
    k_a = l2norm(heads(k_a, A_HEADS))
    v_a = heads(v_a, A_HEADS)
    g_a = -jnp.exp(a_log.astype(f32)) * jax.nn.softplus(
        a_raw.astype(f32).reshape(bsz, l, 2, A_HEADS) + dt_bias.astype(f32))
    beta_a = jax.nn.sigmoid(b_raw.astype(f32).reshape(bsz, l, 2, A_HEADS))
    g_a = g_a.transpose(2, 0, 3, 1)
    beta_a = beta_a.transpose(2, 0, 3, 1)
    q_b = heads(q_b, B_HEADS) * (B_DK ** -0.5)
    k_b = heads(k_b, B_HEADS)
    v_b = heads(v_b, B_HEADS)
    log_gamma = -jnp.exp(ret_decay.astype(f32))
    o_a = 0.0
    o_b = 0.0
    fin_a = []
    fin_b = []
    for d in range(2):
        rev = d == 1
        oa, sa = gated_delta_chunked(flip_if(q_a, rev), flip_if(k_a, rev), flip_if(v_a, rev),
                                     flip_if(g_a[d], rev), flip_if(beta_a[d], rev), s_delta0[:, d])
        ob, sb = retention_chunked(flip_if(q_b, rev), flip_if(k_b, rev), flip_if(v_b, rev),
                                   log_gamma[d], s_ret0[:, d])
        o_a = o_a + flip_if(oa, rev)
        o_b = o_b + flip_if(ob, rev)
        fin_a.append(sa)
        fin_b.append(sb)
    z = z_a.astype(f32).reshape(bsz, l, A_HEADS, A_DV)
    o_a = rmsnorm(o_a.transpose(0, 2, 1, 3), norm_a) * jax.nn.silu(z)
    gb = g_b.astype(f32).reshape(bsz, l, B_HEADS, B_DV)
    o_b = head_groupnorm(o_b.transpose(0, 2, 1, 3), norm_b) * jax.nn.silu(gb)
    o = jnp.concatenate([o_a.reshape(bsz, l, A_V_W), o_b.reshape(bsz, l, B_V_W)], axis=-1).astype(h.dtype)
    return o @ w_out, jnp.stack(fin_a, axis=1), jnp.stack(fin_b, axis=1)


def hyena_filter(l, w1, b1, freq1, w2, b2, freq2, w3):
    f32 = jnp.float32
    t = jnp.linspace(0.0, 1.0, l, dtype=f32)[:, None]
    wpos = 2.0 * math.pi * jnp.arange(l, dtype=f32)[:, None] / l
    bands = jnp.linspace(1e-4, HY_BANDS - 1, HY_BANDS, dtype=f32)[None, :]
    z = jnp.concatenate([t, jnp.cos(bands * wpos), -jnp.sin(bands * wpos)], axis=-1)
    hid = jnp.sin(freq1 * (z @ w1 + b1))
    hid = jnp.sin(freq2 * (hid @ w2 + b2))
    filt = (hid @ w3).astype(f32)
    min_decay = math.log(HY_TARGET) / HY_SLOW_PCT
    max_decay = math.log(HY_TARGET) / HY_FAST_PCT
    deltas = jnp.abs(jnp.linspace(min_decay, max_decay, D_MODEL, dtype=f32))
    window = jnp.exp(-t * deltas[None, :])
    return filt[:, :D_MODEL] * window, filt[:, D_MODEL:] * window


def bidir_long_conv(u, h_fwd, h_bwd, bias):
    f32 = jnp.float32
    l = u.shape[1]
    kern = jnp.concatenate([h_fwd, jnp.zeros((1, h_fwd.shape[1]), f32), jnp.flip(h_bwd[1:], axis=0)], axis=0)
    uf = jnp.fft.rfft(u.astype(f32), n=2 * l, axis=1)
    kf = jnp.fft.rfft(kern, axis=0)
    y = jnp.fft.irfft(uf * kf[None], n=2 * l, axis=1)[:, :l]
    return (y + u.astype(f32) * bias.astype(f32)).astype(u.dtype)


def mixer_hyena(h, w_in, b_in, conv_w, conv_b, f_w1, f_b1, f_freq1, f_w2, f_b2, f_freq2, f_w3, f_bias, w_out, b_out):
    l = h.shape[1]
    u = dwconv1d(h @ w_in + b_in, conv_w) + conv_b
    x0, x1, v = split_cols(u, [D_MODEL, D_MODEL, D_MODEL])
    h_fwd, h_bwd = hyena_filter(l, f_w1, f_b1, f_freq1, f_w2, f_b2, f_freq2, f_w3)
    v = bidir_long_conv(v * x1, h_fwd, h_bwd, f_bias)
    return (v * x0) @ w_out + b_out


def conv_ffn(h, w_gate, w_up, w_conv, b_conv, w_down, on_grid):
    gate = h @ w_gate
    up = h @ w_up
    gate = dwconv2d_grid(gate, w_conv) if on_grid else dwconv1d(gate, w_conv[1])
    return (jax.nn.silu(gate + b_conv) * up) @ w_down


def setup_inputs(seed: int = 0) -> dict:
    key = jax.random.key(seed)
    ks = iter(jax.random.split(key, 48))
    f32 = jnp.float32
    D = D_MODEL

    def nrm(shape, scale=1.0):
        return jax.random.normal(next(ks), shape, f32) * scale

    def gain(shape):
        return 1.0 + nrm(shape, 0.05)

    ret_base = jnp.log(-jnp.log(1.0 - 2.0 ** (-5.0 - jnp.arange(B_HEADS, dtype=f32))))
    dt = jnp.exp(jax.random.uniform(next(ks), (N_AB, 2, A_HEADS), f32, math.log(1e-3), math.log(1e-1)))
    a_log = jnp.log(jax.random.uniform(next(ks), (N_AB, 2, A_HEADS), f32, 1.0, 16.0))
    return {
        "x_prompt": nrm((BATCH, SEQ, D)),
        "x_sample": nrm((DEC_BATCH, DEC_SEQ, D)),
        "state_delta": nrm((DEC_BATCH, N_AB, 2, A_HEADS, A_DK, A_DV), 0.3),
        "state_ret": nrm((DEC_BATCH, N_AB, 2, B_HEADS, B_DK, B_DV), 1.0),
        "c": nrm((DEC_BATCH, D)),
        "c_ctx": nrm((D,)),
        "mod_w": nrm((DEPTH, D, 6 * D), 0.5 * D ** -0.5),
        "mod_b": nrm((DEPTH, 6 * D), 0.02),
        "norm1": gain((DEPTH, D)),
        "norm2": gain((DEPTH, D)),
        "ab_w_in": nrm((N_AB, D, AB_IN), D ** -0.5),
        "ab_conv": nrm((N_AB, SHORT_K, A_CONV_W), SHORT_K ** -0.5),
        "ab_a_log": a_log,
        "ab_dt_bias": dt + jnp.log(-jnp.expm1(-dt)),
        "ab_norm_a": gain((N_AB, A_DV)),
        "ab_norm_b": gain((N_AB, B_DV)),
        "ab_ret_decay": ret_base[None, None, :] + nrm((N_AB, 2, B_HEADS), 0.05),
        "ab_w_out": nrm((N_AB, MIX_W, D), MIX_W ** -0.5),
        "hy_w_in": nrm((N_HY, D, 3 * D), D ** -0.5),
        "hy_b_in": nrm((N_HY, 3 * D), 0.02),
        "hy_conv_w": nrm((N_HY, HY_SHORT_K, 3 * D), HY_SHORT_K ** -0.5),
        "hy_conv_b": nrm((N_HY, 3 * D), 0.02),
        "hy_f_w1": nrm((N_HY, HY_EMB, HY_FW), HY_EMB ** -0.5),
        "hy_f_b1": nrm((N_HY, HY_FW), 0.1),
        "hy_f_freq1": gain((N_HY, HY_FW)),
        "hy_f_w2": nrm((N_HY, HY_FW, HY_FW), HY_FW ** -0.5),
        "hy_f_b2": nrm((N_HY, HY_FW), 0.1),
        "hy_f_freq2": gain((N_HY, HY_FW)),
        "hy_f_w3": nrm((N_HY, HY_FW, 2 * D), 0.02),
        "hy_f_bias": nrm((N_HY, D), 0.5),
        "hy_w_out": nrm((N_HY, D, D), D ** -0.5),
        "hy_b_out": nrm((N_HY, D), 0.02),
        "ffn_w_gate": nrm((DEPTH, D, D_FF), D ** -0.5),
        "ffn_w_up": nrm((DEPTH, D, D_FF), D ** -0.5),
        "ffn_conv": nrm((DEPTH, FFN_K, FFN_K, D_FF), 1.0 / FFN_K),
        "ffn_conv_b": nrm((DEPTH, D_FF), 0.02),
        "ffn_w_down": nrm((DEPTH, D_FF, D), D_FF ** -0.5),
        "final_norm": gain((D,)),
    }


def reference(x_prompt, x_sample, state_delta, state_ret, c, c_ctx, mod_w, mod_b, norm1, norm2,
              ab_w_in, ab_conv, ab_a_log, ab_dt_bias, ab_norm_a, ab_norm_b, ab_ret_decay, ab_w_out,
              hy_w_in, hy_b_in, hy_conv_w, hy_conv_b, hy_f_w1, hy_f_b1, hy_f_freq1, hy_f_w2, hy_f_b2,
              hy_f_freq2, hy_f_w3, hy_f_bias, hy_w_out, hy_b_out,
              ffn_w_gate, ffn_w_up, ffn_conv, ffn_conv_b, ffn_w_down, final_norm):
    f32 = jnp.float32
    n_ctx_b = x_prompt.shape[0]
    xc, xl = x_prompt, x_sample
    sc_ctx = jax.nn.silu(c_ctx)[None, None, :]
    sc_lat = jax.nn.silu(c)[:, None, :]
    new_delta = []
    new_ret = []
    for l in range(DEPTH):
        mc = jnp.split(sc_ctx @ mod_w[l] + mod_b[l], 6, axis=-1)
        ml = jnp.split(sc_lat @ mod_w[l] + mod_b[l], 6, axis=-1)
        hc = modulate(rmsnorm(xc, norm1[l]), mc[0], mc[1])
        hl = modulate(rmsnorm(xl, norm1[l]), ml[0], ml[1])
        j = l // 2
        if l % 2 == 0:
            ab = (ab_w_in[j], ab_conv[j], ab_a_log[j], ab_dt_bias[j], ab_norm_a[j], ab_norm_b[j],
                  ab_ret_decay[j], ab_w_out[j])
            zd = jnp.zeros((n_ctx_b, 2, A_HEADS, A_DK, A_DV), f32)
            zr = jnp.zeros((n_ctx_b, 2, B_HEADS, B_DK, B_DV), f32)
            oc, sd, sr = mixer_ab(hc, *ab, zd, zr)
            ol, _, _ = mixer_ab(hl, *ab, state_delta[:, j], state_ret[:, j])
            new_delta.append(sd)
            new_ret.append(sr)
        else:
            hy = (hy_w_in[j], hy_b_in[j], hy_conv_w[j], hy_conv_b[j], hy_f_w1[j], hy_f_b1[j], hy_f_freq1[j],
                  hy_f_w2[j], hy_f_b2[j], hy_f_freq2[j], hy_f_w3[j], hy_f_bias[j], hy_w_out[j], hy_b_out[j])
            oc = mixer_hyena(hc, *hy)
            ol = mixer_hyena(hl, *hy)
        xc = xc + mc[2] * oc
        xl = xl + ml[2] * ol
        ffn = (ffn_w_gate[l], ffn_w_up[l], ffn_conv[l], ffn_conv_b[l], ffn_w_down[l])
        hc = modulate(rmsnorm(xc, norm2[l]), mc[3], mc[4])
        hl = modulate(rmsnorm(xl, norm2[l]), ml[3], ml[4])
        xc = xc + mc[5] * conv_ffn(hc, *ffn, on_grid=False)
        xl = xl + ml[5] * conv_ffn(hl, *ffn, on_grid=True)
    y_prompt = rmsnorm(xc, final_norm)
    y_sample = rmsnorm(xl, final_norm)
    new_state_delta = jnp.stack(new_delta, axis=1)
    new_state_ret = jnp.stack(new_ret, axis=1)
    return (y_prompt, y_sample, new_state_delta, new_state_ret)
```

```python
import functools
import math

import numpy as np
import jax
import jax.numpy as jnp
from jax import lax
from jax.experimental import pallas as pl
from jax.experimental.pallas import tpu as pltpu

F32, BF16 = jnp.float32, jnp.bfloat16
HI = lax.Precision.HIGHEST

D = 1024
NCTX, LCTX = 16, 256
NLAT, LLAT = 2, 4096
DEPTH = 4
TOK_CTX = NCTX * LCTX
TOK_LAT = NLAT * LLAT
NTOK = TOK_CTX + TOK_LAT
ROWBLK = 4096
NGROUP = NTOK // ROWBLK
GRID_W = 64
CHUNK = 64
NCH = NTOK // CHUNK
EPS = 1e-6
HEADS = 4
DK = 128
HW = HEADS * DK
LANE = 128
D_FF = 2816
HY_EMB = 33
HY_BANDS = 16
HY_FW = 64
HY_TARGET = 1e-2
HY_FAST_PCT = 0.3
HY_SLOW_PCT = 1.5
FFT_N2 = 128
VMEM_LIMIT = 52 * 1024 * 1024

CB_QA, CB_KA, CB_VA, CB_ZA, CB_QB, CB_KB, CB_VB, CB_GB, CB_GATE = 0, 4, 8, 12, 16, 20, 24, 28, 32
AB_N = 33 * LANE

CTX = dict(nseq=NCTX, L=LCTX, row0=0)
LAT = dict(nseq=NLAT, L=LLAT, row0=TOK_CTX)


def _params(*sem):
    return pltpu.CompilerParams(dimension_semantics=sem, vmem_limit_bytes=VMEM_LIMIT)


def _silu(x):
    return x * jax.nn.sigmoid(x)


def _dot(a, b, precision=None):
    return jnp.dot(a, b, preferred_element_type=F32, precision=precision)


def _dot_nt(a, b):
    return lax.dot_general(a, b, (((1,), (1,)), ((), ())), preferred_element_type=F32)


def _dot_tn(a, b):
    return lax.dot_general(a, b, (((0,), (0,)), ((), ())), preferred_element_type=F32)


def _pick_lane(x, lane):
    li = lax.broadcasted_iota(jnp.int32, x.shape, 1)
    return jnp.sum(jnp.where(li == lane, x, 0.0), axis=-1, keepdims=True)


def _mod_kernel(c_ref, w_ref, b_ref, o_ref):
    s = _silu(c_ref[...])
    o_ref[...] = _dot(s.astype(BF16), w_ref[...].astype(BF16)) + b_ref[...]


def _mod_all(cvec, mod_w, mod_b):
    tn = 1536
    return pl.pallas_call(
        _mod_kernel,
        grid=(DEPTH, 6 * D // tn),
        in_specs=[pl.BlockSpec((8, D), lambda l, j: (0, 0)),
                  pl.BlockSpec((None, D, tn), lambda l, j: (l, 0, j)),
                  pl.BlockSpec((None, 1, tn), lambda l, j: (l, 0, j))],
        out_specs=pl.BlockSpec((None, 8, tn), lambda l, j: (l, 0, j)),
        out_shape=jax.ShapeDtypeStruct((DEPTH, 8, 6 * D), F32),
        compiler_params=_params("arbitrary", "arbitrary"),
        name="mod",
    )(cvec, mod_w, mod_b.reshape(DEPTH, 1, 6 * D))


NMM_TM = 1024


def _nmm_kernel(x_ref, g_ref, sh_ref, sc_ref, w_ref, *rest, has_bias):
    if has_bias:
        b_ref, o_ref, h_ref = rest
    else:
        o_ref, h_ref = rest

    @pl.when(pl.program_id(1) == 0)
    def _():
        x = x_ref[...]
        y = x * lax.rsqrt(jnp.mean(x * x, axis=-1, keepdims=True) + EPS) * g_ref[...]
        h_ref[...] = (y * (1.0 + sc_ref[...]) + sh_ref[...]).astype(BF16)

    acc = _dot(h_ref[...], w_ref[...])
    if has_bias:
        acc = acc + b_ref[...]
    o_ref[...] = acc


def _nmm(x, gain, shift, scale, w, bias, tn, name):
    n = w.shape[1]
    grp = lambda i, j: (i * NMM_TM // ROWBLK, 0, 0)
    in_specs = [pl.BlockSpec((NMM_TM, D), lambda i, j: (i, 0)),
                pl.BlockSpec((1, D), lambda i, j: (0, 0)),
                pl.BlockSpec((None, 1, D), grp),
                pl.BlockSpec((None, 1, D), grp),
                pl.BlockSpec((D, tn), lambda i, j: (0, j))]
    args = [x, gain.reshape(1, D), shift, scale, w]
    if bias is not None:
        in_specs.append(pl.BlockSpec((1, tn), lambda i, j: (0, j)))
        args.append(bias.reshape(1, n))
    return pl.pallas_call(
        functools.partial(_nmm_kernel, has_bias=bias is not None),
        grid=(NTOK // NMM_TM, n // tn),
        in_specs=in_specs,
        out_specs=pl.BlockSpec((NMM_TM, tn), lambda i, j: (i, j)),
        out_shape=jax.ShapeDtypeStruct((NTOK, n), F32),
        scratch_shapes=[pltpu.VMEM((NMM_TM, D), BF16)],
        compiler_params=_params("arbitrary", "arbitrary"),
        name=name,
    )(*args)


MMR_TM = 512


def _mmr_kernel(*refs, n_a, has_bias, final):
    a_refs, w_refs = refs[:n_a], refs[n_a:2 * n_a]
    rest = list(refs[2 * n_a:])
    b_ref = rest.pop(0) if has_bias else None
    x_ref, gate_ref = rest.pop(0), rest.pop(0)
    fn_ref = rest.pop(0) if final else None
    o_ref = rest.pop(0)
    acc = _dot(a_refs[0][...].astype(BF16), w_refs[0][...])
    for a_ref, w_ref in zip(a_refs[1:], w_refs[1:]):
        acc = acc + _dot(a_ref[...].astype(BF16), w_ref[...])
    if has_bias:
        acc = acc + b_ref[...]
    y = x_ref[...] + gate_ref[...] * acc
    if final:
        y = y * lax.rsqrt(jnp.mean(y * y, axis=-1, keepdims=True) + EPS) * fn_ref[...]
    o_ref[...] = y


def _mmr(a_list, w_list, bias, x, gate, final_gain, name):
    grp = lambda i: (i * MMR_TM // ROWBLK, 0, 0)
    in_specs, args = [], []
    for a in a_list:
        in_specs.append(pl.BlockSpec((MMR_TM, a.shape[1]), lambda i: (i, 0)))
        args.append(a)
    for w in w_list:
        in_specs.append(pl.BlockSpec(w.shape, lambda i: (0, 0)))
        args.append(w)
    if bias is not None:
        in_specs.append(pl.BlockSpec((1, D), lambda i: (0, 0)))
        args.append(bias.reshape(1, D))
    in_specs += [pl.BlockSpec((MMR_TM, D), lambda i: (i, 0)), pl.BlockSpec((None, 1, D), grp)]
    args += [x, gate]
    if final_gain is not None:
        in_specs.append(pl.BlockSpec((1, D), lambda i: (0, 0)))
        args.append(final_gain.reshape(1, D))
    return pl.pallas_call(
        functools.partial(_mmr_kernel, n_a=len(a_list), has_bias=bias is not None,
                          final=final_gain is not None),
        grid=(NTOK // MMR_TM,),
        in_specs=in_specs,
        out_specs=pl.BlockSpec((MMR_TM, D), lambda i: (i, 0)),
        out_shape=jax.ShapeDtypeStruct((NTOK, D), F32),
        compiler_params=_params("arbitrary"),
        name=name,
    )(*args)


def _seq_shifts(x, i):
    lseq = jnp.where(i == 0, LCTX, LLAT)
    pos = lax.broadcasted_iota(jnp.int32, x.shape, 0) & (lseq - 1)
    prev = jnp.where(pos == 0, 0.0, pltpu.roll(x, 1, 0))
    nxt = jnp.where(pos == lseq - 1, 0.0, pltpu.roll(x, ROWBLK - 1, 0))
    return prev, nxt


def _conv3(x, w, i):
    prev, nxt = _seq_shifts(x, i)
    return prev * w[0:1] + x * w[1:2] + nxt * w[2:3]


def _ab_conv_kernel(x_ref, w_ref, o_ref):
    i, j = pl.program_id(0), pl.program_id(1)
    y = _silu(_conv3(x_ref[...], w_ref[...], i))
    nrm = y * lax.rsqrt(jnp.sum(y * y, axis=-1, keepdims=True) + EPS)
    nrm = nrm * jnp.where(j < CB_KA, DK ** -0.5, 1.0)
    o_ref[...] = jnp.where(j < CB_VA, nrm, y)


def _ab_conv(proj, conv_w):
    ncb = 3 * HEADS
    return pl.pallas_call(
        _ab_conv_kernel,
        grid=(NGROUP, ncb),
        in_specs=[pl.BlockSpec((ROWBLK, LANE), lambda i, j: (i, j)),
                  pl.BlockSpec((3, LANE), lambda i, j: (0, j))],
        out_specs=pl.BlockSpec((ROWBLK, LANE), lambda i, j: (i, j)),
        out_shape=jax.ShapeDtypeStruct((NTOK, ncb * LANE), F32),
        compiler_params=_params("arbitrary", "arbitrary"),
        name="ab_conv",
    )(proj, conv_w)


def _gates_kernel(x_ref, alog_ref, dtb_ref, o_ref):
    x = x_ref[...]
    lane = lax.broadcasted_iota(jnp.int32, x.shape, 1)
    row = lax.broadcasted_iota(jnp.int32, x.shape, 0) & (CHUNK - 1)
    t = x + dtb_ref[...]
    softplus = jnp.maximum(t, 0.0) + jnp.log(1.0 + jnp.exp(-jnp.abs(t)))
    g = -jnp.exp(alog_ref[...]) * softplus
    pre, suf = g, g
    s = 1
    while s < CHUNK:
        pre = pre + jnp.where(row >= s, pltpu.roll(pre, s, 0), 0.0)
        suf = suf + jnp.where(row < CHUNK - s, pltpu.roll(suf, ROWBLK - s, 0), 0.0)
        s *= 2
    gc = jnp.where(lane < HEADS, pre, suf)
    o_ref[...] = jnp.where(lane < 2 * HEADS, gc, jax.nn.sigmoid(x))


def _gates(proj, a_log, dt_bias):
    pad = lambda v: jnp.pad(v.reshape(1, 2 * HEADS), ((0, 0), (0, LANE - 2 * HEADS)))
    return pl.pallas_call(
        _gates_kernel,
        grid=(NGROUP,),
        in_specs=[pl.BlockSpec((ROWBLK, LANE), lambda i: (i, CB_GATE)),
                  pl.BlockSpec((1, LANE), lambda i: (0, 0)),
                  pl.BlockSpec((1, LANE), lambda i: (0, 0))],
        out_specs=pl.BlockSpec((ROWBLK, LANE), lambda i: (i, 0)),
        out_shape=jax.ShapeDtypeStruct((NTOK, LANE), F32),
        compiler_params=_params("arbitrary"),
        name="ab_gates",
    )(proj, pad(a_log), pad(dt_bias))


DP_RB = 256
DP_NC = DP_RB // CHUNK


def _unit_tri_inv(lm):
    ii = lax.broadcasted_iota(jnp.int32, lm.shape, 0)
    jj = lax.broadcasted_iota(jnp.int32, lm.shape, 1)
    q = -lm
    p = jnp.where(ii == jj, 1.0, 0.0) + q
    for _ in range(int(math.log2(CHUNK)) - 1):
        q = _dot(q, q, HI)
        p = p + _dot(p, q, HI)
    return p


def _delta_prep_kernel(q_ref, k_ref, v_ref, g_ref, gt_ref, u_ref, wq_ref, ak_ref, gl_ref):
    h = pl.program_id(1)
    ii = lax.broadcasted_iota(jnp.int32, (CHUNK, CHUNK), 0)
    jj = lax.broadcasted_iota(jnp.int32, (CHUNK, CHUNK), 1)
    for c in range(DP_NC):
        rows = slice(c * CHUNK, (c + 1) * CHUNK)
        q, k, v, gts = q_ref[rows, :], k_ref[rows, :], v_ref[rows, :], g_ref[rows, :]
        qk = _dot_nt(q.astype(BF16), k.astype(BF16))
        for d in range(2):
            incl = (ii >= jj) if d == 0 else (ii <= jj)
            strict = (ii > jj) if d == 0 else (ii < jj)
            gcol = _pick_lane(gts, d * HEADS + h)
            bcol = _pick_lane(gts, 2 * HEADS + d * HEADS + h)
            grow = gt_ref[d * HEADS + h][c:c + 1, :]
            diff = gcol - grow
            dmask = jnp.where(incl, jnp.exp(jnp.where(incl, diff, 0.0)), 0.0)
            kb = k * bcol
            lm = jnp.where(strict, _dot_nt(kb.astype(BF16), k.astype(BF16)) * dmask, 0.0)
            tmat = _unit_tri_inv(lm).astype(BF16)
            gam = jnp.exp(gcol)
            u = _dot(tmat, (v * bcol).astype(BF16))
            w = _dot(tmat, (kb * gam).astype(BF16))
            attn = jnp.where(incl, qk * dmask, 0.0)
            gtot = gcol[CHUNK - 1:CHUNK, :] if d == 0 else gcol[0:1, :]
            kd = k * jnp.exp(gtot - gcol)
            u_ref[d, rows, :] = u
            wq_ref[d, c] = jnp.concatenate([w, q * gam], axis=0).astype(BF16)
            ak_ref[d, c] = jnp.concatenate([attn, kd.T], axis=0).astype(BF16)
            gl_ref[d, c] = jnp.broadcast_to(jnp.exp(gtot), (1, LANE))


def _delta_prep(qkv, gates, gates_t):
    nrb = NTOK // DP_RB
    blk = lambda cb: pl.BlockSpec((DP_RB, LANE), lambda i, h: (i, cb + h))
    return pl.pallas_call(
        _delta_prep_kernel,
        grid=(nrb, HEADS),
        in_specs=[blk(CB_QA), blk(CB_KA), blk(CB_VA),
                  pl.BlockSpec((DP_RB, LANE), lambda i, h: (i, 0)),
                  pl.BlockSpec((None, 2 * HEADS, DP_NC, CHUNK), lambda i, h: (i, 0, 0, 0))],
        out_specs=[pl.BlockSpec((2, DP_RB, LANE), lambda i, h: (0, i, h)),
                   pl.BlockSpec((2, None, DP_NC, 2 * CHUNK, DK), lambda i, h: (0, h, i, 0, 0)),
                   pl.BlockSpec((2, None, DP_NC, CHUNK + DK, CHUNK), lambda i, h: (0, h, i, 0, 0)),
                   pl.BlockSpec((2, None, DP_NC, 1, LANE), lambda i, h: (0, h, i, 0, 0))],
        out_shape=[jax.ShapeDtypeStruct((2, NTOK, HW), F32),
                   jax.ShapeDtypeStruct((2, HEADS, NCH, 2 * CHUNK, DK), BF16),
                   jax.ShapeDtypeStruct((2, HEADS, NCH, CHUNK + DK, CHUNK), BF16),
                   jax.ShapeDtypeStruct((2, HEADS, NCH, 1, LANE), F32)],
        compiler_params=_params("arbitrary", "arbitrary"),
        name="delta_prep",
    )(qkv, qkv, qkv, gates, gates_t)


def _delta_scan_kernel(*refs, L, zero_init, aliased):
    refs = list(refs)
    u_ref, wq_ref, ak_ref, gl_ref, z_ref, ng_ref = [refs.pop(0) for _ in range(6)]
    s0_ref = None if zero_init else refs.pop(0)
    if aliased:
        refs.pop(0)
    o_ref, sfin_ref, o_scr = refs
    n = L // CHUNK

    def body(c, carry):
        new = []
        for d in range(2):
            s = carry[d]
            cc = c if d == 0 else n - 1 - c
            r0 = pl.multiple_of(cc * CHUNK, CHUNK)
            r1 = _dot(wq_ref[d, cc], s.astype(BF16))
            vnew = u_ref[d, pl.ds(r0, CHUNK), :] - r1[:CHUNK]
            r2 = _dot(ak_ref[d, cc], vnew.astype(BF16))
            o_scr[d, pl.ds(r0, CHUNK), :] = r1[CHUNK:] + r2[:CHUNK]
            new.append(s * gl_ref[d, cc] + r2[CHUNK:])
        return tuple(new)

    if zero_init:
        init = (jnp.zeros((DK, DK), F32), jnp.zeros((DK, DK), F32))
    else:
        init = (s0_ref[0], s0_ref[1])
    fin = lax.fori_loop(0, n, body, init)
    sfin_ref[0] = fin[0]
    sfin_ref[1] = fin[1]
    o = o_scr[0] + o_scr[1]
    o = o * lax.rsqrt(jnp.mean(o * o, axis=-1, keepdims=True) + EPS) * ng_ref[...]
    o_ref[...] = o * _silu(z_ref[...])


def _delta_scan(stream, u, wq, ak, gl, proj, norm_gain, s0, prev_out):
    nseq, L = stream["nseq"], stream["L"]
    rb0 = stream["row0"] // L
    n = L // CHUNK
    in_specs = [pl.BlockSpec((2, L, LANE), lambda s, h: (0, rb0 + s, h)),
                pl.BlockSpec((2, None, n, 2 * CHUNK, DK), lambda s, h: (0, h, rb0 + s, 0, 0)),
                pl.BlockSpec((2, None, n, CHUNK + DK, CHUNK), lambda s, h: (0, h, rb0 + s, 0, 0)),
                pl.BlockSpec((2, None, n, 1, LANE), lambda s, h: (0, h, rb0 + s, 0, 0)),
                pl.BlockSpec((L, LANE), lambda s, h: (rb0 + s, CB_ZA + h)),
                pl.BlockSpec((1, LANE), lambda s, h: (0, 0))]
    args = [u, wq, ak, gl, proj, norm_gain.reshape(1, DK)]
    if s0 is not None:
        in_specs.append(pl.BlockSpec((None, 2, None, DK, DK), lambda s, h: (s, 0, h, 0, 0)))
        args.append(s0)
    aliases = {}
    if prev_out is not None:
        in_specs.append(pl.BlockSpec(memory_space=pl.ANY))
        aliases = {len(args): 0}
        args.append(prev_out)
    return pl.pallas_call(
        functools.partial(_delta_scan_kernel, L=L, zero_init=s0 is None, aliased=prev_out is not None),
        grid=(nseq, HEADS),
        in_specs=in_specs,
        out_specs=[pl.BlockSpec((L, LANE), lambda s, h: (rb0 + s, h)),
                   pl.BlockSpec((None, 2, None, DK, DK), lambda s, h: (s, 0, h, 0, 0))],
        out_shape=[jax.ShapeDtypeStruct((NTOK, HW), F32),
                   jax.ShapeDtypeStruct((nseq, 2, HEADS, DK, DK), F32)],
        scratch_shapes=[pltpu.VMEM((2, L, LANE), F32)],
        input_output_aliases=aliases,
        compiler_params=_params("arbitrary", "arbitrary"),
        name="delta_scan_%d" % L,
    )(*args)


def _ret_scan_kernel(*refs, L, zero_init, aliased):
    refs = list(refs)
    q_ref, k_ref, v_ref, gb_ref, dec_ref, ng_ref = [refs.pop(0) for _ in range(6)]
    r0_ref = None if zero_init else refs.pop(0)
    if aliased:
        refs.pop(0)
    o_ref, rfin_ref, o_scr = refs
    h = pl.program_id(1)
    n = L // CHUNK
    ii = lax.broadcasted_iota(jnp.int32, (CHUNK, CHUNK), 0)
    jj = lax.broadcasted_iota(jnp.int32, (CHUNK, CHUNK), 1)
    ci = lax.broadcasted_iota(jnp.int32, (CHUNK, 1), 0).astype(F32)
    consts = []
    for d in range(2):
        lg = -jnp.exp(_pick_lane(dec_ref[...], d * HEADS + h))
        dist = (ii - jj) if d == 0 else (jj - ii)
        dmat = jnp.where(dist >= 0, jnp.exp(jnp.maximum(dist, 0).astype(F32) * lg), 0.0)
        cross = jnp.exp(((ci + 1.0) if d == 0 else (CHUNK - ci)) * lg)
        sdec = jnp.exp(((CHUNK - 1.0 - ci) if d == 0 else ci) * lg)
        consts.append((dmat, cross, sdec, jnp.exp(CHUNK * lg)))

    def body(c, carry):
        new = []
        for d in range(2):
            dmat, cross, sdec, cdec = consts[d]
            r = carry[d]
            cc = c if d == 0 else n - 1 - c
            rows = pl.ds(pl.multiple_of(cc * CHUNK, CHUNK), CHUNK)
            q = q_ref[rows, :] * DK ** -0.5
            k, v = k_ref[rows, :], v_ref[rows, :]
            vb = v.astype(BF16)
            inner = _dot_nt(q.astype(BF16), k.astype(BF16)) * dmat
            o_scr[d, rows, :] = _dot(inner.astype(BF16), vb) + _dot((q * cross).astype(BF16), r.astype(BF16))
            new.append(r * cdec + _dot_tn((k * sdec).astype(BF16), vb))
        return tuple(new)

    if zero_init:
        init = (jnp.zeros((DK, DK), F32), jnp.zeros((DK, DK), F32))
    else:
        init = (r0_ref[0], r0_ref[1])
    fin = lax.fori_loop(0, n, body, init)
    rfin_ref[0] = fin[0]
    rfin_ref[1] = fin[1]
    o = o_scr[0] + o_scr[1]
    mu = jnp.mean(o, axis=-1, keepdims=True)
    var = jnp.mean(jnp.square(o - mu), axis=-1, keepdims=True)
    o = (o - mu) * lax.rsqrt(var + EPS) * ng_ref[...]
    o_ref[...] = o * _silu(gb_ref[...])


def _ret_scan(stream, proj, ret_decay, norm_gain, r0, prev_out):
    nseq, L = stream["nseq"], stream["L"]
    rb0 = stream["row0"] // L
    blk = lambda cb: pl.BlockSpec((L, LANE), lambda s, h: (rb0 + s, cb + h))
    in_specs = [blk(CB_QB), blk(CB_KB), blk(CB_VB), blk(CB_GB),
                pl.BlockSpec((1, LANE), lambda s, h: (0, 0)),
                pl.BlockSpec((1, LANE), lambda s, h: (0, 0))]
    dec = jnp.pad(ret_decay.reshape(1, 2 * HEADS), ((0, 0), (0, LANE - 2 * HEADS)))
    args = [proj, proj, proj, proj, dec, norm_gain.reshape(1, DK)]
    if r0 is not None:
        in_specs.append(pl.BlockSpec((None, 2, None, DK, DK), lambda s, h: (s, 0, h, 0, 0)))
        args.append(r0)
    aliases = {}
    if prev_out is not None:
        in_specs.append(pl.BlockSpec(memory_space=pl.ANY))
        aliases = {len(args): 0}
        args.append(prev_out)
    return pl.pallas_call(
        functools.partial(_ret_scan_kernel, L=L, zero_init=r0 is None, aliased=prev_out is not None),
        grid=(nseq, HEADS),
        in_specs=in_specs,
        out_specs=[pl.BlockSpec((L, LANE), lambda s, h: (rb0 + s, h)),
                   pl.BlockSpec((None, 2, None, DK, DK), lambda s, h: (s, 0, h, 0, 0))],
        out_shape=[jax.ShapeDtypeStruct((NTOK, HW), F32),
                   jax.ShapeDtypeStruct((nseq, 2, HEADS, DK, DK), F32)],
        scratch_shapes=[pltpu.VMEM((2, L, LANE), F32)],
        input_output_aliases=aliases,
        compiler_params=_params("arbitrary", "arbitrary"),
        name="ret_scan_%d" % L,
    )(*args)


def _hy_conv_kernel(x0_ref, x1_ref, v_ref, w0_ref, w1_ref, wv_ref, b0_ref, b1_ref, bv_ref, x0o_ref, p_ref):
    i = pl.program_id(0)
    x0o_ref[...] = _conv3(x0_ref[...], w0_ref[...], i) + b0_ref[...]
    x1 = _conv3(x1_ref[...], w1_ref[...], i) + b1_ref[...]
    v = _conv3(v_ref[...], wv_ref[...], i) + bv_ref[...]
    p_ref[...] = v * x1


def _hy_conv(u3, conv_w, conv_b):
    ncb = D // LANE
    xb = lambda k: pl.BlockSpec((ROWBLK, LANE), lambda i, j: (i, k * ncb + j))
    wb = lambda k: pl.BlockSpec((3, LANE), lambda i, j: (0, k * ncb + j))
    bb = lambda k: pl.BlockSpec((1, LANE), lambda i, j: (0, k * ncb + j))
    cb = conv_b.reshape(1, 3 * D)
    return pl.pallas_call(
        _hy_conv_kernel,
        grid=(NGROUP, ncb),
        in_specs=[xb(0), xb(1), xb(2), wb(0), wb(1), wb(2), bb(0), bb(1), bb(2)],
        out_specs=[pl.BlockSpec((ROWBLK, LANE), lambda i, j: (i, j))] * 2,
        out_shape=[jax.ShapeDtypeStruct((NTOK, D), F32)] * 2,
        compiler_params=_params("arbitrary", "arbitrary"),
        name="hy_conv",
    )(u3, u3, u3, conv_w, conv_w, conv_w, cb, cb, cb)


def _filter_features(L):
    r = np.arange(2 * L)
    pos = np.where(r < L, r, 2 * L - r) % L
    t = pos / (L - 1.0)
    bands = np.linspace(1e-4, HY_BANDS - 1, HY_BANDS)
    ang = 2.0 * np.pi * np.outer(pos, bands) / L
    z = np.zeros((2 * L, HY_FW), np.float64)
    z[:, 0] = t
    z[:, 1:1 + HY_BANDS] = np.cos(ang)
    z[:, 1 + HY_BANDS:HY_EMB] = -np.sin(ang)
    z[:, HY_EMB] = (r != L)
    return z.astype(np.float32)


def _filter_kernel(z_ref, w1_ref, b1_ref, f1_ref, w2_ref, b2_ref, f2_ref, w3_ref, dl_ref, o_ref):
    z = z_ref[...]
    hid = jnp.sin(f1_ref[...] * (_dot(z, w1_ref[...], HI) + b1_ref[...]))
    hid = jnp.sin(f2_ref[...] * (_dot(hid, w2_ref[...], HI) + b2_ref[...]))
    filt = _dot(hid, w3_ref[...], HI)
    window = jnp.exp(-z[:, 0:1] * dl_ref[...]) * z[:, HY_EMB:HY_EMB + 1]
    o_ref[...] = filt * window


def _hy_filter(L, w1, b1, f1, w2, b2, f2, w3):
    rb = min(512, L)
    nblk = 2 * L // rb
    z = jnp.asarray(_filter_features(L))
    w1p = jnp.pad(w1, ((0, HY_FW - HY_EMB), (0, 0)))
    min_decay = math.log(HY_TARGET) / HY_SLOW_PCT
    max_decay = math.log(HY_TARGET) / HY_FAST_PCT
    deltas = jnp.asarray(np.abs(np.linspace(min_decay, max_decay, D)).astype(np.float32).reshape(1, D))
    vec = lambda v: v.reshape(1, HY_FW)
    full = lambda shp: pl.BlockSpec(shp, lambda i: (0, 0))
    return pl.pallas_call(
        _filter_kernel,
        grid=(nblk,),
        in_specs=[pl.BlockSpec((rb, HY_FW), lambda i: (i, 0)),
                  full((HY_FW, HY_FW)), full((1, HY_FW)), full((1, HY_FW)),
                  full((HY_FW, HY_FW)), full((1, HY_FW)), full((1, HY_FW)),
                  pl.BlockSpec((HY_FW, D), lambda i: (0, i // (nblk // 2))),
                  full((1, D))],
        out_specs=pl.BlockSpec((rb, D), lambda i: (i, 0)),
        out_shape=jax.ShapeDtypeStruct((2 * L, D), F32),
        compiler_params=_params("arbitrary"),
        name="hy_filter_%d" % L,
    )(z, w1p, vec(b1), vec(f1), w2, vec(b2), vec(f2), w3, deltas)


FFT_TC = 8192


def _cis(num, den):
    ang = -2.0 * np.pi * (num % den) / den
    return np.cos(ang), np.sin(ang)


def _fft_consts(L):
    n = 2 * L
    n1 = n // FFT_N2
    k1 = np.arange(n1)
    fr, fi = _cis(np.outer(k1, k1), n1)
    half = n1 // 2
    sig = np.block([[fr[:, :half], -fi[:, :half]], [fi[:, :half], fr[:, :half]]])
    ker = np.concatenate([fr, fi], axis=0)
    cr, ci = fr[:, :half].T, -fi[:, :half].T
    inv = np.stack([np.concatenate([cr, -ci], axis=1), np.concatenate([ci, cr], axis=1)]) / n
    k2 = np.arange(FFT_N2)
    gr, gi = _cis(np.outer(k2, k2), FFT_N2)
    f2 = np.block([[gr, -gi], [gi, gr]])
    f2inv = np.block([[gr, gi], [-gi, gr]])
    tr, ti = _cis(np.outer(k1, k2), n)
    f = lambda a: np.asarray(a, np.float32)
    return dict(n1=n1, sig=f(sig), ker=f(ker), inv=f(inv), f2=f(f2), f2inv=f(f2inv),
                twr=f(tr).reshape(n1, FFT_N2, 1), twi=f(ti).reshape(n1, FFT_N2, 1))


def _lin_rows(mat, rows):
    out = []
    for m in range(mat.shape[0]):
        acc = None
        for k in range(mat.shape[1]):
            cf = float(mat[m, k])
            if abs(cf) < 1e-9:
                continue
            term = rows[k] if abs(cf - 1.0) < 1e-9 else (-rows[k] if abs(cf + 1.0) < 1e-9 else cf * rows[k])
            acc = term if acc is None else acc + term
        out.append(jnp.zeros_like(rows[0]) if acc is None else acc)
    return out


def _fft1_mxu_kernel(m_ref, *refs):
    o_ref = refs[-1]
    x = jnp.concatenate([r[...] for r in refs[:-1]], axis=0) if len(refs) > 2 else refs[0][...]
    y = _dot(m_ref[...], x, HI)
    n1 = y.shape[0] // 2
    o_ref[0] = y[:n1]
    o_ref[1] = y[n1:]


def _fft1_vpu_kernel(*refs, mat, split):
    o_ref = refs[-1]
    if split:
        x = refs[0]
        rows = [x[s, r:r + 1, :] for s in range(2) for r in range(x.shape[1])]
    else:
        x = refs[0]
        rows = [x[r:r + 1, :] for r in range(x.shape[0])]
    out = _lin_rows(mat, rows)
    n1 = len(out) // 2
    for m, row in enumerate(out):
        o_ref[m // n1, m % n1:m % n1 + 1, :] = row


def _fft_stage1(stream, consts, x, is_kernel):
    L, n1 = stream["L"], consts["n1"]
    cols = FFT_N2 * D
    npairs = 1 if is_kernel else stream["nseq"] // 2
    out_shape = jax.ShapeDtypeStruct((npairs, 2, n1, cols), F32)
    out_spec = pl.BlockSpec((None, 2, n1, FFT_TC), lambda p, j: (p, 0, 0, j))
    grid = (npairs, cols // FFT_TC)
    name = "fft1_%s_%d" % ("k" if is_kernel else "x", L)
    if n1 >= 8:
        mat = jnp.asarray(consts["ker"] if is_kernel else consts["sig"])
        mspec = pl.BlockSpec(mat.shape, lambda p, j: (0, 0))
        if is_kernel:
            xs = [x.reshape(n1, cols)]
            specs = [pl.BlockSpec((n1, FFT_TC), lambda p, j: (0, j))]
        else:
            assert npairs == 1
            xv = x.reshape(NTOK // (n1 // 2 * FFT_N2), n1 // 2, cols)
            b0 = stream["row0"] // L
            xs = [xv, xv]
            specs = [pl.BlockSpec((None, n1 // 2, FFT_TC), lambda p, j: (b0, 0, j)),
                     pl.BlockSpec((None, n1 // 2, FFT_TC), lambda p, j: (b0 + 1, 0, j))]
        return pl.pallas_call(_fft1_mxu_kernel, grid=grid, in_specs=[mspec] + specs, out_specs=out_spec,
                              out_shape=out_shape, compiler_params=_params("arbitrary", "arbitrary"),
                              name=name)(mat, *xs)
    if is_kernel:
        xs = x.reshape(n1, cols)
        spec = pl.BlockSpec((n1, FFT_TC), lambda p, j: (0, j))
        mat = consts["ker"]
    else:
        assert stream["row0"] == 0
        xs = x.reshape(NTOK // (n1 // 2 * FFT_N2), n1 // 2, cols)
        spec = pl.BlockSpec((2, n1 // 2, FFT_TC), lambda p, j: (p, 0, j))
        mat = consts["sig"]
    return pl.pallas_call(functools.partial(_fft1_vpu_kernel, mat=mat, split=not is_kernel),
                          grid=grid, in_specs=[spec], out_specs=out_spec, out_shape=out_shape,
                          compiler_params=_params("arbitrary", "arbitrary"), name=name)(xs)


def _fft2_kernel(a_ref, twr_ref, twi_ref, f2_ref, *rest, conv):
    ar, ai = a_ref[0], a_ref[1]
    twr, twi = twr_ref[...], twi_ref[...]
    x = jnp.concatenate([ar * twr - ai * twi, ar * twi + ai * twr], axis=0)
    y = _dot(f2_ref[...], x, HI)
    yr, yi = y[:FFT_N2], y[FFT_N2:]
    if not conv:
        o_ref, = rest
        o_ref[0] = yr
        o_ref[1] = yi
        return
    kf_ref, f2inv_ref, o_ref = rest
    kr, ki = kf_ref[0], kf_ref[1]
    z = jnp.concatenate([yr * kr - yi * ki, yr * ki + yi * kr], axis=0)
    w = _dot(f2inv_ref[...], z, HI)
    wr, wi = w[:FFT_N2], w[FFT_N2:]
    o_ref[0] = wr * twr + wi * twi
    o_ref[1] = wi * twr - wr * twi


def _fft_stage2(stream, consts, a, kf):
    n1 = consts["n1"]
    npairs = a.shape[0]
    a5 = a.reshape(npairs, 2, n1, FFT_N2, D)
    slab = pl.BlockSpec((None, 2, None, FFT_N2, D), lambda p, k: (p, 0, k, 0, 0))
    tw = pl.BlockSpec((None, FFT_N2, 1), lambda p, k: (k, 0, 0))
    mat = pl.BlockSpec((2 * FFT_N2, 2 * FFT_N2), lambda p, k: (0, 0))
    in_specs = [slab, tw, tw, mat]
    args = [a5, jnp.asarray(consts["twr"]), jnp.asarray(consts["twi"]), jnp.asarray(consts["f2"])]
    if kf is not None:
        in_specs += [pl.BlockSpec((None, 2, None, FFT_N2, D), lambda p, k: (0, 0, k, 0, 0)), mat]
        args += [kf.reshape(1, 2, n1, FFT_N2, D), jnp.asarray(consts["f2inv"])]
    out = pl.pallas_call(
        functools.partial(_fft2_kernel, conv=kf is not None),
        grid=(npairs, n1),
        in_specs=in_specs,
        out_specs=slab,
        out_shape=jax.ShapeDtypeStruct(a5.shape, F32),
        compiler_params=_params("arbitrary", "arbitrary"),
        name="fft2_%s_%d" % ("conv" if kf is not None else "spec", stream["L"]),
    )(*args)
    return out.reshape(a.shape)


def _fft3_mxu_kernel(m_ref, b_ref, p_ref, x0_ref, bias_ref, *rest):
    o_ref = rest[-1]
    n1 = b_ref.shape[1]
    y = _dot(m_ref[...], b_ref[...].reshape(2 * n1, b_ref.shape[2]), HI)
    o_ref[...] = (y + p_ref[...] * bias_ref[...]) * x0_ref[...]


def _fft3_vpu_kernel(b_ref, p_ref, x0_ref, bias_ref, o_ref, *, mat):
    n1 = b_ref.shape[1]
    rows = [b_ref[c, r:r + 1, :] for c in range(2) for r in range(n1)]
    out = _lin_rows(mat, rows)
    half = n1 // 2
    bias = bias_ref[...]
    for m, row in enumerate(out):
        s, r = m // half, m % half
        o_ref[s, r:r + 1, :] = (row + p_ref[s, r:r + 1, :] * bias) * x0_ref[s, r:r + 1, :]


def _fft_stage3(stream, consts, b, p, x0, bias, prev_out):
    L, n1 = stream["L"], consts["n1"]
    half = n1 // 2
    cols = FFT_N2 * D
    npairs = b.shape[0]
    bias_t = jnp.tile(bias.reshape(1, D), (1, FFT_TC // D))
    view = (NTOK // (half * FFT_N2), half, cols)
    pv, xv = p.reshape(view), x0.reshape(view)
    out_shape = jax.ShapeDtypeStruct(view, F32)
    name = "fft3_%d" % L
    if n1 >= 8:
        assert npairs == 1
        b0 = stream["row0"] // L
        seq = pl.BlockSpec((None, half, FFT_TC), lambda j, s: (b0 + s, 0, j))
        in_specs = [pl.BlockSpec((None, half, 2 * n1), lambda j, s: (s, 0, 0)),
                    pl.BlockSpec((None, 2, n1, FFT_TC), lambda j, s: (0, 0, 0, j)),
                    seq, seq, pl.BlockSpec((1, FFT_TC), lambda j, s: (0, 0))]
        args = [jnp.asarray(consts["inv"]), b, pv, xv, bias_t]
        aliases = {}
        if prev_out is not None:
            in_specs.append(pl.BlockSpec(memory_space=pl.ANY))
            aliases = {len(args): 0}
            args.append(prev_out.reshape(view))
        out = pl.pallas_call(_fft3_mxu_kernel, grid=(cols // FFT_TC, 2), in_specs=in_specs, out_specs=seq,
                             out_shape=out_shape, input_output_aliases=aliases,
                             compiler_params=_params("arbitrary", "arbitrary"), name=name)(*args)
        return out.reshape(NTOK, D)
    assert stream["row0"] == 0 and prev_out is None
    mat = np.concatenate([consts["inv"][0], consts["inv"][1]], axis=0)
    seq = pl.BlockSpec((2, half, FFT_TC), lambda pr, j: (pr, 0, j))
    out = pl.pallas_call(
        functools.partial(_fft3_vpu_kernel, mat=mat),
        grid=(npairs, cols // FFT_TC),
        in_specs=[pl.BlockSpec((None, 2, n1, FFT_TC), lambda pr, j: (pr, 0, 0, j)), seq, seq,
                  pl.BlockSpec((1, FFT_TC), lambda pr, j: (0, 0))],
        out_specs=seq, out_shape=out_shape,
        compiler_params=_params("arbitrary", "arbitrary"), name=name)(b, pv, xv, bias_t)
    return out.reshape(NTOK, D)


def _hy_long_conv(p, x0, filt_w, bias):
    out = None
    for stream in (CTX, LAT):
        consts = _fft_consts(stream["L"])
        kern = _hy_filter(stream["L"], *filt_w)
        kf = _fft_stage2(stream, consts, _fft_stage1(stream, consts, kern, True), None)
        a = _fft_stage1(stream, consts, p, False)
        b = _fft_stage2(stream, consts, a, kf)
        out = _fft_stage3(stream, consts, b, p, x0, bias, out)
    return out


def _ffn_conv_kernel(g_ref, u_ref, w_ref, b_ref, o_ref):
    i = pl.program_id(0)
    x = g_ref[...]
    w = w_ref[...]

    @pl.when(i == 0)
    def _():
        y = _conv3(x, w[3:6], i)
        o_ref[...] = _silu(y + b_ref[...]) * u_ref[...]

    @pl.when(i > 0)
    def _():
        pos = lax.broadcasted_iota(jnp.int32, x.shape, 0)
        col = pos & (GRID_W - 1)
        left = jnp.where(col == 0, 0.0, pltpu.roll(x, 1, 0))
        right = jnp.where(col == GRID_W - 1, 0.0, pltpu.roll(x, ROWBLK - 1, 0))
        acc = left * w[3:4] + x * w[4:5] + right * w[5:6]
        up = left * w[0:1] + x * w[1:2] + right * w[2:3]
        dn = left * w[6:7] + x * w[7:8] + right * w[8:9]
        acc = acc + jnp.where(pos < GRID_W, 0.0, pltpu.roll(up, GRID_W, 0))
        acc = acc + jnp.where(pos >= ROWBLK - GRID_W, 0.0, pltpu.roll(dn, ROWBLK - GRID_W, 0))
        o_ref[...] = _silu(acc + b_ref[...]) * u_ref[...]


def _ffn_conv(gu, conv_w, conv_b):
    ncb = D_FF // LANE
    return pl.pallas_call(
        _ffn_conv_kernel,
        grid=(NGROUP, ncb),
        in_specs=[pl.BlockSpec((ROWBLK, LANE), lambda i, j: (i, j)),
                  pl.BlockSpec((ROWBLK, LANE), lambda i, j: (i, ncb + j)),
                  pl.BlockSpec((9, LANE), lambda i, j: (0, j)),
                  pl.BlockSpec((1, LANE), lambda i, j: (0, j))],
        out_specs=pl.BlockSpec((ROWBLK, LANE), lambda i, j: (i, j)),
        out_shape=jax.ShapeDtypeStruct((NTOK, D_FF), F32),
        compiler_params=_params("arbitrary", "arbitrary"),
        name="ffn_conv",
    )(gu, gu, conv_w.reshape(9, D_FF), conv_b.reshape(1, D_FF))


def kernel(x_prompt, x_sample, state_delta, state_ret, c, c_ctx, mod_w, mod_b, norm1, norm2, ab_w_in, ab_conv, ab_a_log, ab_dt_bias, ab_norm_a, ab_norm_b, ab_ret_decay, ab_w_out, hy_w_in, hy_b_in, hy_conv_w, hy_conv_b, hy_f_w1, hy_f_b1, hy_f_freq1, hy_f_w2, hy_f_b2, hy_f_freq2, hy_f_w3, hy_f_bias, hy_w_out, hy_b_out, ffn_w_gate, ffn_w_up, ffn_conv, ffn_conv_b, ffn_w_down, final_norm):
    x = jnp.concatenate([x_prompt.reshape(TOK_CTX, D), x_sample.reshape(TOK_LAT, D)], axis=0)
    cvec = jnp.concatenate([c_ctx[None], c, jnp.zeros((8 - 1 - NLAT, D), F32)], axis=0)
    mod = _mod_all(cvec, mod_w, mod_b)
    new_delta, new_ret = [], []
    for l in range(DEPTH):
        m = [mod[l, :NGROUP, k * D:(k + 1) * D].reshape(NGROUP, 1, D) for k in range(6)]
        j = l // 2
        if l % 2 == 0:
            w = ab_w_in[j]
            gate_c0 = 3 * HW + HW
            w_all = jnp.concatenate([w[:, :gate_c0], w[:, gate_c0 + 4 * HEADS:], w[:, gate_c0:gate_c0 + 4 * HEADS],
                                     jnp.zeros((D, LANE - 4 * HEADS), F32)], axis=1).astype(BF16)
            proj = _nmm(x, norm1[l], m[0], m[1], w_all, None, AB_N // 3, "ab_in")
            qkv = _ab_conv(proj, ab_conv[j])
            gates = _gates(proj, ab_a_log[j], ab_dt_bias[j])
            gates_t = gates[:, :2 * HEADS].reshape(NTOK // DP_RB, DP_NC, CHUNK, 2 * HEADS).transpose(0, 3, 1, 2)
            u, wq, ak, gl = _delta_prep(qkv, gates, gates_t)
            oa, sd = _delta_scan(CTX, u, wq, ak, gl, proj, ab_norm_a[j], None, None)
            oa, _ = _delta_scan(LAT, u, wq, ak, gl, proj, ab_norm_a[j], state_delta[:, j], oa)
            ob, sr = _ret_scan(CTX, proj, ab_ret_decay[j], ab_norm_b[j], None, None)
            ob, _ = _ret_scan(LAT, proj, ab_ret_decay[j], ab_norm_b[j], state_ret[:, j], ob)
            new_delta.append(sd)
            new_ret.append(sr)
            wo = ab_w_out[j].astype(BF16)
            x = _mmr([oa, ob], [wo[:HW], wo[HW:]], None, x, m[2], None, "ab_out")
        else:
            u3 = _nmm(x, norm1[l], m[0], m[1], hy_w_in[j].astype(BF16), hy_b_in[j], 512, "hy_in")
            x0, p = _hy_conv(u3, hy_conv_w[j], hy_conv_b[j])
            filt_w = (hy_f_w1[j], hy_f_b1[j], hy_f_freq1[j], hy_f_w2[j], hy_f_b2[j], hy_f_freq2[j], hy_f_w3[j])
            y = _hy_long_conv(p, x0, filt_w, hy_f_bias[j])
            x = _mmr([y], [hy_w_out[j].astype(BF16)], hy_b_out[j], x, m[2], None, "hy_out")
        w_gu = jnp.concatenate([ffn_w_gate[l], ffn_w_up[l]], axis=1).astype(BF16)
        gu = _nmm(x, norm2[l], m[3], m[4], w_gu, None, 512, "ffn_in")
        act = _ffn_conv(gu, ffn_conv[l], ffn_conv_b[l])
        x = _mmr([act], [ffn_w_down[l].astype(BF16)], None, x, m[5],
                 final_norm if l == DEPTH - 1 else None, "ffn_out")
    y_prompt = x[:TOK_CTX].reshape(NCTX, LCTX, D)
    y_sample = x[TOK_CTX:].reshape(NLAT, LLAT, D)
    return (y_prompt, y_sample, jnp.stack(new_delta, axis=1), jnp.stack(new_ret, axis=1))
```

```python
import functools
import math

import numpy as np
import jax
import jax.numpy as jnp
from jax import lax
from jax.experimental import pallas as pl
from jax.experimental.pallas import tpu as pltpu

F32, BF16 = jnp.float32, jnp.bfloat16
HI = lax.Precision.HIGHEST

D = 1024
NCTX, LCTX = 16, 256
NLAT, LLAT = 2, 4096
DEPTH = 4
TOK_CTX = NCTX * LCTX
TOK_LAT = NLAT * LLAT
NTOK = TOK_CTX + TOK_LAT
ROWBLK = 4096
NGROUP = NTOK // ROWBLK
GRID_W = 64
CHUNK = 64
NCH = NTOK // CHUNK
EPS = 1e-6
HEADS = 4
DK = 128
HW = HEADS * DK
LANE = 128
D_FF = 2816
HY_EMB = 33
HY_BANDS = 16
HY_FW = 64
HY_TARGET = 1e-2
HY_FAST_PCT = 0.3
HY_SLOW_PCT = 1.5
FFT_N2 = 128
VMEM_LIMIT = 52 * 1024 * 1024

CB_QA, CB_KA, CB_VA, CB_ZA, CB_QB, CB_KB, CB_VB, CB_GB, CB_GATE = 0, 4, 8, 12, 16, 20, 24, 28, 32
AB_N = 33 * LANE

CTX = dict(nseq=NCTX, L=LCTX, row0=0)
LAT = dict(nseq=NLAT, L=LLAT, row0=TOK_CTX)


def _params(*sem):
    return pltpu.CompilerParams(dimension_semantics=sem, vmem_limit_bytes=VMEM_LIMIT)


def _silu(x):
    return x * jax.nn.sigmoid(x)


def _dot(a, b, precision=None):
    return jnp.dot(a, b, preferred_element_type=F32, precision=precision)


def _dot_nt(a, b):
    return lax.dot_general(a, b, (((1,), (1,)), ((), ())), preferred_element_type=F32)


def _dot_tn(a, b):
    return lax.dot_general(a, b, (((0,), (0,)), ((), ())), preferred_element_type=F32)


def _pick_lane(x, lane):
    li = lax.broadcasted_iota(jnp.int32, x.shape, 1)
    return jnp.sum(jnp.where(li == lane, x, 0.0), axis=-1, keepdims=True)


def _mod_kernel(c_ref, w_ref, b_ref, o_ref):
    s = _silu(c_ref[...])
    o_ref[...] = _dot(s.astype(BF16), w_ref[...].astype(BF16)) + b_ref[...]


def _mod_all(cvec, mod_w, mod_b):
    tn = 1536
    return pl.pallas_call(
        _mod_kernel,
        grid=(DEPTH, 6 * D // tn),
        in_specs=[pl.BlockSpec((8, D), lambda l, j: (0, 0)),
                  pl.BlockSpec((None, D, tn), lambda l, j: (l, 0, j)),
                  pl.BlockSpec((None, 1, tn), lambda l, j: (l, 0, j))],
        out_specs=pl.BlockSpec((None, 8, tn), lambda l, j: (l, 0, j)),
        out_shape=jax.ShapeDtypeStruct((DEPTH, 8, 6 * D), F32),
        compiler_params=_params("arbitrary", "arbitrary"),
        name="mod",
    )(cvec, mod_w, mod_b.reshape(DEPTH, 1, 6 * D))


NMM_TM = 1024


def _nmm_kernel(x_ref, g_ref, sh_ref, sc_ref, w_ref, *rest, has_bias):
    if has_bias:
        b_ref, o_ref, h_ref = rest
    else:
        o_ref, h_ref = rest

    @pl.when(pl.program_id(1) == 0)
    def _():
        x = x_ref[...]
        y = x * lax.rsqrt(jnp.mean(x * x, axis=-1, keepdims=True) + EPS) * g_ref[...]
        h_ref[...] = (y * (1.0 + sc_ref[...]) + sh_ref[...]).astype(BF16)

    acc = _dot(h_ref[...], w_ref[...])
    if has_bias:
        acc = acc + b_ref[...]
    o_ref[...] = acc


def _nmm(x, gain, shift, scale, w, bias, tn, name):
    n = w.shape[1]
    grp = lambda i, j: (i * NMM_TM // ROWBLK, 0, 0)
    in_specs = [pl.BlockSpec((NMM_TM, D), lambda i, j: (i, 0)),
                pl.BlockSpec((1, D), lambda i, j: (0, 0)),
                pl.BlockSpec((None, 1, D), grp),
                pl.BlockSpec((None, 1, D), grp),
                pl.BlockSpec((D, tn), lambda i, j: (0, j))]
    args = [x, gain.reshape(1, D), shift, scale, w]
    if bias is not None:
        in_specs.append(pl.BlockSpec((1, tn), lambda i, j: (0, j)))
        args.append(bias.reshape(1, n))
    return pl.pallas_call(
        functools.partial(_nmm_kernel, has_bias=bias is not None),
        grid=(NTOK // NMM_TM, n // tn),
        in_specs=in_specs,
        out_specs=pl.BlockSpec((NMM_TM, tn), lambda i, j: (i, j)),
        out_shape=jax.ShapeDtypeStruct((NTOK, n), F32),
        scratch_shapes=[pltpu.VMEM((NMM_TM, D), BF16)],
        compiler_params=_params("arbitrary", "arbitrary"),
        name=name,
    )(*args)


MMR_TM = 512


def _mmr_kernel(*refs, n_a, has_bias, final):
    a_refs, w_refs = refs[:n_a], refs[n_a:2 * n_a]
    rest = list(refs[2 * n_a:])
    b_ref = rest.pop(0) if has_bias else None
    x_ref, gate_ref = rest.pop(0), rest.pop(0)
    fn_ref = rest.pop(0) if final else None
    o_ref = rest.pop(0)
    acc = _dot(a_refs[0][...].astype(BF16), w_refs[0][...])
    for a_ref, w_ref in zip(a_refs[1:], w_refs[1:]):
        acc = acc + _dot(a_ref[...].astype(BF16), w_ref[...])
    if has_bias:
        acc = acc + b_ref[...]
    y = x_ref[...] + gate_ref[...] * acc
    if final:
        y = y * lax.rsqrt(jnp.mean(y * y, axis=-1, keepdims=True) + EPS) * fn_ref[...]
    o_ref[...] = y


def _mmr(a_list, w_list, bias, x, gate, final_gain, name):
    grp = lambda i: (i * MMR_TM // ROWBLK, 0, 0)
    in_specs, args = [], []
    for a in a_list:
        in_specs.append(pl.BlockSpec((MMR_TM, a.shape[1]), lambda i: (i, 0)))
        args.append(a)
    for w in w_list:
        in_specs.append(pl.BlockSpec(w.shape, lambda i: (0, 0)))
        args.append(w)
    if bias is not None:
        in_specs.append(pl.BlockSpec((1, D), lambda i: (0, 0)))
        args.append(bias.reshape(1, D))
    in_specs += [pl.BlockSpec((MMR_TM, D), lambda i: (i, 0)), pl.BlockSpec((None, 1, D), grp)]
    args += [x, gate]
    if final_gain is not None:
        in_specs.append(pl.BlockSpec((1, D), lambda i: (0, 0)))
        args.append(final_gain.reshape(1, D))
    return pl.pallas_call(
        functools.partial(_mmr_kernel, n_a=len(a_list), has_bias=bias is not None,
                          final=final_gain is not None),
        grid=(NTOK // MMR_TM,),
        in_specs=in_specs,
        out_specs=pl.BlockSpec((MMR_TM, D), lambda i: (i, 0)),
        out_shape=jax.ShapeDtypeStruct((NTOK, D), F32),
        compiler_params=_params("arbitrary"),
        name=name,
    )(*args)


def _seq_shifts(x, i):
    lseq = jnp.where(i == 0, LCTX, LLAT)
    pos = lax.broadcasted_iota(jnp.int32, x.shape, 0) & (lseq - 1)
    prev = jnp.where(pos == 0, 0.0, pltpu.roll(x, 1, 0))
    nxt = jnp.where(pos == lseq - 1, 0.0, pltpu.roll(x, ROWBLK - 1, 0))
    return prev, nxt


def _conv3(x, w, i):
    prev, nxt = _seq_shifts(x, i)
    return prev * w[0:1] + x * w[1:2] + nxt * w[2:3]


def _ab_conv_kernel(x_ref, w_ref, o_ref):
    i, j = pl.program_id(0), pl.program_id(1)
    y = _silu(_conv3(x_ref[...], w_ref[...], i))
    nrm = y * lax.rsqrt(jnp.sum(y * y, axis=-1, keepdims=True) + EPS)
    nrm = nrm * jnp.where(j < CB_KA, DK ** -0.5, 1.0)
    o_ref[...] = jnp.where(j < CB_VA, nrm, y)


def _ab_conv(proj, conv_w):
    ncb = 3 * HEADS
    return pl.pallas_call(
        _ab_conv_kernel,
        grid=(NGROUP, ncb),
        in_specs=[pl.BlockSpec((ROWBLK, LANE), lambda i, j: (i, j)),
                  pl.BlockSpec((3, LANE), lambda i, j: (0, j))],
        out_specs=pl.BlockSpec((ROWBLK, LANE), lambda i, j: (i, j)),
        out_shape=jax.ShapeDtypeStruct((NTOK, ncb * LANE), F32),
        compiler_params=_params("arbitrary", "arbitrary"),
        name="ab_conv",
    )(proj, conv_w)


def _gates_kernel(x_ref, alog_ref, dtb_ref, o_ref):
    x = x_ref[...]
    lane = lax.broadcasted_iota(jnp.int32, x.shape, 1)
    row = lax.broadcasted_iota(jnp.int32, x.shape, 0) & (CHUNK - 1)
    t = x + dtb_ref[...]
    softplus = jnp.maximum(t, 0.0) + jnp.log(1.0 + jnp.exp(-jnp.abs(t)))
    g = -jnp.exp(alog_ref[...]) * softplus
    pre, suf = g, g
    s = 1
    while s < CHUNK:
        pre = pre + jnp.where(row >= s, pltpu.roll(pre, s, 0), 0.0)
        suf = suf + jnp.where(row < CHUNK - s, pltpu.roll(suf, ROWBLK - s, 0), 0.0)
        s *= 2
    gc = jnp.where(lane < HEADS, pre, suf)
    o_ref[...] = jnp.where(lane < 2 * HEADS, gc, jax.nn.sigmoid(x))


def _gates(proj, a_log, dt_bias):
    pad = lambda v: jnp.pad(v.reshape(1, 2 * HEADS), ((0, 0), (0, LANE - 2 * HEADS)))
    return pl.pallas_call(
        _gates_kernel,
        grid=(NGROUP,),
        in_specs=[pl.BlockSpec((ROWBLK, LANE), lambda i: (i, CB_GATE)),
                  pl.BlockSpec((1, LANE), lambda i: (0, 0)),
                  pl.BlockSpec((1, LANE), lambda i: (0, 0))],
        out_specs=pl.BlockSpec((ROWBLK, LANE), lambda i: (i, 0)),
        out_shape=jax.ShapeDtypeStruct((NTOK, LANE), F32),
        compiler_params=_params("arbitrary"),
        name="ab_gates",
    )(proj, pad(a_log), pad(dt_bias))


DP_RB = 512
DP_NC = DP_RB // CHUNK
TRI_BASE = 8


def _split(x):
    hi = x.astype(BF16)
    return hi, (x - hi.astype(F32)).astype(BF16)


def _dot3(a, b):
    return _dot(a[0], b[0]) + _dot(a[0], b[1]) + _dot(a[1], b[0])


def _unit_tri_inv_batch(lms):
    ii = lax.broadcasted_iota(jnp.int32, (CHUNK, CHUNK), 0)
    jj = lax.broadcasted_iota(jnp.int32, (CHUNK, CHUNK), 1)
    same = lambda b: (ii >> int(math.log2(b))) == (jj >> int(math.log2(b)))
    eye = jnp.where(ii == jj, 1.0, 0.0)
    qs = [jnp.where(same(TRI_BASE), -lm, 0.0) for lm in lms]
    ps = [eye + q for q in qs]
    qs = [_split(q) for q in qs]
    for _ in range(int(math.log2(TRI_BASE)) - 1):
        qs = [_split(_dot3(q, q)) for q in qs]
        ps = [p + _dot3(_split(p), q) for p, q in zip(ps, qs)]
    b = TRI_BASE
    while b < CHUNK:
        off = jnp.logical_and(same(2 * b), jnp.logical_not(same(b)))
        pss = [_split(p) for p in ps]
        ts = [_dot3(p, _split(jnp.where(off, lm, 0.0))) for p, lm in zip(pss, lms)]
        ps = [p - _dot3(_split(t), p2) for p, p2, t in zip(ps, pss, ts)]
        b *= 2
    return ps


def _delta_prep_kernel(q_ref, k_ref, v_ref, g_ref, gt_ref, u_ref, wq_ref, ak_ref, gl_ref):
    h = pl.program_id(1)
    ii = lax.broadcasted_iota(jnp.int32, (CHUNK, CHUNK), 0)
    jj = lax.broadcasted_iota(jnp.int32, (CHUNK, CHUNK), 1)
    probs = []
    for c in range(DP_NC):
        rows = slice(c * CHUNK, (c + 1) * CHUNK)
        q, k, v, gts = q_ref[rows, :], k_ref[rows, :], v_ref[rows, :], g_ref[rows, :]
        kbf = k.astype(BF16)
        qk = _dot_nt(q.astype(BF16), kbf)
        for d in range(2):
            incl = (ii >= jj) if d == 0 else (ii <= jj)
            strict = (ii > jj) if d == 0 else (ii < jj)
            gcol = _pick_lane(gts, d * HEADS + h)
            bcol = _pick_lane(gts, 2 * HEADS + d * HEADS + h)
            grow = gt_ref[d * HEADS + h][c:c + 1, :]
            dmask = jnp.where(incl, jnp.exp(jnp.where(incl, gcol - grow, 0.0)), 0.0)
            kb = k * bcol
            lm = jnp.where(strict, _dot_nt(kb.astype(BF16), kbf) * dmask, 0.0)
            probs.append((c, d, rows, q, k, v, kb, gcol, bcol, dmask, qk, incl, lm))
    tmats = _unit_tri_inv_batch([p[-1] for p in probs])
    for (c, d, rows, q, k, v, kb, gcol, bcol, dmask, qk, incl, _), tmat in zip(probs, tmats):
        gam = jnp.exp(gcol)
        rhs = jnp.concatenate([v * bcol, kb * gam], axis=1).astype(BF16)
        uw = _dot(tmat.astype(BF16), rhs)
        attn = jnp.where(incl, qk * dmask, 0.0)
        gtot = gcol[CHUNK - 1:CHUNK, :] if d == 0 else gcol[0:1, :]
        kd = k * jnp.exp(gtot - gcol)
        u_ref[d, rows, :] = uw[:, :DK]
        wq_ref[d, c] = jnp.concatenate([uw[:, DK:], q * gam], axis=0).astype(BF16)
        ak_ref[d, c] = jnp.concatenate([attn, kd.T], axis=0).astype(BF16)
        gl_ref[d, c] = jnp.broadcast_to(jnp.exp(gtot), (1, LANE))


def _delta_prep(qkv, gates, gates_t):
    nrb = NTOK // DP_RB
    blk = lambda cb: pl.BlockSpec((DP_RB, LANE), lambda i, h: (i, cb + h))
    return pl.pallas_call(
        _delta_prep_kernel,
        grid=(nrb, HEADS),
        in_specs=[blk(CB_QA), blk(CB_KA), blk(CB_VA),
                  pl.BlockSpec((DP_RB, LANE), lambda i, h: (i, 0)),
                  pl.BlockSpec((None, 2 * HEADS, DP_NC, CHUNK), lambda i, h: (i, 0, 0, 0))],
        out_specs=[pl.BlockSpec((2, DP_RB, LANE), lambda i, h: (0, i, h)),
                   pl.BlockSpec((2, None, DP_NC, 2 * CHUNK, DK), lambda i, h: (0, h, i, 0, 0)),
                   pl.BlockSpec((2, None, DP_NC, CHUNK + DK, CHUNK), lambda i, h: (0, h, i, 0, 0)),
                   pl.BlockSpec((2, None, DP_NC, 1, LANE), lambda i, h: (0, h, i, 0, 0))],
        out_shape=[jax.ShapeDtypeStruct((2, NTOK, HW), F32),
                   jax.ShapeDtypeStruct((2, HEADS, NCH, 2 * CHUNK, DK), BF16),
                   jax.ShapeDtypeStruct((2, HEADS, NCH, CHUNK + DK, CHUNK), BF16),
                   jax.ShapeDtypeStruct((2, HEADS, NCH, 1, LANE), F32)],
        compiler_params=_params("arbitrary", "arbitrary"),
        name="delta_prep",
    )(qkv, qkv, qkv, gates, gates_t)


def _delta_scan_kernel(*refs, L, zero_init, aliased):
    refs = list(refs)
    u_ref, wq_ref, ak_ref, gl_ref, z_ref, ng_ref = [refs.pop(0) for _ in range(6)]
    s0_ref = None if zero_init else refs.pop(0)
    if aliased:
        refs.pop(0)
    o_ref, sfin_ref, o_scr = refs
    n = L // CHUNK

    def body(c, carry):
        new = []
        for d in range(2):
            s = carry[d]
            cc = c if d == 0 else n - 1 - c
            r0 = pl.multiple_of(cc * CHUNK, CHUNK)
            r1 = _dot(wq_ref[d, cc], s.astype(BF16))
            vnew = u_ref[d, pl.ds(r0, CHUNK), :] - r1[:CHUNK]
            r2 = _dot(ak_ref[d, cc], vnew.astype(BF16))
            o_scr[d, pl.ds(r0, CHUNK), :] = r1[CHUNK:] + r2[:CHUNK]
            new.append(s * gl_ref[d, cc] + r2[CHUNK:])
        return tuple(new)

    if zero_init:
        init = (jnp.zeros((DK, DK), F32), jnp.zeros((DK, DK), F32))
    else:
        init = (s0_ref[0], s0_ref[1])
    fin = lax.fori_loop(0, n, body, init)
    sfin_ref[0] = fin[0]
    sfin_ref[1] = fin[1]
    o = o_scr[0] + o_scr[1]
    o = o * lax.rsqrt(jnp.mean(o * o, axis=-1, keepdims=True) + EPS) * ng_ref[...]
    o_ref[...] = o * _silu(z_ref[...])


def _delta_scan(stream, u, wq, ak, gl, proj, norm_gain, s0, prev_out):
    nseq, L = stream["nseq"], stream["L"]
    rb0 = stream["row0"] // L
    n = L // CHUNK
    in_specs = [pl.BlockSpec((2, L, LANE), lambda s, h: (0, rb0 + s, h)),
                pl.BlockSpec((2, None, n, 2 * CHUNK, DK), lambda s, h: (0, h, rb0 + s, 0, 0)),
                pl.BlockSpec((2, None, n, CHUNK + DK, CHUNK), lambda s, h: (0, h, rb0 + s, 0, 0)),
                pl.BlockSpec((2, None, n, 1, LANE), lambda s, h: (0, h, rb0 + s, 0, 0)),
                pl.BlockSpec((L, LANE), lambda s, h: (rb0 + s, CB_ZA + h)),
                pl.BlockSpec((1, LANE), lambda s, h: (0, 0))]
    args = [u, wq, ak, gl, proj, norm_gain.reshape(1, DK)]
    if s0 is not None:
        in_specs.append(pl.BlockSpec((None, 2, None, DK, DK), lambda s, h: (s, 0, h, 0, 0)))
        args.append(s0)
    aliases = {}
    if prev_out is not None:
        in_specs.append(pl.BlockSpec(memory_space=pl.ANY))
        aliases = {len(args): 0}
        args.append(prev_out)
    return pl.pallas_call(
        functools.partial(_delta_scan_kernel, L=L, zero_init=s0 is None, aliased=prev_out is not None),
        grid=(nseq, HEADS),
        in_specs=in_specs,
        out_specs=[pl.BlockSpec((L, LANE), lambda s, h: (rb0 + s, h)),
                   pl.BlockSpec((None, 2, None, DK, DK), lambda s, h: (s, 0, h, 0, 0))],
        out_shape=[jax.ShapeDtypeStruct((NTOK, HW), F32),
                   jax.ShapeDtypeStruct((nseq, 2, HEADS, DK, DK), F32)],
        scratch_shapes=[pltpu.VMEM((2, L, LANE), F32)],
        input_output_aliases=aliases,
        compiler_params=_params("arbitrary", "arbitrary"),
        name="delta_scan_%d" % L,
    )(*args)


def _ret_scan_kernel(*refs, L, zero_init, aliased):
    refs = list(refs)
    q_ref, k_ref, v_ref, gb_ref, dec_ref, ng_ref = [refs.pop(0) for _ in range(6)]
    r0_ref = None if zero_init else refs.pop(0)
    if aliased:
        refs.pop(0)
    o_ref, rfin_ref, o_scr = refs
    h = pl.program_id(1)
    n = L // CHUNK
    ii = lax.broadcasted_iota(jnp.int32, (CHUNK, CHUNK), 0)
    jj = lax.broadcasted_iota(jnp.int32, (CHUNK, CHUNK), 1)
    ci = lax.broadcasted_iota(jnp.int32, (CHUNK, 1), 0).astype(F32)
    consts = []
    for d in range(2):
        lg = -jnp.exp(_pick_lane(dec_ref[...], d * HEADS + h))
        dist = (ii - jj) if d == 0 else (jj - ii)
        dmat = jnp.where(dist >= 0, jnp.exp(jnp.maximum(dist, 0).astype(F32) * lg), 0.0)
        cross = jnp.exp(((ci + 1.0) if d == 0 else (CHUNK - ci)) * lg)
        sdec = jnp.exp(((CHUNK - 1.0 - ci) if d == 0 else ci) * lg)
        consts.append((dmat, cross, sdec, jnp.exp(CHUNK * lg)))

    def body(c, carry):
        new = []
        for d in range(2):
            dmat, cross, sdec, cdec = consts[d]
            r = carry[d]
            cc = c if d == 0 else n - 1 - c
            rows = pl.ds(pl.multiple_of(cc * CHUNK, CHUNK), CHUNK)
            q = q_ref[rows, :] * DK ** -0.5
            k, v = k_ref[rows, :], v_ref[rows, :]
            vb = v.astype(BF16)
            inner = _dot_nt(q.astype(BF16), k.astype(BF16)) * dmat
            o_scr[d, rows, :] = _dot(inner.astype(BF16), vb) + _dot((q * cross).astype(BF16), r.astype(BF16))
            new.append(r * cdec + _dot_tn((k * sdec).astype(BF16), vb))
        return tuple(new)

    if zero_init:
        init = (jnp.zeros((DK, DK), F32), jnp.zeros((DK, DK), F32))
    else:
        init = (r0_ref[0], r0_ref[1])
    fin = lax.fori_loop(0, n, body, init)
    rfin_ref[0] = fin[0]
    rfin_ref[1] = fin[1]
    o = o_scr[0] + o_scr[1]
    mu = jnp.mean(o, axis=-1, keepdims=True)
    var = jnp.mean(jnp.square(o - mu), axis=-1, keepdims=True)
    o = (o - mu) * lax.rsqrt(var + EPS) * ng_ref[...]
    o_ref[...] = o * _silu(gb_ref[...])


def _ret_scan(stream, proj, ret_decay, norm_gain, r0, prev_out):
    nseq, L = stream["nseq"], stream["L"]
    rb0 = stream["row0"] // L
    blk = lambda cb: pl.BlockSpec((L, LANE), lambda s, h: (rb0 + s, cb + h))
    in_specs = [blk(CB_QB), blk(CB_KB), blk(CB_VB), blk(CB_GB),
                pl.BlockSpec((1, LANE), lambda s, h: (0, 0)),
                pl.BlockSpec((1, LANE), lambda s, h: (0, 0))]
    dec = jnp.pad(ret_decay.reshape(1, 2 * HEADS), ((0, 0), (0, LANE - 2 * HEADS)))
    args = [proj, proj, proj, proj, dec, norm_gain.reshape(1, DK)]
    if r0 is not None:
        in_specs.append(pl.BlockSpec((None, 2, None, DK, DK), lambda s, h: (s, 0, h, 0, 0)))
        args.append(r0)
    aliases = {}
    if prev_out is not None:
        in_specs.append(pl.BlockSpec(memory_space=pl.ANY))
        aliases = {len(args): 0}
        args.append(prev_out)
    return pl.pallas_call(
        functools.partial(_ret_scan_kernel, L=L, zero_init=r0 is None, aliased=prev_out is not None),
        grid=(nseq, HEADS),
        in_specs=in_specs,
        out_specs=[pl.BlockSpec((L, LANE), lambda s, h: (rb0 + s, h)),
                   pl.BlockSpec((None, 2, None, DK, DK), lambda s, h: (s, 0, h, 0, 0))],
        out_shape=[jax.ShapeDtypeStruct((NTOK, HW), F32),
                   jax.ShapeDtypeStruct((nseq, 2, HEADS, DK, DK), F32)],
        scratch_shapes=[pltpu.VMEM((2, L, LANE), F32)],
        input_output_aliases=aliases,
        compiler_params=_params("arbitrary", "arbitrary"),
        name="ret_scan_%d" % L,
    )(*args)


def _hy_conv_kernel(x0_ref, x1_ref, v_ref, w0_ref, w1_ref, wv_ref, b0_ref, b1_ref, bv_ref, x0o_ref, p_ref):
    i = pl.program_id(0)
    x0o_ref[...] = _conv3(x0_ref[...], w0_ref[...], i) + b0_ref[...]
    x1 = _conv3(x1_ref[...], w1_ref[...], i) + b1_ref[...]
    v = _conv3(v_ref[...], wv_ref[...], i) + bv_ref[...]
    p_ref[...] = v * x1


def _hy_conv(u3, conv_w, conv_b):
    ncb = D // LANE
    xb = lambda k: pl.BlockSpec((ROWBLK, LANE), lambda i, j: (i, k * ncb + j))
    wb = lambda k: pl.BlockSpec((3, LANE), lambda i, j: (0, k * ncb + j))
    bb = lambda k: pl.BlockSpec((1, LANE), lambda i, j: (0, k * ncb + j))
    cb = conv_b.reshape(1, 3 * D)
    return pl.pallas_call(
        _hy_conv_kernel,
        grid=(NGROUP, ncb),
        in_specs=[xb(0), xb(1), xb(2), wb(0), wb(1), wb(2), bb(0), bb(1), bb(2)],
        out_specs=[pl.BlockSpec((ROWBLK, LANE), lambda i, j: (i, j))] * 2,
        out_shape=[jax.ShapeDtypeStruct((NTOK, D), F32)] * 2,
        compiler_params=_params("arbitrary", "arbitrary"),
        name="hy_conv",
    )(u3, u3, u3, conv_w, conv_w, conv_w, cb, cb, cb)


def _filter_features(L):
    r = np.arange(2 * L)
    pos = np.where(r < L, r, 2 * L - r) % L
    t = pos / (L - 1.0)
    bands = np.linspace(1e-4, HY_BANDS - 1, HY_BANDS)
    ang = 2.0 * np.pi * np.outer(pos, bands) / L
    z = np.zeros((2 * L, HY_FW), np.float64)
    z[:, 0] = t
    z[:, 1:1 + HY_BANDS] = np.cos(ang)
    z[:, 1 + HY_BANDS:HY_EMB] = -np.sin(ang)
    z[:, HY_EMB] = (r != L)
    return z.astype(np.float32)


def _filter_kernel(z_ref, w1_ref, b1_ref, f1_ref, w2_ref, b2_ref, f2_ref, w3_ref, dl_ref, o_ref):
    z = z_ref[...]
    hid = jnp.sin(f1_ref[...] * (_dot(z, w1_ref[...], HI) + b1_ref[...]))
    hid = jnp.sin(f2_ref[...] * (_dot(hid, w2_ref[...], HI) + b2_ref[...]))
    filt = _dot(hid, w3_ref[...], HI)
    window = jnp.exp(-z[:, 0:1] * dl_ref[...]) * z[:, HY_EMB:HY_EMB + 1]
    o_ref[...] = filt * window


def _hy_filter(L, w1, b1, f1, w2, b2, f2, w3):
    rb = min(512, L)
    nblk = 2 * L // rb
    z = jnp.asarray(_filter_features(L))
    w1p = jnp.pad(w1, ((0, HY_FW - HY_EMB), (0, 0)))
    min_decay = math.log(HY_TARGET) / HY_SLOW_PCT
    max_decay = math.log(HY_TARGET) / HY_FAST_PCT
    deltas = jnp.asarray(np.abs(np.linspace(min_decay, max_decay, D)).astype(np.float32).reshape(1, D))
    vec = lambda v: v.reshape(1, HY_FW)
    full = lambda shp: pl.BlockSpec(shp, lambda i: (0, 0))
    return pl.pallas_call(
        _filter_kernel,
        grid=(nblk,),
        in_specs=[pl.BlockSpec((rb, HY_FW), lambda i: (i, 0)),
                  full((HY_FW, HY_FW)), full((1, HY_FW)), full((1, HY_FW)),
                  full((HY_FW, HY_FW)), full((1, HY_FW)), full((1, HY_FW)),
                  pl.BlockSpec((HY_FW, D), lambda i: (0, i // (nblk // 2))),
                  full((1, D))],
        out_specs=pl.BlockSpec((rb, D), lambda i: (i, 0)),
        out_shape=jax.ShapeDtypeStruct((2 * L, D), F32),
        compiler_params=_params("arbitrary"),
        name="hy_filter_%d" % L,
    )(z, w1p, vec(b1), vec(f1), w2, vec(b2), vec(f2), w3, deltas)


FFT_TC = 8192


def _cis(num, den):
    ang = -2.0 * np.pi * (num % den) / den
    return np.cos(ang), np.sin(ang)


def _fft_consts(L):
    n = 2 * L
    n1 = n // FFT_N2
    k1 = np.arange(n1)
    fr, fi = _cis(np.outer(k1, k1), n1)
    half = n1 // 2
    sig = np.block([[fr[:, :half], -fi[:, :half]], [fi[:, :half], fr[:, :half]]])
    ker = np.concatenate([fr, fi], axis=0)
    cr, ci = fr[:, :half].T, -fi[:, :half].T
    inv = np.stack([np.concatenate([cr, -ci], axis=1), np.concatenate([ci, cr], axis=1)]) / n
    k2 = np.arange(FFT_N2)
    gr, gi = _cis(np.outer(k2, k2), FFT_N2)
    f2 = np.block([[gr, -gi], [gi, gr]])
    f2inv = np.block([[gr, gi], [-gi, gr]])
    tr, ti = _cis(np.outer(k1, k2), n)
    f = lambda a: np.asarray(a, np.float32)
    return dict(n1=n1, sig=f(sig), ker=f(ker), inv=f(inv), f2=f(f2), f2inv=f(f2inv),
                twr=f(tr).reshape(n1, FFT_N2, 1), twi=f(ti).reshape(n1, FFT_N2, 1))


def _lin_rows(mat, rows):
    out = []
    for m in range(mat.shape[0]):
        acc = None
        for k in range(mat.shape[1]):
            cf = float(mat[m, k])
            if abs(cf) < 1e-9:
                continue
            term = rows[k] if abs(cf - 1.0) < 1e-9 else (-rows[k] if abs(cf + 1.0) < 1e-9 else cf * rows[k])
            acc = term if acc is None else acc + term
        out.append(jnp.zeros_like(rows[0]) if acc is None else acc)
    return out


def _fft1_mxu_kernel(m_ref, *refs):
    o_ref = refs[-1]
    x = jnp.concatenate([r[...] for r in refs[:-1]], axis=0) if len(refs) > 2 else refs[0][...]
    y = _dot(m_ref[...], x, HI)
    n1 = y.shape[0] // 2
    o_ref[0] = y[:n1]
    o_ref[1] = y[n1:]


def _fft1_vpu_kernel(*refs, mat, split):
    o_ref = refs[-1]
    if split:
        x = refs[0]
        rows = [x[s, r:r + 1, :] for s in range(2) for r in range(x.shape[1])]
    else:
        x = refs[0]
        rows = [x[r:r + 1, :] for r in range(x.shape[0])]
    out = _lin_rows(mat, rows)
    n1 = len(out) // 2
    for m, row in enumerate(out):
        o_ref[m // n1, m % n1:m % n1 + 1, :] = row


def _fft_stage1(stream, consts, x, is_kernel):
    L, n1 = stream["L"], consts["n1"]
    cols = FFT_N2 * D
    npairs = 1 if is_kernel else stream["nseq"] // 2
    out_shape = jax.ShapeDtypeStruct((npairs, 2, n1, cols), F32)
    out_spec = pl.BlockSpec((None, 2, n1, FFT_TC), lambda p, j: (p, 0, 0, j))
    grid = (npairs, cols // FFT_TC)
    name = "fft1_%s_%d" % ("k" if is_kernel else "x", L)
    if n1 >= 8:
        mat = jnp.asarray(consts["ker"] if is_kernel else consts["sig"])
        mspec = pl.BlockSpec(mat.shape, lambda p, j: (0, 0))
        if is_kernel:
            xs = [x.reshape(n1, cols)]
            specs = [pl.BlockSpec((n1, FFT_TC), lambda p, j: (0, j))]
        else:
            assert npairs == 1
            xv = x.reshape(NTOK // (n1 // 2 * FFT_N2), n1 // 2, cols)
            b0 = stream["row0"] // L
            xs = [xv, xv]
            specs = [pl.BlockSpec((None, n1 // 2, FFT_TC), lambda p, j: (b0, 0, j)),
                     pl.BlockSpec((None, n1 // 2, FFT_TC), lambda p, j: (b0 + 1, 0, j))]
        return pl.pallas_call(_fft1_mxu_kernel, grid=grid, in_specs=[mspec] + specs, out_specs=out_spec,
                              out_shape=out_shape, compiler_params=_params("arbitrary", "arbitrary"),
                              name=name)(mat, *xs)
    if is_kernel:
        xs = x.reshape(n1, cols)
        spec = pl.BlockSpec((n1, FFT_TC), lambda p, j: (0, j))
        mat = consts["ker"]
    else:
        assert stream["row0"] == 0
        xs = x.reshape(NTOK // (n1 // 2 * FFT_N2), n1 // 2, cols)
        spec = pl.BlockSpec((2, n1 // 2, FFT_TC), lambda p, j: (p, 0, j))
        mat = consts["sig"]
    return pl.pallas_call(functools.partial(_fft1_vpu_kernel, mat=mat, split=not is_kernel),
                          grid=grid, in_specs=[spec], out_specs=out_spec, out_shape=out_shape,
                          compiler_params=_params("arbitrary", "arbitrary"), name=name)(xs)


def _fft2_kernel(a_ref, twr_ref, twi_ref, f2_ref, *rest, conv):
    ar, ai = a_ref[0], a_ref[1]
    twr, twi = twr_ref[...], twi_ref[...]
    x = jnp.concatenate([ar * twr - ai * twi, ar * twi + ai * twr], axis=0)
    y = _dot(f2_ref[...], x, HI)
    yr, yi = y[:FFT_N2], y[FFT_N2:]
    if not conv:
        o_ref, = rest
        o_ref[0] = yr
        o_ref[1] = yi
        return
    kf_ref, f2inv_ref, o_ref = rest
    kr, ki = kf_ref[0], kf_ref[1]
    z = jnp.concatenate([yr * kr - yi * ki, yr * ki + yi * kr], axis=0)
    w = _dot(f2inv_ref[...], z, HI)
    wr, wi = w[:FFT_N2], w[FFT_N2:]
    o_ref[0] = wr * twr + wi * twi
    o_ref[1] = wi * twr - wr * twi


def _fft_stage2(stream, consts, a, kf):
    n1 = consts["n1"]
    npairs = a.shape[0]
    a5 = a.reshape(npairs, 2, n1, FFT_N2, D)
    slab = pl.BlockSpec((None, 2, None, FFT_N2, D), lambda p, k: (p, 0, k, 0, 0))
    tw = pl.BlockSpec((None, FFT_N2, 1), lambda p, k: (k, 0, 0))
    mat = pl.BlockSpec((2 * FFT_N2, 2 * FFT_N2), lambda p, k: (0, 0))
    in_specs = [slab, tw, tw, mat]
    args = [a5, jnp.asarray(consts["twr"]), jnp.asarray(consts["twi"]), jnp.asarray(consts["f2"])]
    if kf is not None:
        in_specs += [pl.BlockSpec((None, 2, None, FFT_N2, D), lambda p, k: (0, 0, k, 0, 0)), mat]
        args += [kf.reshape(1, 2, n1, FFT_N2, D), jnp.asarray(consts["f2inv"])]
    out = pl.pallas_call(
        functools.partial(_fft2_kernel, conv=kf is not None),
        grid=(npairs, n1),
        in_specs=in_specs,
        out_specs=slab,
        out_shape=jax.ShapeDtypeStruct(a5.shape, F32),
        compiler_params=_params("arbitrary", "arbitrary"),
        name="fft2_%s_%d" % ("conv" if kf is not None else "spec", stream["L"]),
    )(*args)
    return out.reshape(a.shape)


def _fft3_mxu_kernel(m_ref, b_ref, p_ref, x0_ref, bias_ref, *rest):
    o_ref = rest[-1]
    n1 = b_ref.shape[1]
    y = _dot(m_ref[...], b_ref[...].reshape(2 * n1, b_ref.shape[2]), HI)
    o_ref[...] = (y + p_ref[...] * bias_ref[...]) * x0_ref[...]


def _fft3_vpu_kernel(b_ref, p_ref, x0_ref, bias_ref, o_ref, *, mat):
    n1 = b_ref.shape[1]
    rows = [b_ref[c, r:r + 1, :] for c in range(2) for r in range(n1)]
    out = _lin_rows(mat, rows)
    half = n1 // 2
    bias = bias_ref[...]
    for m, row in enumerate(out):
        s, r = m // half, m % half
        o_ref[s, r:r + 1, :] = (row + p_ref[s, r:r + 1, :] * bias) * x0_ref[s, r:r + 1, :]


def _fft_stage3(stream, consts, b, p, x0, bias, prev_out):
    L, n1 = stream["L"], consts["n1"]
    half = n1 // 2
    cols = FFT_N2 * D
    npairs = b.shape[0]
    bias_t = jnp.tile(bias.reshape(1, D), (1, FFT_TC // D))
    view = (NTOK // (half * FFT_N2), half, cols)
    pv, xv = p.reshape(view), x0.reshape(view)
    out_shape = jax.ShapeDtypeStruct(view, F32)
    name = "fft3_%d" % L
    if n1 >= 8:
        assert npairs == 1
        b0 = stream["row0"] // L
        seq = pl.BlockSpec((None, half, FFT_TC), lambda j, s: (b0 + s, 0, j))
        in_specs = [pl.BlockSpec((None, half, 2 * n1), lambda j, s: (s, 0, 0)),
                    pl.BlockSpec((None, 2, n1, FFT_TC), lambda j, s: (0, 0, 0, j)),
                    seq, seq, pl.BlockSpec((1, FFT_TC), lambda j, s: (0, 0))]
        args = [jnp.asarray(consts["inv"]), b, pv, xv, bias_t]
        aliases = {}
        if prev_out is not None:
            in_specs.append(pl.BlockSpec(memory_space=pl.ANY))
            aliases = {len(args): 0}
            args.append(prev_out.reshape(view))
        out = pl.pallas_call(_fft3_mxu_kernel, grid=(cols // FFT_TC, 2), in_specs=in_specs, out_specs=seq,
                             out_shape=out_shape, input_output_aliases=aliases,
                             compiler_params=_params("arbitrary", "arbitrary"), name=name)(*args)
        return out.reshape(NTOK, D)
    assert stream["row0"] == 0 and prev_out is None
    mat = np.concatenate([consts["inv"][0], consts["inv"][1]], axis=0)
    seq = pl.BlockSpec((2, half, FFT_TC), lambda pr, j: (pr, 0, j))
    out = pl.pallas_call(
        functools.partial(_fft3_vpu_kernel, mat=mat),
        grid=(npairs, cols // FFT_TC),
        in_specs=[pl.BlockSpec((None, 2, n1, FFT_TC), lambda pr, j: (pr, 0, 0, j)), seq, seq,
                  pl.BlockSpec((1, FFT_TC), lambda pr, j: (0, 0))],
        out_specs=seq, out_shape=out_shape,
        compiler_params=_params("arbitrary", "arbitrary"), name=name)(b, pv, xv, bias_t)
    return out.reshape(NTOK, D)


def _hy_long_conv(p, x0, filt_w, bias):
    out = None
    for stream in (CTX, LAT):
        consts = _fft_consts(stream["L"])
        kern = _hy_filter(stream["L"], *filt_w)
        kf = _fft_stage2(stream, consts, _fft_stage1(stream, consts, kern, True), None)
        a = _fft_stage1(stream, consts, p, False)
        b = _fft_stage2(stream, consts, a, kf)
        out = _fft_stage3(stream, consts, b, p, x0, bias, out)
    return out


FFN_TM = 1024
FFN_TF = 256
FFN_HALO = 128


def _ffn_kernel(*refs, final):
    refs = list(refs)
    (xc_ref, xp_ref, xn_ref, g_ref, sh_ref, sc_ref, wg_ref, wu_ref, wd_ref, cw_ref, cb_ref,
     gate_ref) = [refs.pop(0) for _ in range(12)]
    fn_ref = refs.pop(0) if final else None
    o_ref, h_ref, acc_ref = refs
    i, f = pl.program_id(0), pl.program_id(1)
    ext = FFN_TM + 2 * FFN_HALO
    is_lat = i >= TOK_CTX // FFN_TM

    @pl.when(f == 0)
    def _():
        def norm(x):
            y = x * lax.rsqrt(jnp.mean(x * x, axis=-1, keepdims=True) + EPS) * g_ref[...]
            return (y * (1.0 + sc_ref[...]) + sh_ref[...]).astype(BF16)
        seq_pos = (i * FFN_TM - TOK_CTX) & (LLAT - 1)
        keep_prev = jnp.logical_and(is_lat, seq_pos != 0)
        keep_next = jnp.logical_and(is_lat, seq_pos != LLAT - FFN_TM)
        h_ref[0:FFN_HALO, :] = jnp.where(keep_prev, norm(xp_ref[...]), 0.0).astype(BF16)
        h_ref[FFN_HALO:FFN_HALO + FFN_TM, :] = norm(xc_ref[...])
        h_ref[FFN_HALO + FFN_TM:ext, :] = jnp.where(keep_next, norm(xn_ref[...]), 0.0).astype(BF16)
        acc_ref[...] = jnp.zeros_like(acc_ref)

    gate = _dot(h_ref[...], wg_ref[...])
    up = _dot(h_ref[FFN_HALO:FFN_HALO + FFN_TM, :], wu_ref[...])
    period = jnp.where(is_lat, GRID_W, LCTX)
    r = lax.broadcasted_iota(jnp.int32, (ext, 1), 0)
    col = (r + FFN_HALO) & (period - 1)
    left = jnp.where(col == 0, 0.0, pltpu.roll(gate, 1, 0))
    right = jnp.where(col == period - 1, 0.0, pltpu.roll(gate, ext - 1, 0))
    kidx = lax.broadcasted_iota(jnp.int32, (9, 1), 0)
    mid_row = jnp.logical_and(kidx >= 3, kidx < 6)
    w = jnp.where(jnp.logical_or(is_lat, mid_row), cw_ref[...], 0.0)
    tap = lambda k, lo: (left[lo:lo + FFN_TM] * w[3 * k:3 * k + 1] + gate[lo:lo + FFN_TM] * w[3 * k + 1:3 * k + 2]
                         + right[lo:lo + FFN_TM] * w[3 * k + 2:3 * k + 3])
    z = tap(0, FFN_HALO - GRID_W) + tap(1, FFN_HALO) + tap(2, FFN_HALO + GRID_W) + cb_ref[...]
    act = (0.5 * z) * (1.0 + jnp.tanh(0.5 * z)) * up
    acc_ref[...] += _dot(act.astype(BF16), wd_ref[...])

    @pl.when(f == pl.num_programs(1) - 1)
    def _():
        y = xc_ref[...] + gate_ref[...] * acc_ref[...]
        if final:
            y = y * lax.rsqrt(jnp.mean(y * y, axis=-1, keepdims=True) + EPS) * fn_ref[...]
        o_ref[...] = y


def _ffn(x, gain, shift, scale, w_gu, conv_w, conv_b, w_down, gate, final_gain):
    nf = D_FF // FFN_TF
    hb = FFN_TM // FFN_HALO
    nhb = NTOK // FFN_HALO
    grp = lambda i, f: (i * FFN_TM // ROWBLK, 0, 0)
    vec = pl.BlockSpec((None, 1, D), grp)
    in_specs = [pl.BlockSpec((FFN_TM, D), lambda i, f: (i, 0)),
                pl.BlockSpec((FFN_HALO, D), lambda i, f: (jnp.maximum(i * hb - 1, 0), 0)),
                pl.BlockSpec((FFN_HALO, D), lambda i, f: (jnp.minimum((i + 1) * hb, nhb - 1), 0)),
                pl.BlockSpec((1, D), lambda i, f: (0, 0)), vec, vec,
                pl.BlockSpec((D, FFN_TF), lambda i, f: (0, f)),
                pl.BlockSpec((D, FFN_TF), lambda i, f: (0, nf + f)),
                pl.BlockSpec((FFN_TF, D), lambda i, f: (f, 0)),
                pl.BlockSpec((9, FFN_TF), lambda i, f: (0, f)),
                pl.BlockSpec((1, FFN_TF), lambda i, f: (0, f)),
                vec]
    args = [x, x, x, gain.reshape(1, D), shift, scale, w_gu, w_gu, w_down, conv_w.reshape(9, D_FF),
            conv_b.reshape(1, D_FF), gate]
    if final_gain is not None:
        in_specs.append(pl.BlockSpec((1, D), lambda i, f: (0, 0)))
        args.append(final_gain.reshape(1, D))
    return pl.pallas_call(
        functools.partial(_ffn_kernel, final=final_gain is not None),
        grid=(NTOK // FFN_TM, nf),
        in_specs=in_specs,
        out_specs=pl.BlockSpec((FFN_TM, D), lambda i, f: (i, 0)),
        out_shape=jax.ShapeDtypeStruct((NTOK, D), F32),
        scratch_shapes=[pltpu.VMEM((FFN_TM + 2 * FFN_HALO, D), BF16), pltpu.VMEM((FFN_TM, D), F32)],
        compiler_params=_params("arbitrary", "arbitrary"),
        name="ffn",
    )(*args)


def _ab_w_in_cols(w):
    gate_c0 = 3 * HW + HW
    return jnp.concatenate([w[:, :gate_c0], w[:, gate_c0 + 4 * HEADS:], w[:, gate_c0:gate_c0 + 4 * HEADS],
                            jnp.zeros((D, LANE - 4 * HEADS), F32)], axis=1).astype(BF16)


def _ab_mixer(proj, conv_w, a_log, dt_bias, norm_a, norm_b, ret_decay, s_delta0, s_ret0):
    qkv = _ab_conv(proj, conv_w)
    gates = _gates(proj, a_log, dt_bias)
    gates_t = gates[:, :2 * HEADS].reshape(NTOK // DP_RB, DP_NC, CHUNK, 2 * HEADS).transpose(0, 3, 1, 2)
    u, wq, ak, gl = _delta_prep(qkv, gates, gates_t)
    oa, sd = _delta_scan(CTX, u, wq, ak, gl, proj, norm_a, None, None)
    oa, _ = _delta_scan(LAT, u, wq, ak, gl, proj, norm_a, s_delta0, oa)
    ob, sr = _ret_scan(CTX, proj, ret_decay, norm_b, None, None)
    ob, _ = _ret_scan(LAT, proj, ret_decay, norm_b, s_ret0, ob)
    return oa, ob, sd, sr


def kernel(x_prompt, x_sample, state_delta, state_ret, c, c_ctx, mod_w, mod_b, norm1, norm2, ab_w_in, ab_conv, ab_a_log, ab_dt_bias, ab_norm_a, ab_norm_b, ab_ret_decay, ab_w_out, hy_w_in, hy_b_in, hy_conv_w, hy_conv_b, hy_f_w1, hy_f_b1, hy_f_freq1, hy_f_w2, hy_f_b2, hy_f_freq2, hy_f_w3, hy_f_bias, hy_w_out, hy_b_out, ffn_w_gate, ffn_w_up, ffn_conv, ffn_conv_b, ffn_w_down, final_norm):
    x = jnp.concatenate([x_prompt.reshape(TOK_CTX, D), x_sample.reshape(TOK_LAT, D)], axis=0)
    cvec = jnp.concatenate([c_ctx[None], c, jnp.zeros((8 - 1 - NLAT, D), F32)], axis=0)
    mod = _mod_all(cvec, mod_w, mod_b)
    new_delta, new_ret = [], []
    for l in range(DEPTH):
        m = [mod[l, :NGROUP, k * D:(k + 1) * D].reshape(NGROUP, 1, D) for k in range(6)]
        j = l // 2
        if l % 2 == 0:
            proj = _nmm(x, norm1[l], m[0], m[1], _ab_w_in_cols(ab_w_in[j]), None, AB_N // 3, "ab_in")
            oa, ob, sd, sr = _ab_mixer(proj, ab_conv[j], ab_a_log[j], ab_dt_bias[j], ab_norm_a[j], ab_norm_b[j],
                                       ab_ret_decay[j], state_delta[:, j], state_ret[:, j])
            new_delta.append(sd)
            new_ret.append(sr)
            wo = ab_w_out[j].astype(BF16)
            x = _mmr([oa, ob], [wo[:HW], wo[HW:]], None, x, m[2], None, "ab_out")
        else:
            u3 = _nmm(x, norm1[l], m[0], m[1], hy_w_in[j].astype(BF16), hy_b_in[j], 512, "hy_in")
            x0, p = _hy_conv(u3, hy_conv_w[j], hy_conv_b[j])
            filt_w = (hy_f_w1[j], hy_f_b1[j], hy_f_freq1[j], hy_f_w2[j], hy_f_b2[j], hy_f_freq2[j], hy_f_w3[j])
            y = _hy_long_conv(p, x0, filt_w, hy_f_bias[j])
            x = _mmr([y], [hy_w_out[j].astype(BF16)], hy_b_out[j], x, m[2], None, "hy_out")
        w_gu = jnp.concatenate([ffn_w_gate[l], ffn_w_up[l]], axis=1).astype(BF16)
        x = _ffn(x, norm2[l], m[3], m[4], w_gu, ffn_conv[l], ffn_conv_b[l], ffn_w_down[l].astype(BF16), m[5],
                 final_norm if l == DEPTH - 1 else None)
    y_prompt = x[:TOK_CTX].reshape(NCTX, LCTX, D)
    y_sample = x[TOK_CTX:].reshape(NLAT, LLAT, D)
    return (y_prompt, y_sample, jnp.stack(new_delta, axis=1), jnp.stack(new_ret, axis=1))
```

```python
import functools
import math

import numpy as np
import jax
import jax.numpy as jnp
from jax import lax
from jax.experimental import pallas as pl
from jax.experimental.pallas import tpu as pltpu

F32, BF16 = jnp.float32, jnp.bfloat16
HI = lax.Precision.HIGHEST

D = 1024
NCTX, LCTX = 16, 256
NLAT, LLAT = 2, 4096
DEPTH = 4
TOK_CTX = NCTX * LCTX
TOK_LAT = NLAT * LLAT
NTOK = TOK_CTX + TOK_LAT
ROWBLK = 4096
NGROUP = NTOK // ROWBLK
GRID_W = 64
CHUNK = 64
NCH = NTOK // CHUNK
EPS = 1e-6
HEADS = 4
DK = 128
HW = HEADS * DK
LANE = 128
D_FF = 2816
HY_EMB = 33
HY_BANDS = 16
HY_FW = 64
HY_TARGET = 1e-2
HY_FAST_PCT = 0.3
HY_SLOW_PCT = 1.5
FFT_N2 = 128
VMEM_LIMIT = 52 * 1024 * 1024

CB_QA, CB_KA, CB_VA, CB_ZA, CB_QB, CB_KB, CB_VB, CB_GB, CB_GATE = 0, 4, 8, 12, 16, 20, 24, 28, 32
AB_N = 33 * LANE

CTX = dict(nseq=NCTX, L=LCTX, row0=0)
LAT = dict(nseq=NLAT, L=LLAT, row0=TOK_CTX)


def _params(*sem):
    return pltpu.CompilerParams(dimension_semantics=sem, vmem_limit_bytes=VMEM_LIMIT)


def _silu(x):
    return x * jax.nn.sigmoid(x)


def _dot(a, b, precision=None):
    return jnp.dot(a, b, preferred_element_type=F32, precision=precision)


def _dot_nt(a, b):
    return lax.dot_general(a, b, (((1,), (1,)), ((), ())), preferred_element_type=F32)


def _dot_tn(a, b):
    return lax.dot_general(a, b, (((0,), (0,)), ((), ())), preferred_element_type=F32)


def _pick_lane(x, lane):
    li = lax.broadcasted_iota(jnp.int32, x.shape, 1)
    return jnp.sum(jnp.where(li == lane, x, 0.0), axis=-1, keepdims=True)


def _mod_kernel(c_ref, w_ref, b_ref, o_ref):
    s = _silu(c_ref[...])
    o_ref[...] = _dot(s.astype(BF16), w_ref[...].astype(BF16)) + b_ref[...]


def _mod_all(cvec, mod_w, mod_b):
    tn = 1536
    return pl.pallas_call(
        _mod_kernel,
        grid=(DEPTH, 6 * D // tn),
        in_specs=[pl.BlockSpec((8, D), lambda l, j: (0, 0)),
                  pl.BlockSpec((None, D, tn), lambda l, j: (l, 0, j)),
                  pl.BlockSpec((None, 1, tn), lambda l, j: (l, 0, j))],
        out_specs=pl.BlockSpec((None, 8, tn), lambda l, j: (l, 0, j)),
        out_shape=jax.ShapeDtypeStruct((DEPTH, 8, 6 * D), F32),
        compiler_params=_params("arbitrary", "arbitrary"),
        name="mod",
    )(cvec, mod_w, mod_b.reshape(DEPTH, 1, 6 * D))


NMM_TM = 1024


def _nmm_kernel(x_ref, g_ref, sh_ref, sc_ref, w_ref, *rest, has_bias):
    if has_bias:
        b_ref, o_ref, h_ref = rest
    else:
        o_ref, h_ref = rest

    @pl.when(pl.program_id(1) == 0)
    def _():
        x = x_ref[...]
        y = x * lax.rsqrt(jnp.mean(x * x, axis=-1, keepdims=True) + EPS) * g_ref[...]
        h_ref[...] = (y * (1.0 + sc_ref[...]) + sh_ref[...]).astype(BF16)

    acc = _dot(h_ref[...], w_ref[...])
    if has_bias:
        acc = acc + b_ref[...]
    o_ref[...] = acc


def _nmm(x, gain, shift, scale, w, bias, tn, name):
    n = w.shape[1]
    grp = lambda i, j: (i * NMM_TM // ROWBLK, 0, 0)
    in_specs = [pl.BlockSpec((NMM_TM, D), lambda i, j: (i, 0)),
                pl.BlockSpec((1, D), lambda i, j: (0, 0)),
                pl.BlockSpec((None, 1, D), grp),
                pl.BlockSpec((None, 1, D), grp),
                pl.BlockSpec((D, tn), lambda i, j: (0, j))]
    args = [x, gain.reshape(1, D), shift, scale, w]
    if bias is not None:
        in_specs.append(pl.BlockSpec((1, tn), lambda i, j: (0, j)))
        args.append(bias.reshape(1, n))
    return pl.pallas_call(
        functools.partial(_nmm_kernel, has_bias=bias is not None),
        grid=(NTOK // NMM_TM, n // tn),
        in_specs=in_specs,
        out_specs=pl.BlockSpec((NMM_TM, tn), lambda i, j: (i, j)),
        out_shape=jax.ShapeDtypeStruct((NTOK, n), F32),
        scratch_shapes=[pltpu.VMEM((NMM_TM, D), BF16)],
        compiler_params=_params("arbitrary", "arbitrary"),
        name=name,
    )(*args)


MMR_TM = 512


def _mmr_kernel(*refs, n_a, has_bias, final):
    a_refs, w_refs = refs[:n_a], refs[n_a:2 * n_a]
    rest = list(refs[2 * n_a:])
    b_ref = rest.pop(0) if has_bias else None
    x_ref, gate_ref = rest.pop(0), rest.pop(0)
    fn_ref = rest.pop(0) if final else None
    o_ref = rest.pop(0)
    acc = _dot(a_refs[0][...].astype(BF16), w_refs[0][...])
    for a_ref, w_ref in zip(a_refs[1:], w_refs[1:]):
        acc = acc + _dot(a_ref[...].astype(BF16), w_ref[...])
    if has_bias:
        acc = acc + b_ref[...]
    y = x_ref[...] + gate_ref[...] * acc
    if final:
        y = y * lax.rsqrt(jnp.mean(y * y, axis=-1, keepdims=True) + EPS) * fn_ref[...]
    o_ref[...] = y


def _mmr(a_list, w_list, bias, x, gate, final_gain, name):
    grp = lambda i: (i * MMR_TM // ROWBLK, 0, 0)
    in_specs, args = [], []
    for a in a_list:
        in_specs.append(pl.BlockSpec((MMR_TM, a.shape[1]), lambda i: (i, 0)))
        args.append(a)
    for w in w_list:
        in_specs.append(pl.BlockSpec(w.shape, lambda i: (0, 0)))
        args.append(w)
    if bias is not None:
        in_specs.append(pl.BlockSpec((1, D), lambda i: (0, 0)))
        args.append(bias.reshape(1, D))
    in_specs += [pl.BlockSpec((MMR_TM, D), lambda i: (i, 0)), pl.BlockSpec((None, 1, D), grp)]
    args += [x, gate]
    if final_gain is not None:
        in_specs.append(pl.BlockSpec((1, D), lambda i: (0, 0)))
        args.append(final_gain.reshape(1, D))
    return pl.pallas_call(
        functools.partial(_mmr_kernel, n_a=len(a_list), has_bias=bias is not None,
                          final=final_gain is not None),
        grid=(NTOK // MMR_TM,),
        in_specs=in_specs,
        out_specs=pl.BlockSpec((MMR_TM, D), lambda i: (i, 0)),
        out_shape=jax.ShapeDtypeStruct((NTOK, D), F32),
        compiler_params=_params("arbitrary"),
        name=name,
    )(*args)


def _seq_shifts(x, i):
    lseq = jnp.where(i == 0, LCTX, LLAT)
    pos = lax.broadcasted_iota(jnp.int32, x.shape, 0) & (lseq - 1)
    prev = jnp.where(pos == 0, 0.0, pltpu.roll(x, 1, 0))
    nxt = jnp.where(pos == lseq - 1, 0.0, pltpu.roll(x, ROWBLK - 1, 0))
    return prev, nxt


def _conv3(x, w, i):
    prev, nxt = _seq_shifts(x, i)
    return prev * w[0:1] + x * w[1:2] + nxt * w[2:3]


def _ab_conv_kernel(x_ref, w_ref, o_ref):
    i, j = pl.program_id(0), pl.program_id(1)
    y = _silu(_conv3(x_ref[...], w_ref[...], i))
    nrm = y * lax.rsqrt(jnp.sum(y * y, axis=-1, keepdims=True) + EPS)
    nrm = nrm * jnp.where(j < CB_KA, DK ** -0.5, 1.0)
    o_ref[...] = jnp.where(j < CB_VA, nrm, y)


def _ab_conv(proj, conv_w):
    ncb = 3 * HEADS
    return pl.pallas_call(
        _ab_conv_kernel,
        grid=(NGROUP, ncb),
        in_specs=[pl.BlockSpec((ROWBLK, LANE), lambda i, j: (i, j)),
                  pl.BlockSpec((3, LANE), lambda i, j: (0, j))],
        out_specs=pl.BlockSpec((ROWBLK, LANE), lambda i, j: (i, j)),
        out_shape=jax.ShapeDtypeStruct((NTOK, ncb * LANE), F32),
        compiler_params=_params("arbitrary", "arbitrary"),
        name="ab_conv",
    )(proj, conv_w)


def _gates_kernel(x_ref, alog_ref, dtb_ref, o_ref):
    x = x_ref[...]
    lane = lax.broadcasted_iota(jnp.int32, x.shape, 1)
    row = lax.broadcasted_iota(jnp.int32, x.shape, 0) & (CHUNK - 1)
    t = x + dtb_ref[...]
    softplus = jnp.maximum(t, 0.0) + jnp.log(1.0 + jnp.exp(-jnp.abs(t)))
    g = -jnp.exp(alog_ref[...]) * softplus
    pre, suf = g, g
    s = 1
    while s < CHUNK:
        pre = pre + jnp.where(row >= s, pltpu.roll(pre, s, 0), 0.0)
        suf = suf + jnp.where(row < CHUNK - s, pltpu.roll(suf, ROWBLK - s, 0), 0.0)
        s *= 2
    gc = jnp.where(lane < HEADS, pre, suf)
    o_ref[...] = jnp.where(lane < 2 * HEADS, gc, jax.nn.sigmoid(x))


def _gates(proj, a_log, dt_bias):
    pad = lambda v: jnp.pad(v.reshape(1, 2 * HEADS), ((0, 0), (0, LANE - 2 * HEADS)))
    return pl.pallas_call(
        _gates_kernel,
        grid=(NGROUP,),
        in_specs=[pl.BlockSpec((ROWBLK, LANE), lambda i: (i, CB_GATE)),
                  pl.BlockSpec((1, LANE), lambda i: (0, 0)),
                  pl.BlockSpec((1, LANE), lambda i: (0, 0))],
        out_specs=pl.BlockSpec((ROWBLK, LANE), lambda i: (i, 0)),
        out_shape=jax.ShapeDtypeStruct((NTOK, LANE), F32),
        compiler_params=_params("arbitrary"),
        name="ab_gates",
    )(proj, pad(a_log), pad(dt_bias))


DP_RB = 512
DP_NC = DP_RB // CHUNK
TRI_BASE = 8


def _split(x):
    hi = x.astype(BF16)
    return hi, (x - hi.astype(F32)).astype(BF16)


def _dot3(a, b):
    return _dot(a[0], b[0]) + _dot(a[0], b[1]) + _dot(a[1], b[0])


def _unit_tri_inv_batch(lms):
    ii = lax.broadcasted_iota(jnp.int32, (CHUNK, CHUNK), 0)
    jj = lax.broadcasted_iota(jnp.int32, (CHUNK, CHUNK), 1)
    same = lambda b: (ii >> int(math.log2(b))) == (jj >> int(math.log2(b)))
    eye = jnp.where(ii == jj, 1.0, 0.0)
    qs = [jnp.where(same(TRI_BASE), -lm, 0.0) for lm in lms]
    ps = [eye + q for q in qs]
    qs = [_split(q) for q in qs]
    for _ in range(int(math.log2(TRI_BASE)) - 1):
        qs = [_split(_dot3(q, q)) for q in qs]
        ps = [p + _dot3(_split(p), q) for p, q in zip(ps, qs)]
    b = TRI_BASE
    while b < CHUNK:
        off = jnp.logical_and(same(2 * b), jnp.logical_not(same(b)))
        pss = [_split(p) for p in ps]
        ts = [_dot3(p, _split(jnp.where(off, lm, 0.0))) for p, lm in zip(pss, lms)]
        ps = [p - _dot3(_split(t), p2) for p, p2, t in zip(ps, pss, ts)]
        b *= 2
    return ps


def _delta_prep_kernel(q_ref, k_ref, v_ref, g_ref, gt_ref, u_ref, wq_ref, ak_ref, gl_ref):
    h = pl.program_id(1)
    ii = lax.broadcasted_iota(jnp.int32, (CHUNK, CHUNK), 0)
    jj = lax.broadcasted_iota(jnp.int32, (CHUNK, CHUNK), 1)
    probs = []
    for c in range(DP_NC):
        rows = slice(c * CHUNK, (c + 1) * CHUNK)
        q, k, v, gts = q_ref[rows, :], k_ref[rows, :], v_ref[rows, :], g_ref[rows, :]
        kbf = k.astype(BF16)
        qk = _dot_nt(q.astype(BF16), kbf)
        for d in range(2):
            incl = (ii >= jj) if d == 0 else (ii <= jj)
            strict = (ii > jj) if d == 0 else (ii < jj)
            gcol = _pick_lane(gts, d * HEADS + h)
            bcol = _pick_lane(gts, 2 * HEADS + d * HEADS + h)
            grow = gt_ref[d * HEADS + h][c:c + 1, :]
            dmask = jnp.where(incl, jnp.exp(jnp.where(incl, gcol - grow, 0.0)), 0.0)
            kb = k * bcol
            lm = jnp.where(strict, _dot_nt(kb.astype(BF16), kbf) * dmask, 0.0)
            probs.append((c, d, rows, q, k, v, kb, gcol, bcol, dmask, qk, incl, lm))
    tmats = _unit_tri_inv_batch([p[-1] for p in probs])
    for (c, d, rows, q, k, v, kb, gcol, bcol, dmask, qk, incl, _), tmat in zip(probs, tmats):
        gam = jnp.exp(gcol)
        rhs = jnp.concatenate([v * bcol, kb * gam], axis=1).astype(BF16)
        uw = _dot(tmat.astype(BF16), rhs)
        attn = jnp.where(incl, qk * dmask, 0.0)
        gtot = gcol[CHUNK - 1:CHUNK, :] if d == 0 else gcol[0:1, :]
        kd = k * jnp.exp(gtot - gcol)
        u_ref[d, rows, :] = uw[:, :DK]
        wq_ref[d, c] = jnp.concatenate([uw[:, DK:], q * gam], axis=0).astype(BF16)
        ak_ref[d, c] = jnp.concatenate([attn, kd.T], axis=0).astype(BF16)
        gl_ref[d, c] = jnp.broadcast_to(jnp.exp(gtot), (1, LANE))


def _delta_prep(qkv, gates, gates_t):
    nrb = NTOK // DP_RB
    blk = lambda cb: pl.BlockSpec((DP_RB, LANE), lambda i, h: (i, cb + h))
    return pl.pallas_call(
        _delta_prep_kernel,
        grid=(nrb, HEADS),
        in_specs=[blk(CB_QA), blk(CB_KA), blk(CB_VA),
                  pl.BlockSpec((DP_RB, LANE), lambda i, h: (i, 0)),
                  pl.BlockSpec((None, 2 * HEADS, DP_NC, CHUNK), lambda i, h: (i, 0, 0, 0))],
        out_specs=[pl.BlockSpec((2, DP_RB, LANE), lambda i, h: (0, i, h)),
                   pl.BlockSpec((2, None, DP_NC, 2 * CHUNK, DK), lambda i, h: (0, h, i, 0, 0)),
                   pl.BlockSpec((2, None, DP_NC, CHUNK + DK, CHUNK), lambda i, h: (0, h, i, 0, 0)),
                   pl.BlockSpec((2, None, DP_NC, 1, LANE), lambda i, h: (0, h, i, 0, 0))],
        out_shape=[jax.ShapeDtypeStruct((2, NTOK, HW), F32),
                   jax.ShapeDtypeStruct((2, HEADS, NCH, 2 * CHUNK, DK), BF16),
                   jax.ShapeDtypeStruct((2, HEADS, NCH, CHUNK + DK, CHUNK), BF16),
                   jax.ShapeDtypeStruct((2, HEADS, NCH, 1, LANE), F32)],
        compiler_params=_params("arbitrary", "arbitrary"),
        name="delta_prep",
    )(qkv, qkv, qkv, gates, gates_t)


def _delta_scan_kernel(*refs, L, zero_init, aliased):
    refs = list(refs)
    u_ref, wq_ref, ak_ref, gl_ref, z_ref, ng_ref = [refs.pop(0) for _ in range(6)]
    s0_ref = None if zero_init else refs.pop(0)
    if aliased:
        refs.pop(0)
    o_ref, sfin_ref, o_scr = refs
    n = L // CHUNK

    def body(c, carry):
        new = []
        for d in range(2):
            s = carry[d]
            cc = c if d == 0 else n - 1 - c
            r0 = pl.multiple_of(cc * CHUNK, CHUNK)
            r1 = _dot(wq_ref[d, cc], s.astype(BF16))
            vnew = u_ref[d, pl.ds(r0, CHUNK), :] - r1[:CHUNK]
            r2 = _dot(ak_ref[d, cc], vnew.astype(BF16))
            o_scr[d, pl.ds(r0, CHUNK), :] = r1[CHUNK:] + r2[:CHUNK]
            new.append(s * gl_ref[d, cc] + r2[CHUNK:])
        return tuple(new)

    if zero_init:
        init = (jnp.zeros((DK, DK), F32), jnp.zeros((DK, DK), F32))
    else:
        init = (s0_ref[0], s0_ref[1])
    fin = lax.fori_loop(0, n, body, init)
    sfin_ref[0] = fin[0]
    sfin_ref[1] = fin[1]
    o = o_scr[0] + o_scr[1]
    o = o * lax.rsqrt(jnp.mean(o * o, axis=-1, keepdims=True) + EPS) * ng_ref[...]
    o_ref[...] = o * _silu(z_ref[...])


def _delta_scan(stream, u, wq, ak, gl, proj, norm_gain, s0, prev_out):
    nseq, L = stream["nseq"], stream["L"]
    rb0 = stream["row0"] // L
    n = L // CHUNK
    in_specs = [pl.BlockSpec((2, L, LANE), lambda s, h: (0, rb0 + s, h)),
                pl.BlockSpec((2, None, n, 2 * CHUNK, DK), lambda s, h: (0, h, rb0 + s, 0, 0)),
                pl.BlockSpec((2, None, n, CHUNK + DK, CHUNK), lambda s, h: (0, h, rb0 + s, 0, 0)),
                pl.BlockSpec((2, None, n, 1, LANE), lambda s, h: (0, h, rb0 + s, 0, 0)),
                pl.BlockSpec((L, LANE), lambda s, h: (rb0 + s, CB_ZA + h)),
                pl.BlockSpec((1, LANE), lambda s, h: (0, 0))]
    args = [u, wq, ak, gl, proj, norm_gain.reshape(1, DK)]
    if s0 is not None:
        in_specs.append(pl.BlockSpec((None, 2, None, DK, DK), lambda s, h: (s, 0, h, 0, 0)))
        args.append(s0)
    aliases = {}
    if prev_out is not None:
        in_specs.append(pl.BlockSpec(memory_space=pl.ANY))
        aliases = {len(args): 0}
        args.append(prev_out)
    return pl.pallas_call(
        functools.partial(_delta_scan_kernel, L=L, zero_init=s0 is None, aliased=prev_out is not None),
        grid=(nseq, HEADS),
        in_specs=in_specs,
        out_specs=[pl.BlockSpec((L, LANE), lambda s, h: (rb0 + s, h)),
                   pl.BlockSpec((None, 2, None, DK, DK), lambda s, h: (s, 0, h, 0, 0))],
        out_shape=[jax.ShapeDtypeStruct((NTOK, HW), F32),
                   jax.ShapeDtypeStruct((nseq, 2, HEADS, DK, DK), F32)],
        scratch_shapes=[pltpu.VMEM((2, L, LANE), F32)],
        input_output_aliases=aliases,
        compiler_params=_params("arbitrary", "arbitrary"),
        name="delta_scan_%d" % L,
    )(*args)


def _ret_scan_kernel(*refs, L, zero_init, aliased):
    refs = list(refs)
    q_ref, k_ref, v_ref, gb_ref, dec_ref, ng_ref = [refs.pop(0) for _ in range(6)]
    r0_ref = None if zero_init else refs.pop(0)
    if aliased:
        refs.pop(0)
    o_ref, rfin_ref, o_scr = refs
    h = pl.program_id(1)
    n = L // CHUNK
    ii = lax.broadcasted_iota(jnp.int32, (CHUNK, CHUNK), 0)
    jj = lax.broadcasted_iota(jnp.int32, (CHUNK, CHUNK), 1)
    ci = lax.broadcasted_iota(jnp.int32, (CHUNK, 1), 0).astype(F32)
    consts = []
    for d in range(2):
        lg = -jnp.exp(_pick_lane(dec_ref[...], d * HEADS + h))
        dist = (ii - jj) if d == 0 else (jj - ii)
        dmat = jnp.where(dist >= 0, jnp.exp(jnp.maximum(dist, 0).astype(F32) * lg), 0.0)
        cross = jnp.exp(((ci + 1.0) if d == 0 else (CHUNK - ci)) * lg)
        sdec = jnp.exp(((CHUNK - 1.0 - ci) if d == 0 else ci) * lg)
        consts.append((dmat, cross, sdec, jnp.exp(CHUNK * lg)))

    def body(c, carry):
        new = []
        for d in range(2):
            dmat, cross, sdec, cdec = consts[d]
            r = carry[d]
            cc = c if d == 0 else n - 1 - c
            rows = pl.ds(pl.multiple_of(cc * CHUNK, CHUNK), CHUNK)
            q = q_ref[rows, :] * DK ** -0.5
            k, v = k_ref[rows, :], v_ref[rows, :]
            vb = v.astype(BF16)
            inner = _dot_nt(q.astype(BF16), k.astype(BF16)) * dmat
            o_scr[d, rows, :] = _dot(inner.astype(BF16), vb) + _dot((q * cross).astype(BF16), r.astype(BF16))
            new.append(r * cdec + _dot_tn((k * sdec).astype(BF16), vb))
        return tuple(new)

    if zero_init:
        init = (jnp.zeros((DK, DK), F32), jnp.zeros((DK, DK), F32))
    else:
        init = (r0_ref[0], r0_ref[1])
    fin = lax.fori_loop(0, n, body, init)
    rfin_ref[0] = fin[0]
    rfin_ref[1] = fin[1]
    o = o_scr[0] + o_scr[1]
    mu = jnp.mean(o, axis=-1, keepdims=True)
    var = jnp.mean(jnp.square(o - mu), axis=-1, keepdims=True)
    o = (o - mu) * lax.rsqrt(var + EPS) * ng_ref[...]
    o_ref[...] = o * _silu(gb_ref[...])


def _ret_scan(stream, proj, ret_decay, norm_gain, r0, prev_out):
    nseq, L = stream["nseq"], stream["L"]
    rb0 = stream["row0"] // L
    blk = lambda cb: pl.BlockSpec((L, LANE), lambda s, h: (rb0 + s, cb + h))
    in_specs = [blk(CB_QB), blk(CB_KB), blk(CB_VB), blk(CB_GB),
                pl.BlockSpec((1, LANE), lambda s, h: (0, 0)),
                pl.BlockSpec((1, LANE), lambda s, h: (0, 0))]
    dec = jnp.pad(ret_decay.reshape(1, 2 * HEADS), ((0, 0), (0, LANE - 2 * HEADS)))
    args = [proj, proj, proj, proj, dec, norm_gain.reshape(1, DK)]
    if r0 is not None:
        in_specs.append(pl.BlockSpec((None, 2, None, DK, DK), lambda s, h: (s, 0, h, 0, 0)))
        args.append(r0)
    aliases = {}
    if prev_out is not None:
        in_specs.append(pl.BlockSpec(memory_space=pl.ANY))
        aliases = {len(args): 0}
        args.append(prev_out)
    return pl.pallas_call(
        functools.partial(_ret_scan_kernel, L=L, zero_init=r0 is None, aliased=prev_out is not None),
        grid=(nseq, HEADS),
        in_specs=in_specs,
        out_specs=[pl.BlockSpec((L, LANE), lambda s, h: (rb0 + s, h)),
                   pl.BlockSpec((None, 2, None, DK, DK), lambda s, h: (s, 0, h, 0, 0))],
        out_shape=[jax.ShapeDtypeStruct((NTOK, HW), F32),
                   jax.ShapeDtypeStruct((nseq, 2, HEADS, DK, DK), F32)],
        scratch_shapes=[pltpu.VMEM((2, L, LANE), F32)],
        input_output_aliases=aliases,
        compiler_params=_params("arbitrary", "arbitrary"),
        name="ret_scan_%d" % L,
    )(*args)


SCAN_G = LCTX // CHUNK
SCAN_RB = SCAN_G * CHUNK
SCAN_CTX_STEPS = TOK_CTX // SCAN_RB
SCAN_LAT_STEPS = LLAT // SCAN_RB
assert SCAN_RB == LCTX and SCAN_CTX_STEPS % SCAN_LAT_STEPS == 0


def _scan_bwd_group(t):
    i = t % SCAN_LAT_STEPS
    return jnp.where(t < SCAN_CTX_STEPS, t, t - i + (SCAN_LAT_STEPS - 1 - i))


def _scan_seq(t):
    return jnp.where(t < SCAN_CTX_STEPS, t, NCTX + (t - SCAN_CTX_STEPS) // SCAN_LAT_STEPS)


def _ab_scan_kernel(uf_ref, ub_ref, wqf_ref, wqb_ref, akf_ref, akb_ref, glf_ref, glb_ref,
                    qf_ref, kf_ref, vf_ref, qb_ref, kb_ref, vb_ref, dec_ref, s0_ref, r0_ref,
                    odf_ref, odb_ref, orf_ref, orb_ref, sfin_ref, rfin_ref, s_scr, r_scr):
    t = pl.program_id(0)
    in_ctx = t < SCAN_CTX_STEPS
    first = jnp.logical_or(in_ctx, t % SCAN_LAT_STEPS == 0)
    last = jnp.logical_or(in_ctx, t % SCAN_LAT_STEPS == SCAN_LAT_STEPS - 1)

    @pl.when(first)
    def _():
        s_scr[...] = s0_ref[...]
        r_scr[...] = r0_ref[...]

    ii = lax.broadcasted_iota(jnp.int32, (CHUNK, CHUNK), 0)
    jj = lax.broadcasted_iota(jnp.int32, (CHUNK, CHUNK), 1)
    ci = lax.broadcasted_iota(jnp.int32, (CHUNK, 1), 0).astype(F32)
    dirs = ((uf_ref, wqf_ref, akf_ref, glf_ref, qf_ref, kf_ref, vf_ref, odf_ref, orf_ref),
            (ub_ref, wqb_ref, akb_ref, glb_ref, qb_ref, kb_ref, vb_ref, odb_ref, orb_ref))
    ret_consts = {}
    for d in range(2):
        dist = (ii - jj) if d == 0 else (jj - ii)
        for h in range(HEADS):
            lane = d * HEADS + h
            lg = -jnp.exp(dec_ref[:, lane:lane + 1])
            dmat = jnp.where(dist >= 0, jnp.exp(jnp.maximum(dist, 0).astype(F32) * lg), 0.0)
            cross = jnp.exp(((ci + 1.0) if d == 0 else (CHUNK - ci)) * lg)
            sdec = jnp.exp(((CHUNK - 1.0 - ci) if d == 0 else ci) * lg)
            ret_consts[d, h] = (dmat, cross, sdec, jnp.exp(CHUNK * lg))
    s_cur = {(d, h): s_scr[d, h] for d in range(2) for h in range(HEADS)}
    r_cur = {(d, h): r_scr[d, h] for d in range(2) for h in range(HEADS)}
    for c in range(SCAN_G):
        for d in range(2):
            u_ref, wq_ref, ak_ref, gl_ref, q_ref, k_ref, v_ref, od_ref, or_ref = dirs[d]
            cc = c if d == 0 else SCAN_G - 1 - c
            rows = slice(cc * CHUNK, (cc + 1) * CHUNK)
            for h in range(HEADS):
                cols = slice(h * DK, (h + 1) * DK)
                s = s_cur[d, h]
                r1 = _dot(wq_ref[h, cc], s.astype(BF16))
                vnew = u_ref[rows, cols] - r1[:CHUNK]
                r2 = _dot(ak_ref[h, cc], vnew.astype(BF16))
                od_ref[rows, cols] = r1[CHUNK:] + r2[:CHUNK]
                s_cur[d, h] = s * gl_ref[h, cc] + r2[CHUNK:]
                dmat, cross, sdec, cdec = ret_consts[d, h]
                r = r_cur[d, h]
                q = q_ref[rows, cols] * DK ** -0.5
                k = k_ref[rows, cols]
                vb = v_ref[rows, cols].astype(BF16)
                inner = _dot_nt(q.astype(BF16), k.astype(BF16)) * dmat
                or_ref[rows, cols] = _dot(inner.astype(BF16), vb) + _dot((q * cross).astype(BF16), r.astype(BF16))
                r_cur[d, h] = r * cdec + _dot_tn((k * sdec).astype(BF16), vb)
    for d in range(2):
        for h in range(HEADS):
            s_scr[d, h] = s_cur[d, h]
            r_scr[d, h] = r_cur[d, h]

    @pl.when(last)
    def _():
        sfin_ref[...] = s_scr[...]
        rfin_ref[...] = r_scr[...]


def _ab_scan(u, wq, ak, gl, proj, ret_decay, s0, r0):
    nseq = NCTX + NLAT
    fwd = lambda t: t
    row = lambda g, cb: pl.BlockSpec((SCAN_RB, HW), lambda t: (g(t), cb))
    u_spec = lambda d, g: pl.BlockSpec((None, SCAN_RB, HW), lambda t: (d, g(t), 0))
    op_spec = lambda d, g, a, b: pl.BlockSpec((None, HEADS, SCAN_G, a, b), lambda t: (d, 0, g(t), 0, 0))
    state = pl.BlockSpec((None, 2, HEADS, DK, DK), lambda t: (_scan_seq(t), 0, 0, 0, 0))
    in_specs = [u_spec(0, fwd), u_spec(1, _scan_bwd_group),
                op_spec(0, fwd, 2 * CHUNK, DK), op_spec(1, _scan_bwd_group, 2 * CHUNK, DK),
                op_spec(0, fwd, CHUNK + DK, CHUNK), op_spec(1, _scan_bwd_group, CHUNK + DK, CHUNK),
                op_spec(0, fwd, 1, LANE), op_spec(1, _scan_bwd_group, 1, LANE),
                row(fwd, CB_QB // HEADS), row(fwd, CB_KB // HEADS), row(fwd, CB_VB // HEADS),
                row(_scan_bwd_group, CB_QB // HEADS), row(_scan_bwd_group, CB_KB // HEADS),
                row(_scan_bwd_group, CB_VB // HEADS),
                pl.BlockSpec((1, LANE), lambda t: (0, 0)), state, state]
    dec = jnp.pad(ret_decay.reshape(1, 2 * HEADS), ((0, 0), (0, LANE - 2 * HEADS)))
    o_sds = jax.ShapeDtypeStruct((NTOK, HW), F32)
    st_sds = jax.ShapeDtypeStruct((nseq, 2, HEADS, DK, DK), F32)
    return pl.pallas_call(
        _ab_scan_kernel,
        grid=(NTOK // SCAN_RB,),
        in_specs=in_specs,
        out_specs=[row(fwd, 0), row(_scan_bwd_group, 0), row(fwd, 0), row(_scan_bwd_group, 0), state, state],
        out_shape=[o_sds, o_sds, o_sds, o_sds, st_sds, st_sds],
        scratch_shapes=[pltpu.VMEM((2, HEADS, DK, DK), F32), pltpu.VMEM((2, HEADS, DK, DK), F32)],
        compiler_params=_params("arbitrary"),
        name="ab_scan",
    )(u, u, wq, wq, ak, ak, gl, gl, proj, proj, proj, proj, proj, proj, dec, s0, r0)


def _ab_out_kernel(odf_ref, odb_ref, orf_ref, orb_ref, z_ref, gb_ref, na_ref, nb_ref, w_ref, x_ref, gate_ref,
                   o_ref):
    oa = odf_ref[...] + odb_ref[...]
    ob = orf_ref[...] + orb_ref[...]
    z, gb = z_ref[...], gb_ref[...]
    parts = []
    for h in range(HEADS):
        cols = slice(h * DK, (h + 1) * DK)
        a = oa[:, cols]
        a = a * lax.rsqrt(jnp.mean(a * a, axis=-1, keepdims=True) + EPS) * na_ref[...]
        parts.append(a * _silu(z[:, cols]))
    for h in range(HEADS):
        cols = slice(h * DK, (h + 1) * DK)
        b = ob[:, cols]
        mu = jnp.mean(b, axis=-1, keepdims=True)
        var = jnp.mean(jnp.square(b - mu), axis=-1, keepdims=True)
        parts.append((b - mu) * lax.rsqrt(var + EPS) * nb_ref[...] * _silu(gb[:, cols]))
    a = jnp.concatenate(parts, axis=1).astype(BF16)
    o_ref[...] = x_ref[...] + gate_ref[...] * _dot(a, w_ref[...])


def _ab_out(odf, odb, orf, orb, proj, norm_a, norm_b, w_out, x, gate):
    grp = lambda i: (i * MMR_TM // ROWBLK, 0, 0)
    tile = lambda cb: pl.BlockSpec((MMR_TM, HW), lambda i: (i, cb))
    vec = pl.BlockSpec((1, DK), lambda i: (0, 0))
    return pl.pallas_call(
        _ab_out_kernel,
        grid=(NTOK // MMR_TM,),
        in_specs=[tile(0), tile(0), tile(0), tile(0), tile(CB_ZA // HEADS), tile(CB_GB // HEADS), vec, vec,
                  pl.BlockSpec((2 * HW, D), lambda i: (0, 0)),
                  pl.BlockSpec((MMR_TM, D), lambda i: (i, 0)), pl.BlockSpec((None, 1, D), grp)],
        out_specs=pl.BlockSpec((MMR_TM, D), lambda i: (i, 0)),
        out_shape=jax.ShapeDtypeStruct((NTOK, D), F32),
        compiler_params=_params("arbitrary"),
        name="ab_out",
    )(odf, odb, orf, orb, proj, proj, norm_a.reshape(1, DK), norm_b.reshape(1, DK), w_out, x, gate)


def _hy_conv_kernel(x0_ref, x1_ref, v_ref, w0_ref, w1_ref, wv_ref, b0_ref, b1_ref, bv_ref, x0o_ref, p_ref):
    i = pl.program_id(0)
    x0o_ref[...] = _conv3(x0_ref[...], w0_ref[...], i) + b0_ref[...]
    x1 = _conv3(x1_ref[...], w1_ref[...], i) + b1_ref[...]
    v = _conv3(v_ref[...], wv_ref[...], i) + bv_ref[...]
    p_ref[...] = v * x1


def _hy_conv(u3, conv_w, conv_b):
    ncb = D // LANE
    xb = lambda k: pl.BlockSpec((ROWBLK, LANE), lambda i, j: (i, k * ncb + j))
    wb = lambda k: pl.BlockSpec((3, LANE), lambda i, j: (0, k * ncb + j))
    bb = lambda k: pl.BlockSpec((1, LANE), lambda i, j: (0, k * ncb + j))
    cb = conv_b.reshape(1, 3 * D)
    return pl.pallas_call(
        _hy_conv_kernel,
        grid=(NGROUP, ncb),
        in_specs=[xb(0), xb(1), xb(2), wb(0), wb(1), wb(2), bb(0), bb(1), bb(2)],
        out_specs=[pl.BlockSpec((ROWBLK, LANE), lambda i, j: (i, j))] * 2,
        out_shape=[jax.ShapeDtypeStruct((NTOK, D), F32)] * 2,
        compiler_params=_params("arbitrary", "arbitrary"),
        name="hy_conv",
    )(u3, u3, u3, conv_w, conv_w, conv_w, cb, cb, cb)


def _filter_features(L):
    r = np.arange(2 * L)
    pos = np.where(r < L, r, 2 * L - r) % L
    t = pos / (L - 1.0)
    bands = np.linspace(1e-4, HY_BANDS - 1, HY_BANDS)
    ang = 2.0 * np.pi * np.outer(pos, bands) / L
    z = np.zeros((2 * L, HY_FW), np.float64)
    z[:, 0] = t
    z[:, 1:1 + HY_BANDS] = np.cos(ang)
    z[:, 1 + HY_BANDS:HY_EMB] = -np.sin(ang)
    z[:, HY_EMB] = (r != L)
    return z.astype(np.float32)


def _filter_kernel(z_ref, w1_ref, b1_ref, f1_ref, w2_ref, b2_ref, f2_ref, w3_ref, dl_ref, o_ref):
    z = z_ref[...]
    hid = jnp.sin(f1_ref[...] * (_dot(z, w1_ref[...], HI) + b1_ref[...]))
    hid = jnp.sin(f2_ref[...] * (_dot(hid, w2_ref[...], HI) + b2_ref[...]))
    filt = _dot(hid, w3_ref[...], HI)
    window = jnp.exp(-z[:, 0:1] * dl_ref[...]) * z[:, HY_EMB:HY_EMB + 1]
    o_ref[...] = filt * window


def _hy_filter(L, w1, b1, f1, w2, b2, f2, w3):
    rb = min(512, L)
    nblk = 2 * L // rb
    z = jnp.asarray(_filter_features(L))
    w1p = jnp.pad(w1, ((0, HY_FW - HY_EMB), (0, 0)))
    min_decay = math.log(HY_TARGET) / HY_SLOW_PCT
    max_decay = math.log(HY_TARGET) / HY_FAST_PCT
    deltas = jnp.asarray(np.abs(np.linspace(min_decay, max_decay, D)).astype(np.float32).reshape(1, D))
    vec = lambda v: v.reshape(1, HY_FW)
    full = lambda shp: pl.BlockSpec(shp, lambda i: (0, 0))
    return pl.pallas_call(
        _filter_kernel,
        grid=(nblk,),
        in_specs=[pl.BlockSpec((rb, HY_FW), lambda i: (i, 0)),
                  full((HY_FW, HY_FW)), full((1, HY_FW)), full((1, HY_FW)),
                  full((HY_FW, HY_FW)), full((1, HY_FW)), full((1, HY_FW)),
                  pl.BlockSpec((HY_FW, D), lambda i: (0, i // (nblk // 2))),
                  full((1, D))],
        out_specs=pl.BlockSpec((rb, D), lambda i: (i, 0)),
        out_shape=jax.ShapeDtypeStruct((2 * L, D), F32),
        compiler_params=_params("arbitrary"),
        name="hy_filter_%d" % L,
    )(z, w1p, vec(b1), vec(f1), w2, vec(b2), vec(f2), w3, deltas)


FFT_RT = 8


def _cis(num, den):
    ang = -2.0 * np.pi * (num % den) / den
    return np.cos(ang), np.sin(ang)


def _fft_consts(L):
    n = 2 * L
    n1 = n // FFT_N2
    k1 = np.arange(n1)
    fr, fi = _cis(np.outer(k1, k1), n1)
    half = n1 // 2
    sig = np.block([[fr[:, :half], -fi[:, :half]], [fi[:, :half], fr[:, :half]]])
    ker = np.concatenate([fr, fi], axis=0)
    cr, ci = fr[:, :half].T, -fi[:, :half].T
    inv = np.stack([np.concatenate([cr, -ci], axis=1), np.concatenate([ci, cr], axis=1)]) / n
    k2 = np.arange(FFT_N2)
    gr, gi = _cis(np.outer(k2, k2), FFT_N2)
    f2 = np.block([[gr, -gi], [gi, gr]])
    f2inv = np.block([[gr, gi], [-gi, gr]])
    tr, ti = _cis(np.outer(k1, k2), n)
    f = lambda a: np.asarray(a, np.float32)
    return dict(n1=n1, sig=f(sig), ker=f(ker), inv=f(inv), f2=f(f2), f2inv=f(f2inv),
                twr=f(tr).reshape(n1, FFT_N2, 1), twi=f(ti).reshape(n1, FFT_N2, 1))


def _lin_rows(mat, rows):
    out = []
    for m in range(mat.shape[0]):
        acc = None
        for k in range(mat.shape[1]):
            cf = float(mat[m, k])
            if abs(cf) < 1e-9:
                continue
            term = rows[k] if abs(cf - 1.0) < 1e-9 else (-rows[k] if abs(cf + 1.0) < 1e-9 else cf * rows[k])
            acc = term if acc is None else acc + term
        out.append(jnp.zeros_like(rows[0]) if acc is None else acc)
    return out


def _twiddle_dft(slab_r, slab_i, twr, twi, f2):
    x = jnp.concatenate([slab_r * twr - slab_i * twi, slab_r * twi + slab_i * twr], axis=0)
    y = _dot3(f2, _split(x))
    return y[:FFT_N2], y[FFT_N2:]


def _filter_idft_twiddle(yr, yi, kr, ki, twr, twi, f2inv):
    z = jnp.concatenate([yr * kr - yi * ki, yr * ki + yi * kr], axis=0)
    w = _dot3(f2inv, _split(z))
    wr, wi = w[:FFT_N2], w[FFT_N2:]
    return wr * twr + wi * twi, wi * twr - wr * twi


def _fft1_kernel(m_ref, *refs):
    o_ref = refs[-1]
    m = _split(m_ref[...])
    n1 = o_ref.shape[1]
    for r in range(FFT_RT):
        parts = [x_ref[c:c + 32, r, :] for x_ref in refs[:-1] for c in range(0, x_ref.shape[0], 32)]
        x = jnp.concatenate(parts, axis=0) if len(parts) > 1 else parts[0]
        y = _dot3(m, _split(x))
        o_ref[0, :, r, :] = y[:n1]
        o_ref[1, :, r, :] = y[n1:]


def _fft_stage1(consts, mat, xs, lead_blocks):
    n1 = consts["n1"]
    rows = mat.shape[1] // len(xs)
    specs = [pl.BlockSpec((rows, FFT_RT, D), functools.partial(lambda j, lb: (lb, j, 0), lb=lb))
             for lb in lead_blocks]
    return pl.pallas_call(
        _fft1_kernel,
        grid=(FFT_N2 // FFT_RT,),
        in_specs=[pl.BlockSpec(mat.shape, lambda j: (0, 0))] + specs,
        out_specs=pl.BlockSpec((2, n1, FFT_RT, D), lambda j: (0, 0, j, 0)),
        out_shape=jax.ShapeDtypeStruct((2, n1, FFT_N2, D), F32),
        compiler_params=_params("arbitrary"),
        name="fft1_%d" % len(xs),
    )(jnp.asarray(mat), *xs)


def _fft2_kernel(a_ref, twr_ref, twi_ref, f2_ref, *rest, conv):
    twr, twi = twr_ref[...], twi_ref[...]
    yr, yi = _twiddle_dft(a_ref[0], a_ref[1], twr, twi, _split(f2_ref[...]))
    if conv:
        kf_ref, f2inv_ref, o_ref = rest
        yr, yi = _filter_idft_twiddle(yr, yi, kf_ref[0], kf_ref[1], twr, twi, _split(f2inv_ref[...]))
    else:
        o_ref, = rest
    o_ref[0] = yr
    o_ref[1] = yi


def _fft_stage2(consts, a, kf):
    n1 = consts["n1"]
    slab = pl.BlockSpec((2, None, FFT_N2, D), lambda k: (0, k, 0, 0))
    tw = pl.BlockSpec((None, FFT_N2, 1), lambda k: (k, 0, 0))
    mat = pl.BlockSpec((2 * FFT_N2, 2 * FFT_N2), lambda k: (0, 0))
    in_specs = [slab, tw, tw, mat]
    args = [a, jnp.asarray(consts["twr"]), jnp.asarray(consts["twi"]), jnp.asarray(consts["f2"])]
    if kf is not None:
        in_specs += [slab, mat]
        args += [kf, jnp.asarray(consts["f2inv"])]
    return pl.pallas_call(
        functools.partial(_fft2_kernel, conv=kf is not None),
        grid=(n1,),
        in_specs=in_specs,
        out_specs=slab,
        out_shape=jax.ShapeDtypeStruct(a.shape, F32),
        compiler_params=_params("arbitrary"),
        name="fft2_%s" % ("conv" if kf is not None else "spec"),
    )(*args)


def _fft3_kernel(m_ref, b_ref, p_ref, x0_ref, bias_ref, *rest):
    o_ref = rest[-1]
    m = _split(m_ref[...])
    bias = bias_ref[...]
    for r in range(FFT_RT):
        x = jnp.concatenate([b_ref[0, :, r, :], b_ref[1, :, r, :]], axis=0)
        y = _dot3(m, _split(x))
        o_ref[:, r, :] = (y + p_ref[:, r, :] * bias) * x0_ref[:, r, :]


def _fft_stage3(consts, b, pv, xv, bias, lead_block, prev_out):
    n1 = consts["n1"]
    half = n1 // 2
    seq = pl.BlockSpec((half, FFT_RT, D), lambda j, s: (lead_block + s, j, 0))
    in_specs = [pl.BlockSpec((None, half, 2 * n1), lambda j, s: (s, 0, 0)),
                pl.BlockSpec((2, n1, FFT_RT, D), lambda j, s: (0, 0, j, 0)),
                seq, seq, pl.BlockSpec((1, D), lambda j, s: (0, 0)),
                pl.BlockSpec(memory_space=pl.ANY)]
    return pl.pallas_call(
        _fft3_kernel,
        grid=(FFT_N2 // FFT_RT, 2),
        in_specs=in_specs,
        out_specs=seq,
        out_shape=jax.ShapeDtypeStruct(pv.shape, F32),
        input_output_aliases={5: 0},
        compiler_params=_params("arbitrary", "arbitrary"),
        name="fft3",
    )(jnp.asarray(consts["inv"]), b, pv, xv, bias.reshape(1, D), prev_out)


def _fft_ctx_kernel(p_ref, x0_ref, kern_ref, twr_ref, twi_ref, f2_ref, f2inv_ref, bias_ref, o_ref, kf_ref,
                    *, n1, sig, ker, inv):
    f2 = _split(f2_ref[...])

    @pl.when(pl.program_id(0) == 0)
    def _():
        a = _lin_rows(ker, [kern_ref[r] for r in range(n1)])
        for k in range(n1):
            kf_ref[0, k], kf_ref[1, k] = _twiddle_dft(a[k], a[n1 + k], twr_ref[k], twi_ref[k], f2)

    f2inv = _split(f2inv_ref[...])
    a = _lin_rows(sig, [p_ref[r] for r in range(n1)])
    br, bi = [], []
    for k in range(n1):
        twr, twi = twr_ref[k], twi_ref[k]
        yr, yi = _twiddle_dft(a[k], a[n1 + k], twr, twi, f2)
        wr, wi = _filter_idft_twiddle(yr, yi, kf_ref[0, k], kf_ref[1, k], twr, twi, f2inv)
        br.append(wr)
        bi.append(wi)
    out = _lin_rows(inv, br + bi)
    bias = bias_ref[...]
    for m, y in enumerate(out):
        o_ref[m] = (y + p_ref[m] * bias) * x0_ref[m]


def _fft_ctx(consts, pv, xv, kern, bias):
    n1 = consts["n1"]
    inv = np.concatenate([consts["inv"][0], consts["inv"][1]], axis=0)
    pair = pl.BlockSpec((n1, FFT_N2, D), lambda q: (q, 0, 0))
    full = lambda shp: pl.BlockSpec(shp, lambda q: (0,) * len(shp))
    return pl.pallas_call(
        functools.partial(_fft_ctx_kernel, n1=n1, sig=consts["sig"], ker=consts["ker"], inv=inv),
        grid=(NCTX // 2,),
        in_specs=[pair, pair, full((n1, FFT_N2, D)), full((n1, FFT_N2, 1)), full((n1, FFT_N2, 1)),
                  full((2 * FFT_N2, 2 * FFT_N2)), full((2 * FFT_N2, 2 * FFT_N2)), full((1, D))],
        out_specs=pair,
        out_shape=jax.ShapeDtypeStruct(pv.shape, F32),
        scratch_shapes=[pltpu.VMEM((2, n1, FFT_N2, D), F32)],
        compiler_params=_params("arbitrary"),
        name="fft_ctx",
    )(pv, xv, kern, jnp.asarray(consts["twr"]), jnp.asarray(consts["twi"]), jnp.asarray(consts["f2"]),
      jnp.asarray(consts["f2inv"]), bias.reshape(1, D))


def _hy_long_conv(p, x0, filt_w, bias):
    slabs = (NTOK // FFT_N2, FFT_N2, D)
    pv, xv = p.reshape(slabs), x0.reshape(slabs)
    cc = _fft_consts(LCTX)
    out = _fft_ctx(cc, pv, xv, _hy_filter(LCTX, *filt_w).reshape(cc["n1"], FFT_N2, D), bias)
    lc = _fft_consts(LLAT)
    n1 = lc["n1"]
    kern = _hy_filter(LLAT, *filt_w).reshape(n1, FFT_N2, D)
    kf = _fft_stage2(lc, _fft_stage1(lc, lc["ker"], [kern], [0]), None)
    lat0 = TOK_CTX // FFT_N2 // (n1 // 2)
    a = _fft_stage1(lc, lc["sig"], [pv, pv], [lat0, lat0 + 1])
    b = _fft_stage2(lc, a, kf)
    return _fft_stage3(lc, b, pv, xv, bias, lat0, out).reshape(NTOK, D)


FFN_TM = 1024
FFN_TF = 256
FFN_HALO = 128


def _ffn_kernel(*refs, final):
    refs = list(refs)
    (xc_ref, xp_ref, xn_ref, g_ref, sh_ref, sc_ref, wg_ref, wu_ref, wd_ref, cw_ref, cb_ref,
     gate_ref) = [refs.pop(0) for _ in range(12)]
    fn_ref = refs.pop(0) if final else None
    o_ref, h_ref, acc_ref = refs
    i, f = pl.program_id(0), pl.program_id(1)
    ext = FFN_TM + 2 * FFN_HALO
    is_lat = i >= TOK_CTX // FFN_TM

    @pl.when(f == 0)
    def _():
        def norm(x):
            y = x * lax.rsqrt(jnp.mean(x * x, axis=-1, keepdims=True) + EPS) * g_ref[...]
            return (y * (1.0 + sc_ref[...]) + sh_ref[...]).astype(BF16)
        seq_pos = (i * FFN_TM - TOK_CTX) & (LLAT - 1)
        keep_prev = jnp.logical_and(is_lat, seq_pos != 0)
        keep_next = jnp.logical_and(is_lat, seq_pos != LLAT - FFN_TM)
        h_ref[0:FFN_HALO, :] = jnp.where(keep_prev, norm(xp_ref[...]), 0.0).astype(BF16)
        h_ref[FFN_HALO:FFN_HALO + FFN_TM, :] = norm(xc_ref[...])
        h_ref[FFN_HALO + FFN_TM:ext, :] = jnp.where(keep_next, norm(xn_ref[...]), 0.0).astype(BF16)
        acc_ref[...] = jnp.zeros_like(acc_ref)

    gate = _dot(h_ref[...], wg_ref[...])
    up = _dot(h_ref[FFN_HALO:FFN_HALO + FFN_TM, :], wu_ref[...])
    period = jnp.where(is_lat, GRID_W, LCTX)
    r = lax.broadcasted_iota(jnp.int32, (ext, 1), 0)
    col = (r + FFN_HALO) & (period - 1)
    left = jnp.where(col == 0, 0.0, pltpu.roll(gate, 1, 0))
    right = jnp.where(col == period - 1, 0.0, pltpu.roll(gate, ext - 1, 0))
    kidx = lax.broadcasted_iota(jnp.int32, (9, 1), 0)
    mid_row = jnp.logical_and(kidx >= 3, kidx < 6)
    w = jnp.where(jnp.logical_or(is_lat, mid_row), cw_ref[...], 0.0)
    tap = lambda k, lo: (left[lo:lo + FFN_TM] * w[3 * k:3 * k + 1] + gate[lo:lo + FFN_TM] * w[3 * k + 1:3 * k + 2]
                         + right[lo:lo + FFN_TM] * w[3 * k + 2:3 * k + 3])
    z = tap(0, FFN_HALO - GRID_W) + tap(1, FFN_HALO) + tap(2, FFN_HALO + GRID_W) + cb_ref[...]
    act = (0.5 * z) * (1.0 + jnp.tanh(0.5 * z)) * up
    acc_ref[...] += _dot(act.astype(BF16), wd_ref[...])

    @pl.when(f == pl.num_programs(1) - 1)
    def _():
        y = xc_ref[...] + gate_ref[...] * acc_ref[...]
        if final:
            y = y * lax.rsqrt(jnp.mean(y * y, axis=-1, keepdims=True) + EPS) * fn_ref[...]
        o_ref[...] = y


def _ffn(x, gain, shift, scale, w_gu, conv_w, conv_b, w_down, gate, final_gain):
    nf = D_FF // FFN_TF
    hb = FFN_TM // FFN_HALO
    nhb = NTOK // FFN_HALO
    grp = lambda i, f: (i * FFN_TM // ROWBLK, 0, 0)
    vec = pl.BlockSpec((None, 1, D), grp)
    in_specs = [pl.BlockSpec((FFN_TM, D), lambda i, f: (i, 0)),
                pl.BlockSpec((FFN_HALO, D), lambda i, f: (jnp.maximum(i * hb - 1, 0), 0)),
                pl.BlockSpec((FFN_HALO, D), lambda i, f: (jnp.minimum((i + 1) * hb, nhb - 1), 0)),
                pl.BlockSpec((1, D), lambda i, f: (0, 0)), vec, vec,
                pl.BlockSpec((D, FFN_TF), lambda i, f: (0, f)),
                pl.BlockSpec((D, FFN_TF), lambda i, f: (0, nf + f)),
                pl.BlockSpec((FFN_TF, D), lambda i, f: (f, 0)),
                pl.BlockSpec((9, FFN_TF), lambda i, f: (0, f)),
                pl.BlockSpec((1, FFN_TF), lambda i, f: (0, f)),
                vec]
    args = [x, x, x, gain.reshape(1, D), shift, scale, w_gu, w_gu, w_down, conv_w.reshape(9, D_FF),
            conv_b.reshape(1, D_FF), gate]
    if final_gain is not None:
        in_specs.append(pl.BlockSpec((1, D), lambda i, f: (0, 0)))
        args.append(final_gain.reshape(1, D))
    return pl.pallas_call(
        functools.partial(_ffn_kernel, final=final_gain is not None),
        grid=(NTOK // FFN_TM, nf),
        in_specs=in_specs,
        out_specs=pl.BlockSpec((FFN_TM, D), lambda i, f: (i, 0)),
        out_shape=jax.ShapeDtypeStruct((NTOK, D), F32),
        scratch_shapes=[pltpu.VMEM((FFN_TM + 2 * FFN_HALO, D), BF16), pltpu.VMEM((FFN_TM, D), F32)],
        compiler_params=_params("arbitrary", "arbitrary"),
        name="ffn",
    )(*args)


def _ab_w_in_cols(w):
    gate_c0 = 3 * HW + HW
    return jnp.concatenate([w[:, :gate_c0], w[:, gate_c0 + 4 * HEADS:], w[:, gate_c0:gate_c0 + 4 * HEADS],
                            jnp.zeros((D, LANE - 4 * HEADS), F32)], axis=1).astype(BF16)


def _ab_mixer(proj, conv_w, a_log, dt_bias, ret_decay, s_delta0, s_ret0):
    qkv = _ab_conv(proj, conv_w)
    gates = _gates(proj, a_log, dt_bias)
    gates_t = gates[:, :2 * HEADS].reshape(NTOK // DP_RB, DP_NC, CHUNK, 2 * HEADS).transpose(0, 3, 1, 2)
    u, wq, ak, gl = _delta_prep(qkv, gates, gates_t)
    zeros = jnp.zeros((NCTX, 2, HEADS, DK, DK), F32)
    odf, odb, orf, orb, sfin, rfin = _ab_scan(u, wq, ak, gl, proj, ret_decay,
                                              jnp.concatenate([zeros, s_delta0], axis=0),
                                              jnp.concatenate([zeros, s_ret0], axis=0))
    return (odf, odb, orf, orb), sfin[:NCTX], rfin[:NCTX]


def kernel(x_prompt, x_sample, state_delta, state_ret, c, c_ctx, mod_w, mod_b, norm1, norm2, ab_w_in, ab_conv, ab_a_log, ab_dt_bias, ab_norm_a, ab_norm_b, ab_ret_decay, ab_w_out, hy_w_in, hy_b_in, hy_conv_w, hy_conv_b, hy_f_w1, hy_f_b1, hy_f_freq1, hy_f_w2, hy_f_b2, hy_f_freq2, hy_f_w3, hy_f_bias, hy_w_out, hy_b_out, ffn_w_gate, ffn_w_up, ffn_conv, ffn_conv_b, ffn_w_down, final_norm):
    x = jnp.concatenate([x_prompt.reshape(TOK_CTX, D), x_sample.reshape(TOK_LAT, D)], axis=0)
    cvec = jnp.concatenate([c_ctx[None], c, jnp.zeros((8 - 1 - NLAT, D), F32)], axis=0)
    mod = _mod_all(cvec, mod_w, mod_b)
    new_delta, new_ret = [], []
    for l in range(DEPTH):
        m = [mod[l, :NGROUP, k * D:(k + 1) * D].reshape(NGROUP, 1, D) for k in range(6)]
        j = l // 2
        if l % 2 == 0:
            proj = _nmm(x, norm1[l], m[0], m[1], _ab_w_in_cols(ab_w_in[j]), None, AB_N // 3, "ab_in")
            heads, sd, sr = _ab_mixer(proj, ab_conv[j], ab_a_log[j], ab_dt_bias[j], ab_ret_decay[j],
                                      state_delta[:, j], state_ret[:, j])
            new_delta.append(sd)
            new_ret.append(sr)
            x = _ab_out(*heads, proj, ab_norm_a[j], ab_norm_b[j], ab_w_out[j].astype(BF16), x, m[2])
        else:
            u3 = _nmm(x, norm1[l], m[0], m[1], hy_w_in[j].astype(BF16), hy_b_in[j], 512, "hy_in")
            x0, p = _hy_conv(u3, hy_conv_w[j], hy_conv_b[j])
            filt_w = (hy_f_w1[j], hy_f_b1[j], hy_f_freq1[j], hy_f_w2[j], hy_f_b2[j], hy_f_freq2[j], hy_f_w3[j])
            y = _hy_long_conv(p, x0, filt_w, hy_f_bias[j])
            x = _mmr([y], [hy_w_out[j].astype(BF16)], hy_b_out[j], x, m[2], None, "hy_out")
        w_gu = jnp.concatenate([ffn_w_gate[l], ffn_w_up[l]], axis=1).astype(BF16)
        x = _ffn(x, norm2[l], m[3], m[4], w_gu, ffn_conv[l], ffn_conv_b[l], ffn_w_down[l].astype(BF16), m[5],
                 final_norm if l == DEPTH - 1 else None)
    y_prompt = x[:TOK_CTX].reshape(NCTX, LCTX, D)
    y_sample = x[TOK_CTX:].reshape(NLAT, LLAT, D)
    return (y_prompt, y_sample, jnp.stack(new_delta, axis=1), jnp.stack(new_ret, axis=1))
```

```python
import functools
import math

import numpy as np
import jax
import jax.numpy as jnp
from jax import lax
from jax.experimental import pallas as pl
from jax.experimental.pallas import tpu as pltpu

F32, BF16 = jnp.float32, jnp.bfloat16
HI = lax.Precision.HIGHEST

D = 1024
NCTX, LCTX = 16, 256
NLAT, LLAT = 2, 4096
DEPTH = 4
TOK_CTX = NCTX * LCTX
TOK_LAT = NLAT * LLAT
NTOK = TOK_CTX + TOK_LAT
ROWBLK = 4096
NGROUP = NTOK // ROWBLK
GRID_W = 64
CHUNK = 64
NCH = NTOK // CHUNK
EPS = 1e-6
HEADS = 4
DK = 128
HW = HEADS * DK
LANE = 128
D_FF = 2816
HY_EMB = 33
HY_BANDS = 16
HY_FW = 64
HY_TARGET = 1e-2
HY_FAST_PCT = 0.3
HY_SLOW_PCT = 1.5
FFT_N2 = 128
VMEM_LIMIT = 52 * 1024 * 1024

CB_QA, CB_KA, CB_VA, CB_ZA, CB_QB, CB_KB, CB_VB, CB_GB, CB_GATE = 0, 4, 8, 12, 16, 20, 24, 28, 32
AB_N = 33 * LANE

CTX = dict(nseq=NCTX, L=LCTX, row0=0)
LAT = dict(nseq=NLAT, L=LLAT, row0=TOK_CTX)


def _params(*sem):
    return pltpu.CompilerParams(dimension_semantics=sem, vmem_limit_bytes=VMEM_LIMIT)


def _silu(x):
    return x * jax.nn.sigmoid(x)


def _dot(a, b, precision=None):
    return jnp.dot(a, b, preferred_element_type=F32, precision=precision)


def _dot_nt(a, b):
    return lax.dot_general(a, b, (((1,), (1,)), ((), ())), preferred_element_type=F32)


def _dot_tn(a, b):
    return lax.dot_general(a, b, (((0,), (0,)), ((), ())), preferred_element_type=F32)


def _pick_lane(x, lane):
    li = lax.broadcasted_iota(jnp.int32, x.shape, 1)
    return jnp.sum(jnp.where(li == lane, x, 0.0), axis=-1, keepdims=True)


def _mod_kernel(c_ref, w_ref, b_ref, o_ref):
    s = _silu(c_ref[...])
    o_ref[...] = _dot(s.astype(BF16), w_ref[...].astype(BF16)) + b_ref[...]


def _mod_all(cvec, mod_w, mod_b):
    tn = 1536
    return pl.pallas_call(
        _mod_kernel,
        grid=(DEPTH, 6 * D // tn),
        in_specs=[pl.BlockSpec((8, D), lambda l, j: (0, 0)),
                  pl.BlockSpec((None, D, tn), lambda l, j: (l, 0, j)),
                  pl.BlockSpec((None, 1, tn), lambda l, j: (l, 0, j))],
        out_specs=pl.BlockSpec((None, 8, tn), lambda l, j: (l, 0, j)),
        out_shape=jax.ShapeDtypeStruct((DEPTH, 8, 6 * D), F32),
        compiler_params=_params("arbitrary", "arbitrary"),
        name="mod",
    )(cvec, mod_w, mod_b.reshape(DEPTH, 1, 6 * D))


NMM_TM = 1024


def _nmm_kernel(x_ref, g_ref, sh_ref, sc_ref, w_ref, *rest, has_bias):
    if has_bias:
        b_ref, o_ref, h_ref = rest
    else:
        o_ref, h_ref = rest

    @pl.when(pl.program_id(1) == 0)
    def _():
        x = x_ref[...]
        y = x * lax.rsqrt(jnp.mean(x * x, axis=-1, keepdims=True) + EPS) * g_ref[...]
        h_ref[...] = (y * (1.0 + sc_ref[...]) + sh_ref[...]).astype(BF16)

    acc = _dot(h_ref[...], w_ref[...])
    if has_bias:
        acc = acc + b_ref[...]
    o_ref[...] = acc


def _nmm(x, gain, shift, scale, w, bias, tn, name):
    n = w.shape[1]
    grp = lambda i, j: (i * NMM_TM // ROWBLK, 0, 0)
    in_specs = [pl.BlockSpec((NMM_TM, D), lambda i, j: (i, 0)),
                pl.BlockSpec((1, D), lambda i, j: (0, 0)),
                pl.BlockSpec((None, 1, D), grp),
                pl.BlockSpec((None, 1, D), grp),
                pl.BlockSpec((D, tn), lambda i, j: (0, j))]
    args = [x, gain.reshape(1, D), shift, scale, w]
    if bias is not None:
        in_specs.append(pl.BlockSpec((1, tn), lambda i, j: (0, j)))
        args.append(bias.reshape(1, n))
    return pl.pallas_call(
        functools.partial(_nmm_kernel, has_bias=bias is not None),
        grid=(NTOK // NMM_TM, n // tn),
        in_specs=in_specs,
        out_specs=pl.BlockSpec((NMM_TM, tn), lambda i, j: (i, j)),
        out_shape=jax.ShapeDtypeStruct((NTOK, n), F32),
        scratch_shapes=[pltpu.VMEM((NMM_TM, D), BF16)],
        compiler_params=_params("arbitrary", "arbitrary"),
        name=name,
    )(*args)


MMR_TM = 512


def _mmr_kernel(*refs, n_a, has_bias, final):
    a_refs, w_refs = refs[:n_a], refs[n_a:2 * n_a]
    rest = list(refs[2 * n_a:])
    b_ref = rest.pop(0) if has_bias else None
    x_ref, gate_ref = rest.pop(0), rest.pop(0)
    fn_ref = rest.pop(0) if final else None
    o_ref = rest.pop(0)
    acc = _dot(a_refs[0][...].astype(BF16), w_refs[0][...])
    for a_ref, w_ref in zip(a_refs[1:], w_refs[1:]):
        acc = acc + _dot(a_ref[...].astype(BF16), w_ref[...])
    if has_bias:
        acc = acc + b_ref[...]
    y = x_ref[...] + gate_ref[...] * acc
    if final:
        y = y * lax.rsqrt(jnp.mean(y * y, axis=-1, keepdims=True) + EPS) * fn_ref[...]
    o_ref[...] = y


def _mmr(a_list, w_list, bias, x, gate, final_gain, name):
    grp = lambda i: (i * MMR_TM // ROWBLK, 0, 0)
    in_specs, args = [], []
    for a in a_list:
        in_specs.append(pl.BlockSpec((MMR_TM, a.shape[1]), lambda i: (i, 0)))
        args.append(a)
    for w in w_list:
        in_specs.append(pl.BlockSpec(w.shape, lambda i: (0, 0)))
        args.append(w)
    if bias is not None:
        in_specs.append(pl.BlockSpec((1, D), lambda i: (0, 0)))
        args.append(bias.reshape(1, D))
    in_specs += [pl.BlockSpec((MMR_TM, D), lambda i: (i, 0)), pl.BlockSpec((None, 1, D), grp)]
    args += [x, gate]
    if final_gain is not None:
        in_specs.append(pl.BlockSpec((1, D), lambda i: (0, 0)))
        args.append(final_gain.reshape(1, D))
    return pl.pallas_call(
        functools.partial(_mmr_kernel, n_a=len(a_list), has_bias=bias is not None,
                          final=final_gain is not None),
        grid=(NTOK // MMR_TM,),
        in_specs=in_specs,
        out_specs=pl.BlockSpec((MMR_TM, D), lambda i: (i, 0)),
        out_shape=jax.ShapeDtypeStruct((NTOK, D), F32),
        compiler_params=_params("arbitrary"),
        name=name,
    )(*args)


def _seq_shifts(x, i):
    lseq = jnp.where(i == 0, LCTX, LLAT)
    pos = lax.broadcasted_iota(jnp.int32, x.shape, 0) & (lseq - 1)
    prev = jnp.where(pos == 0, 0.0, pltpu.roll(x, 1, 0))
    nxt = jnp.where(pos == lseq - 1, 0.0, pltpu.roll(x, ROWBLK - 1, 0))
    return prev, nxt


def _conv3(x, w, i):
    prev, nxt = _seq_shifts(x, i)
    return prev * w[0:1] + x * w[1:2] + nxt * w[2:3]


def _ab_conv_kernel(x_ref, w_ref, o_ref):
    i, j = pl.program_id(0), pl.program_id(1)
    y = _silu(_conv3(x_ref[...], w_ref[...], i))
    nrm = y * lax.rsqrt(jnp.sum(y * y, axis=-1, keepdims=True) + EPS)
    nrm = nrm * jnp.where(j < CB_KA, DK ** -0.5, 1.0)
    o_ref[...] = jnp.where(j < CB_VA, nrm, y)


def _ab_conv(proj, conv_w):
    ncb = 3 * HEADS
    return pl.pallas_call(
        _ab_conv_kernel,
        grid=(NGROUP, ncb),
        in_specs=[pl.BlockSpec((ROWBLK, LANE), lambda i, j: (i, j)),
                  pl.BlockSpec((3, LANE), lambda i, j: (0, j))],
        out_specs=pl.BlockSpec((ROWBLK, LANE), lambda i, j: (i, j)),
        out_shape=jax.ShapeDtypeStruct((NTOK, ncb * LANE), F32),
        compiler_params=_params("arbitrary", "arbitrary"),
        name="ab_conv",
    )(proj, conv_w)


def _gates_kernel(x_ref, alog_ref, dtb_ref, o_ref):
    x = x_ref[...]
    lane = lax.broadcasted_iota(jnp.int32, x.shape, 1)
    row = lax.broadcasted_iota(jnp.int32, x.shape, 0) & (CHUNK - 1)
    t = x + dtb_ref[...]
    softplus = jnp.maximum(t, 0.0) + jnp.log(1.0 + jnp.exp(-jnp.abs(t)))
    g = -jnp.exp(alog_ref[...]) * softplus
    pre, suf = g, g
    s = 1
    while s < CHUNK:
        pre = pre + jnp.where(row >= s, pltpu.roll(pre, s, 0), 0.0)
        suf = suf + jnp.where(row < CHUNK - s, pltpu.roll(suf, ROWBLK - s, 0), 0.0)
        s *= 2
    gc = jnp.where(lane < HEADS, pre, suf)
    o_ref[...] = jnp.where(lane < 2 * HEADS, gc, jax.nn.sigmoid(x))


def _gates(proj, a_log, dt_bias):
    pad = lambda v: jnp.pad(v.reshape(1, 2 * HEADS), ((0, 0), (0, LANE - 2 * HEADS)))
    return pl.pallas_call(
        _gates_kernel,
        grid=(NGROUP,),
        in_specs=[pl.BlockSpec((ROWBLK, LANE), lambda i: (i, CB_GATE)),
                  pl.BlockSpec((1, LANE), lambda i: (0, 0)),
                  pl.BlockSpec((1, LANE), lambda i: (0, 0))],
        out_specs=pl.BlockSpec((ROWBLK, LANE), lambda i: (i, 0)),
        out_shape=jax.ShapeDtypeStruct((NTOK, LANE), F32),
        compiler_params=_params("arbitrary"),
        name="ab_gates",
    )(proj, pad(a_log), pad(dt_bias))


DP_RB = 512
DP_NC = DP_RB // CHUNK
TRI_BASE = 8


def _split(x):
    hi = x.astype(BF16)
    return hi, (x - hi.astype(F32)).astype(BF16)


def _dot3(a, b):
    return _dot(a[0], b[0]) + _dot(a[0], b[1]) + _dot(a[1], b[0])


def _unit_tri_inv_batch(lms):
    ii = lax.broadcasted_iota(jnp.int32, (CHUNK, CHUNK), 0)
    jj = lax.broadcasted_iota(jnp.int32, (CHUNK, CHUNK), 1)
    same = lambda b: (ii >> int(math.log2(b))) == (jj >> int(math.log2(b)))
    eye = jnp.where(ii == jj, 1.0, 0.0)
    qs = [jnp.where(same(TRI_BASE), -lm, 0.0) for lm in lms]
    ps = [eye + q for q in qs]
    qs = [_split(q) for q in qs]
    for _ in range(int(math.log2(TRI_BASE)) - 1):
        qs = [_split(_dot3(q, q)) for q in qs]
        ps = [p + _dot3(_split(p), q) for p, q in zip(ps, qs)]
    b = TRI_BASE
    while b < CHUNK:
        off = jnp.logical_and(same(2 * b), jnp.logical_not(same(b)))
        pss = [_split(p) for p in ps]
        ts = [_dot3(p, _split(jnp.where(off, lm, 0.0))) for p, lm in zip(pss, lms)]
        ps = [p - _dot3(_split(t), p2) for p, p2, t in zip(ps, pss, ts)]
        b *= 2
    return ps


def _delta_prep_kernel(q_ref, k_ref, v_ref, g_ref, gt_ref, u_ref, wq_ref, ak_ref, gl_ref):
    h = pl.program_id(1)
    ii = lax.broadcasted_iota(jnp.int32, (CHUNK, CHUNK), 0)
    jj = lax.broadcasted_iota(jnp.int32, (CHUNK, CHUNK), 1)
    probs = []
    for c in range(DP_NC):
        rows = slice(c * CHUNK, (c + 1) * CHUNK)
        q, k, v, gts = q_ref[rows, :], k_ref[rows, :], v_ref[rows, :], g_ref[rows, :]
        kbf = k.astype(BF16)
        qk = _dot_nt(q.astype(BF16), kbf)
        for d in range(2):
            incl = (ii >= jj) if d == 0 else (ii <= jj)
            strict = (ii > jj) if d == 0 else (ii < jj)
            gcol = _pick_lane(gts, d * HEADS + h)
            bcol = _pick_lane(gts, 2 * HEADS + d * HEADS + h)
            grow = gt_ref[d * HEADS + h][c:c + 1, :]
            dmask = jnp.where(incl, jnp.exp(jnp.where(incl, gcol - grow, 0.0)), 0.0)
            kb = k * bcol
            lm = jnp.where(strict, _dot_nt(kb.astype(BF16), kbf) * dmask, 0.0)
            probs.append((c, d, rows, q, k, v, kb, gcol, bcol, dmask, qk, incl, lm))
    tmats = _unit_tri_inv_batch([p[-1] for p in probs])
    for (c, d, rows, q, k, v, kb, gcol, bcol, dmask, qk, incl, _), tmat in zip(probs, tmats):
        gam = jnp.exp(gcol)
        rhs = jnp.concatenate([v * bcol, kb * gam], axis=1).astype(BF16)
        uw = _dot(tmat.astype(BF16), rhs)
        attn = jnp.where(incl, qk * dmask, 0.0)
        gtot = gcol[CHUNK - 1:CHUNK, :] if d == 0 else gcol[0:1, :]
        kd = k * jnp.exp(gtot - gcol)
        u_ref[d, rows, :] = uw[:, :DK]
        wq_ref[d, c] = jnp.concatenate([uw[:, DK:], q * gam], axis=0).astype(BF16)
        ak_ref[d, c] = jnp.concatenate([attn, kd.T], axis=0).astype(BF16)
        gl_ref[d, c] = jnp.broadcast_to(jnp.exp(gtot), (1, LANE))


def _delta_prep(qkv, gates, gates_t):
    nrb = NTOK // DP_RB
    blk = lambda cb: pl.BlockSpec((DP_RB, LANE), lambda i, h: (i, cb + h))
    return pl.pallas_call(
        _delta_prep_kernel,
        grid=(nrb, HEADS),
        in_specs=[blk(CB_QA), blk(CB_KA), blk(CB_VA),
                  pl.BlockSpec((DP_RB, LANE), lambda i, h: (i, 0)),
                  pl.BlockSpec((None, 2 * HEADS, DP_NC, CHUNK), lambda i, h: (i, 0, 0, 0))],
        out_specs=[pl.BlockSpec((2, DP_RB, LANE), lambda i, h: (0, i, h)),
                   pl.BlockSpec((2, None, DP_NC, 2 * CHUNK, DK), lambda i, h: (0, h, i, 0, 0)),
                   pl.BlockSpec((2, None, DP_NC, CHUNK + DK, CHUNK), lambda i, h: (0, h, i, 0, 0)),
                   pl.BlockSpec((2, None, DP_NC, 1, LANE), lambda i, h: (0, h, i, 0, 0))],
        out_shape=[jax.ShapeDtypeStruct((2, NTOK, HW), F32),
                   jax.ShapeDtypeStruct((2, HEADS, NCH, 2 * CHUNK, DK), BF16),
                   jax.ShapeDtypeStruct((2, HEADS, NCH, CHUNK + DK, CHUNK), BF16),
                   jax.ShapeDtypeStruct((2, HEADS, NCH, 1, LANE), F32)],
        compiler_params=_params("arbitrary", "arbitrary"),
        name="delta_prep",
    )(qkv, qkv, qkv, gates, gates_t)


def _delta_scan_kernel(*refs, L, zero_init, aliased):
    refs = list(refs)
    u_ref, wq_ref, ak_ref, gl_ref, z_ref, ng_ref = [refs.pop(0) for _ in range(6)]
    s0_ref = None if zero_init else refs.pop(0)
    if aliased:
        refs.pop(0)
    o_ref, sfin_ref, o_scr = refs
    n = L // CHUNK

    def body(c, carry):
        new = []
        for d in range(2):
            s = carry[d]
            cc = c if d == 0 else n - 1 - c
            r0 = pl.multiple_of(cc * CHUNK, CHUNK)
            r1 = _dot(wq_ref[d, cc], s.astype(BF16))
            vnew = u_ref[d, pl.ds(r0, CHUNK), :] - r1[:CHUNK]
            r2 = _dot(ak_ref[d, cc], vnew.astype(BF16))
            o_scr[d, pl.ds(r0, CHUNK), :] = r1[CHUNK:] + r2[:CHUNK]
            new.append(s * gl_ref[d, cc] + r2[CHUNK:])
        return tuple(new)

    if zero_init:
        init = (jnp.zeros((DK, DK), F32), jnp.zeros((DK, DK), F32))
    else:
        init = (s0_ref[0], s0_ref[1])
    fin = lax.fori_loop(0, n, body, init)
    sfin_ref[0] = fin[0]
    sfin_ref[1] = fin[1]
    o = o_scr[0] + o_scr[1]
    o = o * lax.rsqrt(jnp.mean(o * o, axis=-1, keepdims=True) + EPS) * ng_ref[...]
    o_ref[...] = o * _silu(z_ref[...])


def _delta_scan(stream, u, wq, ak, gl, proj, norm_gain, s0, prev_out):
    nseq, L = stream["nseq"], stream["L"]
    rb0 = stream["row0"] // L
    n = L // CHUNK
    in_specs = [pl.BlockSpec((2, L, LANE), lambda s, h: (0, rb0 + s, h)),
                pl.BlockSpec((2, None, n, 2 * CHUNK, DK), lambda s, h: (0, h, rb0 + s, 0, 0)),
                pl.BlockSpec((2, None, n, CHUNK + DK, CHUNK), lambda s, h: (0, h, rb0 + s, 0, 0)),
                pl.BlockSpec((2, None, n, 1, LANE), lambda s, h: (0, h, rb0 + s, 0, 0)),
                pl.BlockSpec((L, LANE), lambda s, h: (rb0 + s, CB_ZA + h)),
                pl.BlockSpec((1, LANE), lambda s, h: (0, 0))]
    args = [u, wq, ak, gl, proj, norm_gain.reshape(1, DK)]
    if s0 is not None:
        in_specs.append(pl.BlockSpec((None, 2, None, DK, DK), lambda s, h: (s, 0, h, 0, 0)))
        args.append(s0)
    aliases = {}
    if prev_out is not None:
        in_specs.append(pl.BlockSpec(memory_space=pl.ANY))
        aliases = {len(args): 0}
        args.append(prev_out)
    return pl.pallas_call(
        functools.partial(_delta_scan_kernel, L=L, zero_init=s0 is None, aliased=prev_out is not None),
        grid=(nseq, HEADS),
        in_specs=in_specs,
        out_specs=[pl.BlockSpec((L, LANE), lambda s, h: (rb0 + s, h)),
                   pl.BlockSpec((None, 2, None, DK, DK), lambda s, h: (s, 0, h, 0, 0))],
        out_shape=[jax.ShapeDtypeStruct((NTOK, HW), F32),
                   jax.ShapeDtypeStruct((nseq, 2, HEADS, DK, DK), F32)],
        scratch_shapes=[pltpu.VMEM((2, L, LANE), F32)],
        input_output_aliases=aliases,
        compiler_params=_params("arbitrary", "arbitrary"),
        name="delta_scan_%d" % L,
    )(*args)


def _ret_scan_kernel(*refs, L, zero_init, aliased):
    refs = list(refs)
    q_ref, k_ref, v_ref, gb_ref, dec_ref, ng_ref = [refs.pop(0) for _ in range(6)]
    r0_ref = None if zero_init else refs.pop(0)
    if aliased:
        refs.pop(0)
    o_ref, rfin_ref, o_scr = refs
    h = pl.program_id(1)
    n = L // CHUNK
    ii = lax.broadcasted_iota(jnp.int32, (CHUNK, CHUNK), 0)
    jj = lax.broadcasted_iota(jnp.int32, (CHUNK, CHUNK), 1)
    ci = lax.broadcasted_iota(jnp.int32, (CHUNK, 1), 0).astype(F32)
    consts = []
    for d in range(2):
        lg = -jnp.exp(_pick_lane(dec_ref[...], d * HEADS + h))
        dist = (ii - jj) if d == 0 else (jj - ii)
        dmat = jnp.where(dist >= 0, jnp.exp(jnp.maximum(dist, 0).astype(F32) * lg), 0.0)
        cross = jnp.exp(((ci + 1.0) if d == 0 else (CHUNK - ci)) * lg)
        sdec = jnp.exp(((CHUNK - 1.0 - ci) if d == 0 else ci) * lg)
        consts.append((dmat, cross, sdec, jnp.exp(CHUNK * lg)))

    def body(c, carry):
        new = []
        for d in range(2):
            dmat, cross, sdec, cdec = consts[d]
            r = carry[d]
            cc = c if d == 0 else n - 1 - c
            rows = pl.ds(pl.multiple_of(cc * CHUNK, CHUNK), CHUNK)
            q = q_ref[rows, :] * DK ** -0.5
            k, v = k_ref[rows, :], v_ref[rows, :]
            vb = v.astype(BF16)
            inner = _dot_nt(q.astype(BF16), k.astype(BF16)) * dmat
            o_scr[d, rows, :] = _dot(inner.astype(BF16), vb) + _dot((q * cross).astype(BF16), r.astype(BF16))
            new.append(r * cdec + _dot_tn((k * sdec).astype(BF16), vb))
        return tuple(new)

    if zero_init:
        init = (jnp.zeros((DK, DK), F32), jnp.zeros((DK, DK), F32))
    else:
        init = (r0_ref[0], r0_ref[1])
    fin = lax.fori_loop(0, n, body, init)
    rfin_ref[0] = fin[0]
    rfin_ref[1] = fin[1]
    o = o_scr[0] + o_scr[1]
    mu = jnp.mean(o, axis=-1, keepdims=True)
    var = jnp.mean(jnp.square(o - mu), axis=-1, keepdims=True)
    o = (o - mu) * lax.rsqrt(var + EPS) * ng_ref[...]
    o_ref[...] = o * _silu(gb_ref[...])


def _ret_scan(stream, proj, ret_decay, norm_gain, r0, prev_out):
    nseq, L = stream["nseq"], stream["L"]
    rb0 = stream["row0"] // L
    blk = lambda cb: pl.BlockSpec((L, LANE), lambda s, h: (rb0 + s, cb + h))
    in_specs = [blk(CB_QB), blk(CB_KB), blk(CB_VB), blk(CB_GB),
                pl.BlockSpec((1, LANE), lambda s, h: (0, 0)),
                pl.BlockSpec((1, LANE), lambda s, h: (0, 0))]
    dec = jnp.pad(ret_decay.reshape(1, 2 * HEADS), ((0, 0), (0, LANE - 2 * HEADS)))
    args = [proj, proj, proj, proj, dec, norm_gain.reshape(1, DK)]
    if r0 is not None:
        in_specs.append(pl.BlockSpec((None, 2, None, DK, DK), lambda s, h: (s, 0, h, 0, 0)))
        args.append(r0)
    aliases = {}
    if prev_out is not None:
        in_specs.append(pl.BlockSpec(memory_space=pl.ANY))
        aliases = {len(args): 0}
        args.append(prev_out)
    return pl.pallas_call(
        functools.partial(_ret_scan_kernel, L=L, zero_init=r0 is None, aliased=prev_out is not None),
        grid=(nseq, HEADS),
        in_specs=in_specs,
        out_specs=[pl.BlockSpec((L, LANE), lambda s, h: (rb0 + s, h)),
                   pl.BlockSpec((None, 2, None, DK, DK), lambda s, h: (s, 0, h, 0, 0))],
        out_shape=[jax.ShapeDtypeStruct((NTOK, HW), F32),
                   jax.ShapeDtypeStruct((nseq, 2, HEADS, DK, DK), F32)],
        scratch_shapes=[pltpu.VMEM((2, L, LANE), F32)],
        input_output_aliases=aliases,
        compiler_params=_params("arbitrary", "arbitrary"),
        name="ret_scan_%d" % L,
    )(*args)


SCAN_G = LCTX // CHUNK
SCAN_RB = SCAN_G * CHUNK
SCAN_CTX_STEPS = TOK_CTX // SCAN_RB
SCAN_LAT_STEPS = LLAT // SCAN_RB
assert SCAN_RB == LCTX and SCAN_CTX_STEPS % SCAN_LAT_STEPS == 0


def _scan_bwd_group(t):
    i = t % SCAN_LAT_STEPS
    return jnp.where(t < SCAN_CTX_STEPS, t, t - i + (SCAN_LAT_STEPS - 1 - i))


def _scan_seq(t):
    return jnp.where(t < SCAN_CTX_STEPS, t, NCTX + (t - SCAN_CTX_STEPS) // SCAN_LAT_STEPS)


def _ab_scan_kernel(uf_ref, ub_ref, wqf_ref, wqb_ref, akf_ref, akb_ref, glf_ref, glb_ref,
                    qf_ref, kf_ref, vf_ref, qb_ref, kb_ref, vb_ref, dec_ref, s0_ref, r0_ref,
                    odf_ref, odb_ref, orf_ref, orb_ref, sfin_ref, rfin_ref, s_scr, r_scr):
    t = pl.program_id(0)
    in_ctx = t < SCAN_CTX_STEPS
    first = jnp.logical_or(in_ctx, t % SCAN_LAT_STEPS == 0)
    last = jnp.logical_or(in_ctx, t % SCAN_LAT_STEPS == SCAN_LAT_STEPS - 1)

    @pl.when(first)
    def _():
        s_scr[...] = jnp.where(in_ctx, 0.0, s0_ref[...])
        r_scr[...] = jnp.where(in_ctx, 0.0, r0_ref[...])

    dirs = ((uf_ref, wqf_ref, akf_ref, glf_ref, qf_ref, kf_ref, vf_ref, odf_ref, orf_ref),
            (ub_ref, wqb_ref, akb_ref, glb_ref, qb_ref, kb_ref, vb_ref, odb_ref, orb_ref))

    ii = lax.broadcasted_iota(jnp.int32, (SCAN_RB, SCAN_RB), 0)
    jj = lax.broadcasted_iota(jnp.int32, (SCAN_RB, SCAN_RB), 1)
    ci = lax.broadcasted_iota(jnp.int32, (SCAN_RB, 1), 0).astype(F32)
    for d in range(2):
        q_ref, k_ref, v_ref, or_ref = dirs[d][4], dirs[d][5], dirs[d][6], dirs[d][8]
        dist = (ii - jj) if d == 0 else (jj - ii)
        for h in range(HEADS):
            cols = slice(h * DK, (h + 1) * DK)
            lane = d * HEADS + h
            lg = -jnp.exp(dec_ref[:, lane:lane + 1])
            dmat = jnp.where(dist >= 0, jnp.exp(jnp.maximum(dist, 0).astype(F32) * lg), 0.0)
            cross = jnp.exp(((ci + 1.0) if d == 0 else (SCAN_RB - ci)) * lg)
            sdec = jnp.exp(((SCAN_RB - 1.0 - ci) if d == 0 else ci) * lg)
            r = r_scr[d, h]
            q = q_ref[:, cols] * DK ** -0.5
            k = k_ref[:, cols]
            vb = v_ref[:, cols].astype(BF16)
            inner = _dot_nt(q.astype(BF16), k.astype(BF16)) * dmat
            or_ref[:, cols] = _dot(inner.astype(BF16), vb) + _dot((q * cross).astype(BF16), r.astype(BF16))
            r_scr[d, h] = r * jnp.exp(SCAN_RB * lg) + _dot_tn((k * sdec).astype(BF16), vb)

    s_cur = {(d, h): s_scr[d, h] for d in range(2) for h in range(HEADS)}
    for c in range(SCAN_G):
        for d in range(2):
            u_ref, wq_ref, ak_ref, gl_ref, q_ref, k_ref, v_ref, od_ref, or_ref = dirs[d]
            cc = c if d == 0 else SCAN_G - 1 - c
            rows = slice(cc * CHUNK, (cc + 1) * CHUNK)
            for h in range(HEADS):
                cols = slice(h * DK, (h + 1) * DK)
                s = s_cur[d, h]
                r1 = _dot(wq_ref[h, cc], s.astype(BF16))
                vnew = u_ref[rows, cols] - r1[:CHUNK]
                r2 = _dot(ak_ref[h, cc], vnew.astype(BF16))
                od_ref[rows, cols] = r1[CHUNK:] + r2[:CHUNK]
                s_cur[d, h] = s * gl_ref[h, cc] + r2[CHUNK:]
    for d in range(2):
        for h in range(HEADS):
            s_scr[d, h] = s_cur[d, h]

    @pl.when(last)
    def _():
        sfin_ref[...] = s_scr[...]
        rfin_ref[...] = r_scr[...]


def _ab_scan(u, wq, ak, gl, proj, ret_decay, s0, r0):
    nseq = NCTX + NLAT
    fwd = lambda t: t
    row = lambda g, cb: pl.BlockSpec((SCAN_RB, HW), lambda t: (g(t), cb))
    u_spec = lambda d, g: pl.BlockSpec((None, SCAN_RB, HW), lambda t: (d, g(t), 0))
    op_spec = lambda d, g, a, b: pl.BlockSpec((None, HEADS, SCAN_G, a, b), lambda t: (d, 0, g(t), 0, 0))
    state = pl.BlockSpec((None, 2, HEADS, DK, DK), lambda t: (_scan_seq(t), 0, 0, 0, 0))
    state0 = pl.BlockSpec((None, 2, HEADS, DK, DK), lambda t: (jnp.maximum(_scan_seq(t) - NCTX, 0), 0, 0, 0, 0))
    in_specs = [u_spec(0, fwd), u_spec(1, _scan_bwd_group),
                op_spec(0, fwd, 2 * CHUNK, DK), op_spec(1, _scan_bwd_group, 2 * CHUNK, DK),
                op_spec(0, fwd, CHUNK + DK, CHUNK), op_spec(1, _scan_bwd_group, CHUNK + DK, CHUNK),
                op_spec(0, fwd, 1, LANE), op_spec(1, _scan_bwd_group, 1, LANE),
                row(fwd, CB_QB // HEADS), row(fwd, CB_KB // HEADS), row(fwd, CB_VB // HEADS),
                row(_scan_bwd_group, CB_QB // HEADS), row(_scan_bwd_group, CB_KB // HEADS),
                row(_scan_bwd_group, CB_VB // HEADS),
                pl.BlockSpec((1, LANE), lambda t: (0, 0)), state0, state0]
    dec = jnp.pad(ret_decay.reshape(1, 2 * HEADS), ((0, 0), (0, LANE - 2 * HEADS)))
    o_sds = jax.ShapeDtypeStruct((NTOK, HW), F32)
    st_sds = jax.ShapeDtypeStruct((nseq, 2, HEADS, DK, DK), F32)
    return pl.pallas_call(
        _ab_scan_kernel,
        grid=(NTOK // SCAN_RB,),
        in_specs=in_specs,
        out_specs=[row(fwd, 0), row(_scan_bwd_group, 0), row(fwd, 0), row(_scan_bwd_group, 0), state, state],
        out_shape=[o_sds, o_sds, o_sds, o_sds, st_sds, st_sds],
        scratch_shapes=[pltpu.VMEM((2, HEADS, DK, DK), F32), pltpu.VMEM((2, HEADS, DK, DK), F32)],
        compiler_params=_params("arbitrary"),
        name="ab_scan",
    )(u, u, wq, wq, ak, ak, gl, gl, proj, proj, proj, proj, proj, proj, dec, s0, r0)


def _ab_out_kernel(odf_ref, odb_ref, orf_ref, orb_ref, z_ref, gb_ref, na_ref, nb_ref, w_ref, x_ref, gate_ref,
                   o_ref):
    oa = odf_ref[...] + odb_ref[...]
    ob = orf_ref[...] + orb_ref[...]
    z, gb = z_ref[...], gb_ref[...]
    parts = []
    for h in range(HEADS):
        cols = slice(h * DK, (h + 1) * DK)
        a = oa[:, cols]
        a = a * lax.rsqrt(jnp.mean(a * a, axis=-1, keepdims=True) + EPS) * na_ref[...]
        parts.append(a * _silu(z[:, cols]))
    for h in range(HEADS):
        cols = slice(h * DK, (h + 1) * DK)
        b = ob[:, cols]
        mu = jnp.mean(b, axis=-1, keepdims=True)
        var = jnp.mean(jnp.square(b - mu), axis=-1, keepdims=True)
        parts.append((b - mu) * lax.rsqrt(var + EPS) * nb_ref[...] * _silu(gb[:, cols]))
    a = jnp.concatenate(parts, axis=1).astype(BF16)
    o_ref[...] = x_ref[...] + gate_ref[...] * _dot(a, w_ref[...])


def _ab_out(odf, odb, orf, orb, proj, norm_a, norm_b, w_out, x, gate):
    grp = lambda i: (i * MMR_TM // ROWBLK, 0, 0)
    tile = lambda cb: pl.BlockSpec((MMR_TM, HW), lambda i: (i, cb))
    vec = pl.BlockSpec((1, DK), lambda i: (0, 0))
    return pl.pallas_call(
        _ab_out_kernel,
        grid=(NTOK // MMR_TM,),
        in_specs=[tile(0), tile(0), tile(0), tile(0), tile(CB_ZA // HEADS), tile(CB_GB // HEADS), vec, vec,
                  pl.BlockSpec((2 * HW, D), lambda i: (0, 0)),
                  pl.BlockSpec((MMR_TM, D), lambda i: (i, 0)), pl.BlockSpec((None, 1, D), grp)],
        out_specs=pl.BlockSpec((MMR_TM, D), lambda i: (i, 0)),
        out_shape=jax.ShapeDtypeStruct((NTOK, D), F32),
        compiler_params=_params("arbitrary"),
        name="ab_out",
    )(odf, odb, orf, orb, proj, proj, norm_a.reshape(1, DK), norm_b.reshape(1, DK), w_out, x, gate)


def _hy_conv_kernel(x0_ref, x1_ref, v_ref, w0_ref, w1_ref, wv_ref, b0_ref, b1_ref, bv_ref, x0o_ref, p_ref):
    i = pl.program_id(0)
    x0o_ref[...] = _conv3(x0_ref[...], w0_ref[...], i) + b0_ref[...]
    x1 = _conv3(x1_ref[...], w1_ref[...], i) + b1_ref[...]
    v = _conv3(v_ref[...], wv_ref[...], i) + bv_ref[...]
    p_ref[...] = v * x1


def _hy_conv(u3, conv_w, conv_b):
    ncb = D // LANE
    xb = lambda k: pl.BlockSpec((ROWBLK, LANE), lambda i, j: (i, k * ncb + j))
    wb = lambda k: pl.BlockSpec((3, LANE), lambda i, j: (0, k * ncb + j))
    bb = lambda k: pl.BlockSpec((1, LANE), lambda i, j: (0, k * ncb + j))
    cb = conv_b.reshape(1, 3 * D)
    return pl.pallas_call(
        _hy_conv_kernel,
        grid=(NGROUP, ncb),
        in_specs=[xb(0), xb(1), xb(2), wb(0), wb(1), wb(2), bb(0), bb(1), bb(2)],
        out_specs=[pl.BlockSpec((ROWBLK, LANE), lambda i, j: (i, j))] * 2,
        out_shape=[jax.ShapeDtypeStruct((NTOK, D), F32)] * 2,
        compiler_params=_params("arbitrary", "arbitrary"),
        name="hy_conv",
    )(u3, u3, u3, conv_w, conv_w, conv_w, cb, cb, cb)


def _filter_features(L):
    r = np.arange(2 * L)
    pos = np.where(r < L, r, 2 * L - r) % L
    t = pos / (L - 1.0)
    bands = np.linspace(1e-4, HY_BANDS - 1, HY_BANDS)
    ang = 2.0 * np.pi * np.outer(pos, bands) / L
    z = np.zeros((2 * L, HY_FW), np.float64)
    z[:, 0] = t
    z[:, 1:1 + HY_BANDS] = np.cos(ang)
    z[:, 1 + HY_BANDS:HY_EMB] = -np.sin(ang)
    z[:, HY_EMB] = (r != L)
    return z.astype(np.float32)


def _filter_kernel(z_ref, w1_ref, b1_ref, f1_ref, w2_ref, b2_ref, f2_ref, w3_ref, dl_ref, o_ref):
    z = z_ref[...]
    hid = jnp.sin(f1_ref[...] * (_dot(z, w1_ref[...], HI) + b1_ref[...]))
    hid = jnp.sin(f2_ref[...] * (_dot(hid, w2_ref[...], HI) + b2_ref[...]))
    filt = _dot(hid, w3_ref[...], HI)
    window = jnp.exp(-z[:, 0:1] * dl_ref[...]) * z[:, HY_EMB:HY_EMB + 1]
    o_ref[...] = filt * window


def _hy_filter(L, w1, b1, f1, w2, b2, f2, w3):
    rb = min(512, L)
    nblk = 2 * L // rb
    z = jnp.asarray(_filter_features(L))
    w1p = jnp.pad(w1, ((0, HY_FW - HY_EMB), (0, 0)))
    min_decay = math.log(HY_TARGET) / HY_SLOW_PCT
    max_decay = math.log(HY_TARGET) / HY_FAST_PCT
    deltas = jnp.asarray(np.abs(np.linspace(min_decay, max_decay, D)).astype(np.float32).reshape(1, D))
    vec = lambda v: v.reshape(1, HY_FW)
    full = lambda shp: pl.BlockSpec(shp, lambda i: (0, 0))
    return pl.pallas_call(
        _filter_kernel,
        grid=(nblk,),
        in_specs=[pl.BlockSpec((rb, HY_FW), lambda i: (i, 0)),
                  full((HY_FW, HY_FW)), full((1, HY_FW)), full((1, HY_FW)),
                  full((HY_FW, HY_FW)), full((1, HY_FW)), full((1, HY_FW)),
                  pl.BlockSpec((HY_FW, D), lambda i: (0, i // (nblk // 2))),
                  full((1, D))],
        out_specs=pl.BlockSpec((rb, D), lambda i: (i, 0)),
        out_shape=jax.ShapeDtypeStruct((2 * L, D), F32),
        compiler_params=_params("arbitrary"),
        name="hy_filter_%d" % L,
    )(z, w1p, vec(b1), vec(f1), w2, vec(b2), vec(f2), w3, deltas)


FFT_RT = 8


def _cis(num, den):
    ang = -2.0 * np.pi * (num % den) / den
    return np.cos(ang), np.sin(ang)


def _fft_consts(L):
    n = 2 * L
    n1 = n // FFT_N2
    k1 = np.arange(n1)
    fr, fi = _cis(np.outer(k1, k1), n1)
    half = n1 // 2
    sig = np.block([[fr[:, :half], -fi[:, :half]], [fi[:, :half], fr[:, :half]]])
    ker = np.concatenate([fr, fi], axis=0)
    cr, ci = fr[:, :half].T, -fi[:, :half].T
    inv = np.stack([np.concatenate([cr, -ci], axis=1), np.concatenate([ci, cr], axis=1)]) / n
    k2 = np.arange(FFT_N2)
    gr, gi = _cis(np.outer(k2, k2), FFT_N2)
    f2 = np.block([[gr, -gi], [gi, gr]])
    f2inv = np.block([[gr, gi], [-gi, gr]])
    tr, ti = _cis(np.outer(k1, k2), n)
    f = lambda a: np.asarray(a, np.float32)
    return dict(n1=n1, sig=f(sig), ker=f(ker), inv=f(inv), f2=f(f2), f2inv=f(f2inv),
                twr=f(tr).reshape(n1, FFT_N2, 1), twi=f(ti).reshape(n1, FFT_N2, 1))


def _lin_rows(mat, rows):
    out = []
    for m in range(mat.shape[0]):
        acc = None
        for k in range(mat.shape[1]):
            cf = float(mat[m, k])
            if abs(cf) < 1e-9:
                continue
            term = rows[k] if abs(cf - 1.0) < 1e-9 else (-rows[k] if abs(cf + 1.0) < 1e-9 else cf * rows[k])
            acc = term if acc is None else acc + term
        out.append(jnp.zeros_like(rows[0]) if acc is None else acc)
    return out


def _twiddle_dft(slab_r, slab_i, twr, twi, f2):
    x = jnp.concatenate([slab_r * twr - slab_i * twi, slab_r * twi + slab_i * twr], axis=0)
    y = _dot3(f2, _split(x))
    return y[:FFT_N2], y[FFT_N2:]


def _filter_idft_twiddle(yr, yi, kr, ki, twr, twi, f2inv):
    z = jnp.concatenate([yr * kr - yi * ki, yr * ki + yi * kr], axis=0)
    w = _dot3(f2inv, _split(z))
    wr, wi = w[:FFT_N2], w[FFT_N2:]
    return wr * twr + wi * twi, wi * twr - wr * twi


def _fft1_kernel(m_ref, *refs):
    o_ref = refs[-1]
    m = _split(m_ref[...])
    n1 = o_ref.shape[1]
    for r in range(FFT_RT):
        parts = [x_ref[c:c + 32, r, :] for x_ref in refs[:-1] for c in range(0, x_ref.shape[0], 32)]
        x = jnp.concatenate(parts, axis=0) if len(parts) > 1 else parts[0]
        y = _dot3(m, _split(x))
        o_ref[0, :, r, :] = y[:n1]
        o_ref[1, :, r, :] = y[n1:]


def _fft_stage1(consts, mat, xs, lead_blocks):
    n1 = consts["n1"]
    rows = mat.shape[1] // len(xs)
    specs = [pl.BlockSpec((rows, FFT_RT, D), functools.partial(lambda j, lb: (lb, j, 0), lb=lb))
             for lb in lead_blocks]
    return pl.pallas_call(
        _fft1_kernel,
        grid=(FFT_N2 // FFT_RT,),
        in_specs=[pl.BlockSpec(mat.shape, lambda j: (0, 0))] + specs,
        out_specs=pl.BlockSpec((2, n1, FFT_RT, D), lambda j: (0, 0, j, 0)),
        out_shape=jax.ShapeDtypeStruct((2, n1, FFT_N2, D), F32),
        compiler_params=_params("arbitrary"),
        name="fft1_%d" % len(xs),
    )(jnp.asarray(mat), *xs)


def _fft2_kernel(a_ref, twr_ref, twi_ref, f2_ref, *rest, conv):
    twr, twi = twr_ref[...], twi_ref[...]
    yr, yi = _twiddle_dft(a_ref[0], a_ref[1], twr, twi, _split(f2_ref[...]))
    if conv:
        kf_ref, f2inv_ref, o_ref = rest
        yr, yi = _filter_idft_twiddle(yr, yi, kf_ref[0], kf_ref[1], twr, twi, _split(f2inv_ref[...]))
    else:
        o_ref, = rest
    o_ref[0] = yr
    o_ref[1] = yi


def _fft_stage2(consts, a, kf):
    n1 = consts["n1"]
    slab = pl.BlockSpec((2, None, FFT_N2, D), lambda k: (0, k, 0, 0))
    tw = pl.BlockSpec((None, FFT_N2, 1), lambda k: (k, 0, 0))
    mat = pl.BlockSpec((2 * FFT_N2, 2 * FFT_N2), lambda k: (0, 0))
    in_specs = [slab, tw, tw, mat]
    args = [a, jnp.asarray(consts["twr"]), jnp.asarray(consts["twi"]), jnp.asarray(consts["f2"])]
    if kf is not None:
        in_specs += [slab, mat]
        args += [kf, jnp.asarray(consts["f2inv"])]
    return pl.pallas_call(
        functools.partial(_fft2_kernel, conv=kf is not None),
        grid=(n1,),
        in_specs=in_specs,
        out_specs=slab,
        out_shape=jax.ShapeDtypeStruct(a.shape, F32),
        compiler_params=_params("arbitrary"),
        name="fft2_%s" % ("conv" if kf is not None else "spec"),
    )(*args)


def _fft3_kernel(m_ref, b_ref, p_ref, x0_ref, bias_ref, *rest):
    o_ref = rest[-1]
    m = _split(m_ref[...])
    bias = bias_ref[...]
    for r in range(FFT_RT):
        x = jnp.concatenate([b_ref[0, :, r, :], b_ref[1, :, r, :]], axis=0)
        y = _dot3(m, _split(x))
        o_ref[:, r, :] = (y + p_ref[:, r, :] * bias) * x0_ref[:, r, :]


def _fft_stage3(consts, b, pv, xv, bias, lead_block, prev_out):
    n1 = consts["n1"]
    half = n1 // 2
    seq = pl.BlockSpec((half, FFT_RT, D), lambda j, s: (lead_block + s, j, 0))
    in_specs = [pl.BlockSpec((None, half, 2 * n1), lambda j, s: (s, 0, 0)),
                pl.BlockSpec((2, n1, FFT_RT, D), lambda j, s: (0, 0, j, 0)),
                seq, seq, pl.BlockSpec((1, D), lambda j, s: (0, 0)),
                pl.BlockSpec(memory_space=pl.ANY)]
    return pl.pallas_call(
        _fft3_kernel,
        grid=(FFT_N2 // FFT_RT, 2),
        in_specs=in_specs,
        out_specs=seq,
        out_shape=jax.ShapeDtypeStruct(pv.shape, F32),
        input_output_aliases={5: 0},
        compiler_params=_params("arbitrary", "arbitrary"),
        name="fft3",
    )(jnp.asarray(consts["inv"]), b, pv, xv, bias.reshape(1, D), prev_out)


def _fft_ctx_kernel(p_ref, x0_ref, kern_ref, twr_ref, twi_ref, f2_ref, f2inv_ref, bias_ref, o_ref, kf_ref,
                    *, n1, sig, ker, inv):
    f2 = _split(f2_ref[...])

    @pl.when(pl.program_id(0) == 0)
    def _():
        a = _lin_rows(ker, [kern_ref[r] for r in range(n1)])
        for k in range(n1):
            kf_ref[0, k], kf_ref[1, k] = _twiddle_dft(a[k], a[n1 + k], twr_ref[k], twi_ref[k], f2)

    f2inv = _split(f2inv_ref[...])
    a = _lin_rows(sig, [p_ref[r] for r in range(n1)])
    br, bi = [], []
    for k in range(n1):
        twr, twi = twr_ref[k], twi_ref[k]
        yr, yi = _twiddle_dft(a[k], a[n1 + k], twr, twi, f2)
        wr, wi = _filter_idft_twiddle(yr, yi, kf_ref[0, k], kf_ref[1, k], twr, twi, f2inv)
        br.append(wr)
        bi.append(wi)
    out = _lin_rows(inv, br + bi)
    bias = bias_ref[...]
    for m, y in enumerate(out):
        o_ref[m] = (y + p_ref[m] * bias) * x0_ref[m]


def _fft_ctx(consts, pv, xv, kern, bias):
    n1 = consts["n1"]
    inv = np.concatenate([consts["inv"][0], consts["inv"][1]], axis=0)
    pair = pl.BlockSpec((n1, FFT_N2, D), lambda q: (q, 0, 0))
    full = lambda shp: pl.BlockSpec(shp, lambda q: (0,) * len(shp))
    return pl.pallas_call(
        functools.partial(_fft_ctx_kernel, n1=n1, sig=consts["sig"], ker=consts["ker"], inv=inv),
        grid=(NCTX // 2,),
        in_specs=[pair, pair, full((n1, FFT_N2, D)), full((n1, FFT_N2, 1)), full((n1, FFT_N2, 1)),
                  full((2 * FFT_N2, 2 * FFT_N2)), full((2 * FFT_N2, 2 * FFT_N2)), full((1, D))],
        out_specs=pair,
        out_shape=jax.ShapeDtypeStruct(pv.shape, F32),
        scratch_shapes=[pltpu.VMEM((2, n1, FFT_N2, D), F32)],
        compiler_params=_params("arbitrary"),
        name="fft_ctx",
    )(pv, xv, kern, jnp.asarray(consts["twr"]), jnp.asarray(consts["twi"]), jnp.asarray(consts["f2"]),
      jnp.asarray(consts["f2inv"]), bias.reshape(1, D))


def _hy_long_conv(p, x0, filt_w, bias):
    slabs = (NTOK // FFT_N2, FFT_N2, D)
    pv, xv = p.reshape(slabs), x0.reshape(slabs)
    cc = _fft_consts(LCTX)
    out = _fft_ctx(cc, pv, xv, _hy_filter(LCTX, *filt_w).reshape(cc["n1"], FFT_N2, D), bias)
    lc = _fft_consts(LLAT)
    n1 = lc["n1"]
    kern = _hy_filter(LLAT, *filt_w).reshape(n1, FFT_N2, D)
    kf = _fft_stage2(lc, _fft_stage1(lc, lc["ker"], [kern], [0]), None)
    lat0 = TOK_CTX // FFT_N2 // (n1 // 2)
    a = _fft_stage1(lc, lc["sig"], [pv, pv], [lat0, lat0 + 1])
    b = _fft_stage2(lc, a, kf)
    return _fft_stage3(lc, b, pv, xv, bias, lat0, out).reshape(NTOK, D)


FFN_TM = 1024
FFN_TF = 256
FFN_HALO = 128


FFN_SR = 32


FFN_CH = 256
FFN_NF = D_FF // FFN_TF
FFN_TILES = NTOK // FFN_TM * FFN_NF


def _ffn_tile(t):
    return jnp.minimum(t, FFN_TILES - 1) // FFN_NF, t % FFN_NF


def _ffn_act_kernel(xc_ref, xp_ref, xn_ref, g_ref, sh_ref, sc_ref, wg_ref, wu_ref, cw_ref, cb_ref,
                    o_ref, h_ref, *slots):
    t = pl.program_id(0)
    i, f = _ffn_tile(t)
    ext = FFN_TM + 2 * FFN_HALO
    is_lat = i >= TOK_CTX // FFN_TM

    @pl.when(t == 0)
    def _():
        for scr in slots[4:]:
            scr[...] = jnp.zeros(scr.shape, F32)

    @pl.when(f == 0)
    def _():
        def norm(x):
            y = x * lax.rsqrt(jnp.mean(x * x, axis=-1, keepdims=True) + EPS) * g_ref[...]
            return (y * (1.0 + sc_ref[...]) + sh_ref[...]).astype(BF16)
        seq_pos = (i * FFN_TM - TOK_CTX) & (LLAT - 1)
        keep_prev = jnp.logical_and(is_lat, seq_pos != 0)
        keep_next = jnp.logical_and(is_lat, seq_pos != LLAT - FFN_TM)
        h_ref[0:FFN_HALO, :] = jnp.where(keep_prev, norm(xp_ref[...]), 0.0).astype(BF16)
        h_ref[FFN_HALO:FFN_HALO + FFN_TM, :] = norm(xc_ref[...])
        h_ref[FFN_HALO + FFN_TM:ext, :] = jnp.where(keep_next, norm(xn_ref[...]), 0.0).astype(BF16)

    def project_gate(scr, lo, hi):
        gate_scr, left_scr, right_scr, _ = scr
        gate = _dot(h_ref[lo:hi, :], wg_ref[...])
        period = jnp.where(is_lat, GRID_W, LCTX)
        r = lax.broadcasted_iota(jnp.int32, (hi - lo, 1), 0)
        col = (r + lo + FFN_HALO) & (period - 1)
        gate_scr[lo:hi, :] = gate
        left_scr[lo:hi, :] = jnp.where(col == 0, 0.0, pltpu.roll(gate, 1, 0))
        right_scr[lo:hi, :] = jnp.where(col == period - 1, 0.0, pltpu.roll(gate, hi - lo - 1, 0))

    def project_up(scr, lo, hi):
        scr[3][lo - FFN_HALO:hi - FFN_HALO, :] = _dot(h_ref[lo:hi, :], wu_ref[...])

    def convolve(scr, strips):
        gate_scr, left_scr, right_scr, up_scr = scr
        ip, _ = _ffn_tile(jnp.maximum(t - 1, 0))
        kidx = lax.broadcasted_iota(jnp.int32, (9, 1), 0)
        mid_row = jnp.logical_and(kidx >= 3, kidx < 6)
        w = jnp.where(jnp.logical_or(ip >= TOK_CTX // FFN_TM, mid_row), cw_ref[...], 0.0)
        bias = cb_ref[...]
        for s in strips:
            z = bias
            for k in range(3):
                lo = s * FFN_SR + FFN_HALO + (k - 1) * GRID_W
                z = z + left_scr[lo:lo + FFN_SR, :] * w[3 * k:3 * k + 1] \
                    + gate_scr[lo:lo + FFN_SR, :] * w[3 * k + 1:3 * k + 2] \
                    + right_scr[lo:lo + FFN_SR, :] * w[3 * k + 2:3 * k + 3]
            out = slice(s * FFN_SR, (s + 1) * FFN_SR)
            o_ref[out, :] = ((0.5 * z) * (1.0 + jnp.tanh(0.5 * z)) * up_scr[out, :]).astype(BF16)

    def step(cur, prev):
        edges = [0] + list(range(FFN_HALO, FFN_HALO + FFN_TM + 1, FFN_CH)) + [ext]
        pieces = []
        for lo, hi in zip(edges[:-1], edges[1:]):
            pieces.append(functools.partial(project_gate, cur, lo, hi))
            if FFN_HALO <= lo < FFN_HALO + FFN_TM:
                pieces.append(functools.partial(project_up, cur, lo, hi))
        nst = FFN_TM // FFN_SR
        for c, piece in enumerate(pieces):
            piece()
            convolve(prev, range(c * nst // len(pieces), (c + 1) * nst // len(pieces)))

    pl.when(t % 2 == 0)(lambda: step(slots[:4], slots[4:]))
    pl.when(t % 2 == 1)(lambda: step(slots[4:], slots[:4]))


def _ffn_act(x, gain, shift, scale, w_gu, conv_w, conv_b):
    hb = FFN_TM // FFN_HALO
    nhb = NTOK // FFN_HALO
    ext = FFN_TM + 2 * FFN_HALO
    row = lambda t: _ffn_tile(t)[0]
    col = lambda t: _ffn_tile(t)[1]
    prev = lambda t: _ffn_tile(jnp.maximum(t - 1, 0))
    vec = pl.BlockSpec((None, 1, D), lambda t: (row(t) * FFN_TM // ROWBLK, 0, 0))
    in_specs = [pl.BlockSpec((FFN_TM, D), lambda t: (row(t), 0)),
                pl.BlockSpec((FFN_HALO, D), lambda t: (jnp.maximum(row(t) * hb - 1, 0), 0)),
                pl.BlockSpec((FFN_HALO, D), lambda t: (jnp.minimum((row(t) + 1) * hb, nhb - 1), 0)),
                pl.BlockSpec((1, D), lambda t: (0, 0)), vec, vec,
                pl.BlockSpec((D, FFN_TF), lambda t: (0, col(t))),
                pl.BlockSpec((D, FFN_TF), lambda t: (0, FFN_NF + col(t))),
                pl.BlockSpec((9, FFN_TF), lambda t: (0, prev(t)[1])),
                pl.BlockSpec((1, FFN_TF), lambda t: (0, prev(t)[1]))]
    return pl.pallas_call(
        _ffn_act_kernel,
        grid=(FFN_TILES + 1,),
        in_specs=in_specs,
        out_specs=pl.BlockSpec((FFN_TM, FFN_TF), lambda t: prev(t)),
        out_shape=jax.ShapeDtypeStruct((NTOK, D_FF), BF16),
        scratch_shapes=[pltpu.VMEM((ext, D), BF16)]
        + 2 * (3 * [pltpu.VMEM((ext, FFN_TF), F32)] + [pltpu.VMEM((FFN_TM, FFN_TF), F32)]),
        compiler_params=_params("arbitrary"),
        name="ffn_act",
    )(x, x, x, gain.reshape(1, D), shift, scale, w_gu, w_gu, conv_w.reshape(9, D_FF), conv_b.reshape(1, D_FF))


def _ab_w_in_cols(w):
    gate_c0 = 3 * HW + HW
    return jnp.concatenate([w[:, :gate_c0], w[:, gate_c0 + 4 * HEADS:], w[:, gate_c0:gate_c0 + 4 * HEADS],
                            jnp.zeros((D, LANE - 4 * HEADS), F32)], axis=1).astype(BF16)


def _ab_mixer(proj, conv_w, a_log, dt_bias, ret_decay, s_delta0, s_ret0):
    qkv = _ab_conv(proj, conv_w)
    gates = _gates(proj, a_log, dt_bias)
    gates_t = gates[:, :2 * HEADS].reshape(NTOK // DP_RB, DP_NC, CHUNK, 2 * HEADS).transpose(0, 3, 1, 2)
    u, wq, ak, gl = _delta_prep(qkv, gates, gates_t)
    odf, odb, orf, orb, sfin, rfin = _ab_scan(u, wq, ak, gl, proj, ret_decay, s_delta0, s_ret0)
    return (odf, odb, orf, orb), sfin[:NCTX], rfin[:NCTX]


def kernel(x_prompt, x_sample, state_delta, state_ret, c, c_ctx, mod_w, mod_b, norm1, norm2, ab_w_in, ab_conv, ab_a_log, ab_dt_bias, ab_norm_a, ab_norm_b, ab_ret_decay, ab_w_out, hy_w_in, hy_b_in, hy_conv_w, hy_conv_b, hy_f_w1, hy_f_b1, hy_f_freq1, hy_f_w2, hy_f_b2, hy_f_freq2, hy_f_w3, hy_f_bias, hy_w_out, hy_b_out, ffn_w_gate, ffn_w_up, ffn_conv, ffn_conv_b, ffn_w_down, final_norm):
    x = jnp.concatenate([x_prompt.reshape(TOK_CTX, D), x_sample.reshape(TOK_LAT, D)], axis=0)
    cvec = jnp.concatenate([c_ctx[None], c, jnp.zeros((8 - 1 - NLAT, D), F32)], axis=0)
    mod = _mod_all(cvec, mod_w, mod_b)
    new_delta, new_ret = [], []
    for l in range(DEPTH):
        m = [mod[l, :NGROUP, k * D:(k + 1) * D].reshape(NGROUP, 1, D) for k in range(6)]
        j = l // 2
        if l % 2 == 0:
            proj = _nmm(x, norm1[l], m[0], m[1], _ab_w_in_cols(ab_w_in[j]), None, AB_N // 3, "ab_in")
            heads, sd, sr = _ab_mixer(proj, ab_conv[j], ab_a_log[j], ab_dt_bias[j], ab_ret_decay[j],
                                      state_delta[:, j], state_ret[:, j])
            new_delta.append(sd)
            new_ret.append(sr)
            x = _ab_out(*heads, proj, ab_norm_a[j], ab_norm_b[j], ab_w_out[j].astype(BF16), x, m[2])
        else:
            u3 = _nmm(x, norm1[l], m[0], m[1], hy_w_in[j].astype(BF16), hy_b_in[j], 512, "hy_in")
            x0, p = _hy_conv(u3, hy_conv_w[j], hy_conv_b[j])
            filt_w = (hy_f_w1[j], hy_f_b1[j], hy_f_freq1[j], hy_f_w2[j], hy_f_b2[j], hy_f_freq2[j], hy_f_w3[j])
            y = _hy_long_conv(p, x0, filt_w, hy_f_bias[j])
            x = _mmr([y], [hy_w_out[j].astype(BF16)], hy_b_out[j], x, m[2], None, "hy_out")
        w_gu = jnp.concatenate([ffn_w_gate[l], ffn_w_up[l]], axis=1).astype(BF16)
        act = _ffn_act(x, norm2[l], m[3], m[4], w_gu, ffn_conv[l], ffn_conv_b[l])
        x = _mmr([act], [ffn_w_down[l].astype(BF16)], None, x, m[5],
                 final_norm if l == DEPTH - 1 else None, "ffn_out")
    y_prompt = x[:TOK_CTX].reshape(NCTX, LCTX, D)
    y_sample = x[TOK_CTX:].reshape(NLAT, LLAT, D)
    return (y_prompt, y_sample, jnp.stack(new_delta, axis=1), jnp.stack(new_ret, axis=1))
```

```python
import functools
import math

import numpy as np
import jax
import jax.numpy as jnp
from jax import lax
from jax.experimental import pallas as pl
from jax.experimental.pallas import tpu as pltpu

F32, BF16 = jnp.float32, jnp.bfloat16
HI = lax.Precision.HIGHEST

D = 1024
NCTX, LCTX = 16, 256
NLAT, LLAT = 2, 4096
DEPTH = 4
TOK_CTX = NCTX * LCTX
TOK_LAT = NLAT * LLAT
NTOK = TOK_CTX + TOK_LAT
ROWBLK = 4096
NGROUP = NTOK // ROWBLK
GRID_W = 64
CHUNK = 64
NCH = NTOK // CHUNK
EPS = 1e-6
HEADS = 4
DK = 128
HW = HEADS * DK
LANE = 128
D_FF = 2816
HY_EMB = 33
HY_BANDS = 16
HY_FW = 64
HY_TARGET = 1e-2
HY_FAST_PCT = 0.3
HY_SLOW_PCT = 1.5
FFT_N2 = 128
VMEM_LIMIT = 52 * 1024 * 1024

CB_QA, CB_KA, CB_VA = 0, 4, 8
CB_ZA, CB_QB, CB_KB, CB_VB, CB_GB, CB_GATE = 0, 4, 8, 12, 16, 20
AB_N = 21 * LANE

CTX = dict(nseq=NCTX, L=LCTX, row0=0)
LAT = dict(nseq=NLAT, L=LLAT, row0=TOK_CTX)


def _params(*sem):
    return pltpu.CompilerParams(dimension_semantics=sem, vmem_limit_bytes=VMEM_LIMIT)


def _silu(x):
    return x * jax.nn.sigmoid(x)


def _dot(a, b, precision=None):
    return jnp.dot(a, b, preferred_element_type=F32, precision=precision)


def _dot_nt(a, b):
    return lax.dot_general(a, b, (((1,), (1,)), ((), ())), preferred_element_type=F32)


def _dot_tn(a, b):
    return lax.dot_general(a, b, (((0,), (0,)), ((), ())), preferred_element_type=F32)


def _pick_lane(x, lane):
    li = lax.broadcasted_iota(jnp.int32, x.shape, 1)
    return jnp.sum(jnp.where(li == lane, x, 0.0), axis=-1, keepdims=True)


def _mod_kernel(c_ref, w_ref, b_ref, o_ref):
    s = _silu(c_ref[...])
    o_ref[...] = _dot(s.astype(BF16), w_ref[...].astype(BF16)) + b_ref[...]


def _mod_all(cvec, mod_w, mod_b):
    tn = 1536
    return pl.pallas_call(
        _mod_kernel,
        grid=(DEPTH, 6 * D // tn),
        in_specs=[pl.BlockSpec((8, D), lambda l, j: (0, 0)),
                  pl.BlockSpec((None, D, tn), lambda l, j: (l, 0, j)),
                  pl.BlockSpec((None, 1, tn), lambda l, j: (l, 0, j))],
        out_specs=pl.BlockSpec((None, 8, tn), lambda l, j: (l, 0, j)),
        out_shape=jax.ShapeDtypeStruct((DEPTH, 8, 6 * D), F32),
        compiler_params=_params("arbitrary", "arbitrary"),
        name="mod",
    )(cvec, mod_w, mod_b.reshape(DEPTH, 1, 6 * D))


NMM_TM = 1024


def _nmm_kernel(x_ref, g_ref, sh_ref, sc_ref, w_ref, *rest, has_bias):
    if has_bias:
        b_ref, o_ref, h_ref = rest
    else:
        o_ref, h_ref = rest

    @pl.when(pl.program_id(1) == 0)
    def _():
        x = x_ref[...]
        y = x * lax.rsqrt(jnp.mean(x * x, axis=-1, keepdims=True) + EPS) * g_ref[...]
        h_ref[...] = (y * (1.0 + sc_ref[...]) + sh_ref[...]).astype(BF16)

    acc = _dot(h_ref[...], w_ref[...])
    if has_bias:
        acc = acc + b_ref[...]
    o_ref[...] = acc


def _nmm(x, gain, shift, scale, w, bias, tn, name):
    n = w.shape[1]
    grp = lambda i, j: (i * NMM_TM // ROWBLK, 0, 0)
    in_specs = [pl.BlockSpec((NMM_TM, D), lambda i, j: (i, 0)),
                pl.BlockSpec((1, D), lambda i, j: (0, 0)),
                pl.BlockSpec((None, 1, D), grp),
                pl.BlockSpec((None, 1, D), grp),
                pl.BlockSpec((D, tn), lambda i, j: (0, j))]
    args = [x, gain.reshape(1, D), shift, scale, w]
    if bias is not None:
        in_specs.append(pl.BlockSpec((1, tn), lambda i, j: (0, j)))
        args.append(bias.reshape(1, n))
    return pl.pallas_call(
        functools.partial(_nmm_kernel, has_bias=bias is not None),
        grid=(NTOK // NMM_TM, n // tn),
        in_specs=in_specs,
        out_specs=pl.BlockSpec((NMM_TM, tn), lambda i, j: (i, j)),
        out_shape=jax.ShapeDtypeStruct((NTOK, n), F32),
        scratch_shapes=[pltpu.VMEM((NMM_TM, D), BF16)],
        compiler_params=_params("arbitrary", "arbitrary"),
        name=name,
    )(*args)


MMR_TM = 512


def _mmr_kernel(*refs, n_a, has_bias, final):
    a_refs, w_refs = refs[:n_a], refs[n_a:2 * n_a]
    rest = list(refs[2 * n_a:])
    b_ref = rest.pop(0) if has_bias else None
    x_ref, gate_ref = rest.pop(0), rest.pop(0)
    fn_ref = rest.pop(0) if final else None
    o_ref = rest.pop(0)
    acc = _dot(a_refs[0][...].astype(BF16), w_refs[0][...])
    for a_ref, w_ref in zip(a_refs[1:], w_refs[1:]):
        acc = acc + _dot(a_ref[...].astype(BF16), w_ref[...])
    if has_bias:
        acc = acc + b_ref[...]
    y = x_ref[...] + gate_ref[...] * acc
    if final:
        y = y * lax.rsqrt(jnp.mean(y * y, axis=-1, keepdims=True) + EPS) * fn_ref[...]
    o_ref[...] = y


def _mmr(a_list, w_list, bias, x, gate, final_gain, name):
    grp = lambda i: (i * MMR_TM // ROWBLK, 0, 0)
    in_specs, args = [], []
    for a in a_list:
        in_specs.append(pl.BlockSpec((MMR_TM, a.shape[1]), lambda i: (i, 0)))
        args.append(a)
    for w in w_list:
        in_specs.append(pl.BlockSpec(w.shape, lambda i: (0, 0)))
        args.append(w)
    if bias is not None:
        in_specs.append(pl.BlockSpec((1, D), lambda i: (0, 0)))
        args.append(bias.reshape(1, D))
    in_specs += [pl.BlockSpec((MMR_TM, D), lambda i: (i, 0)), pl.BlockSpec((None, 1, D), grp)]
    args += [x, gate]
    if final_gain is not None:
        in_specs.append(pl.BlockSpec((1, D), lambda i: (0, 0)))
        args.append(final_gain.reshape(1, D))
    return pl.pallas_call(
        functools.partial(_mmr_kernel, n_a=len(a_list), has_bias=bias is not None,
                          final=final_gain is not None),
        grid=(NTOK // MMR_TM,),
        in_specs=in_specs,
        out_specs=pl.BlockSpec((MMR_TM, D), lambda i: (i, 0)),
        out_shape=jax.ShapeDtypeStruct((NTOK, D), F32),
        compiler_params=_params("arbitrary"),
        name=name,
    )(*args)


def _seq_shifts(x, i):
    lseq = jnp.where(i == 0, LCTX, LLAT)
    pos = lax.broadcasted_iota(jnp.int32, x.shape, 0) & (lseq - 1)
    prev = jnp.where(pos == 0, 0.0, pltpu.roll(x, 1, 0))
    nxt = jnp.where(pos == lseq - 1, 0.0, pltpu.roll(x, ROWBLK - 1, 0))
    return prev, nxt


def _conv3(x, w, i):
    prev, nxt = _seq_shifts(x, i)
    return prev * w[0:1] + x * w[1:2] + nxt * w[2:3]


def _ab_conv_kernel(x_ref, w_ref, o_ref):
    i, j = pl.program_id(0), pl.program_id(1)
    y = _silu(_conv3(x_ref[...], w_ref[...], i))
    nrm = y * lax.rsqrt(jnp.sum(y * y, axis=-1, keepdims=True) + EPS)
    nrm = nrm * jnp.where(j < CB_KA, DK ** -0.5, 1.0)
    o_ref[...] = jnp.where(j < CB_VA, nrm, y)


def _ab_conv(proj, conv_w):
    ncb = 3 * HEADS
    return pl.pallas_call(
        _ab_conv_kernel,
        grid=(NGROUP, ncb),
        in_specs=[pl.BlockSpec((ROWBLK, LANE), lambda i, j: (i, j)),
                  pl.BlockSpec((3, LANE), lambda i, j: (0, j))],
        out_specs=pl.BlockSpec((ROWBLK, LANE), lambda i, j: (i, j)),
        out_shape=jax.ShapeDtypeStruct((NTOK, ncb * LANE), F32),
        compiler_params=_params("arbitrary", "arbitrary"),
        name="ab_conv",
    )(proj, conv_w)


PC_TM = 1024
PC_HALO = 16
PC_TN = 256


def _pc_norm_rows(i, xc_ref, xp_ref, xn_ref, g_ref, sh_ref, sc_ref, h_ref):
    def norm(x):
        y = x * lax.rsqrt(jnp.mean(x * x, axis=-1, keepdims=True) + EPS) * g_ref[...]
        return (y * (1.0 + sc_ref[...]) + sh_ref[...]).astype(BF16)
    h_ref[0:PC_HALO, :] = norm(xp_ref[...])
    h_ref[PC_HALO:PC_HALO + PC_TM, :] = norm(xc_ref[...])
    h_ref[PC_HALO + PC_TM:PC_TM + 2 * PC_HALO, :] = norm(xn_ref[...])


def _pc_proj_conv(i, h_ref, w_ref, b_ref, cw_ref):
    ext = PC_TM + 2 * PC_HALO
    u = _dot(h_ref[...], w_ref[...])
    if b_ref is not None:
        u = u + b_ref[...]
    lseq = jnp.where(i < TOK_CTX // PC_TM, LCTX, LLAT)
    pos = (i * PC_TM - PC_HALO + lax.broadcasted_iota(jnp.int32, (ext, 1), 0)) & (lseq - 1)
    prev = jnp.where(pos == 0, 0.0, pltpu.roll(u, 1, 0))
    nxt = jnp.where(pos == lseq - 1, 0.0, pltpu.roll(u, ext - 1, 0))
    cw = cw_ref[...]
    y = prev * cw[0:1] + u * cw[1:2] + nxt * cw[2:3]
    return y[PC_HALO:PC_HALO + PC_TM]


def _pc_specs(n_extra):
    hb = PC_TM // PC_HALO
    nhb = NTOK // PC_HALO
    vec = pl.BlockSpec((None, 1, D), lambda i, j: (i * PC_TM // ROWBLK, 0, 0))
    return [pl.BlockSpec((PC_TM, D), lambda i, j: (i, 0)),
            pl.BlockSpec((PC_HALO, D), lambda i, j: (jnp.maximum(i * hb - 1, 0), 0)),
            pl.BlockSpec((PC_HALO, D), lambda i, j: (jnp.minimum((i + 1) * hb, nhb - 1), 0)),
            pl.BlockSpec((1, D), lambda i, j: (0, 0)), vec, vec]


def _hy_in_kernel(xc_ref, xp_ref, xn_ref, g_ref, sh_ref, sc_ref, w0_ref, w1_ref, wv_ref, b0_ref, b1_ref, bv_ref,
                  c0_ref, c1_ref, cv_ref, cb0_ref, cb1_ref, cbv_ref, x0_ref, p_ref, h_ref):
    i = pl.program_id(0)

    @pl.when(pl.program_id(1) == 0)
    def _():
        _pc_norm_rows(i, xc_ref, xp_ref, xn_ref, g_ref, sh_ref, sc_ref, h_ref)

    x0_ref[...] = _pc_proj_conv(i, h_ref, w0_ref, b0_ref, c0_ref) + cb0_ref[...]
    x1 = _pc_proj_conv(i, h_ref, w1_ref, b1_ref, c1_ref) + cb1_ref[...]
    v = _pc_proj_conv(i, h_ref, wv_ref, bv_ref, cv_ref) + cbv_ref[...]
    p_ref[...] = v * x1


def _hy_in(x, gain, shift, scale, w, b, conv_w, conv_b):
    nj = D // PC_TN
    wsp = lambda k: pl.BlockSpec((D, PC_TN), lambda i, j: (0, k * nj + j))
    row = lambda r, k: pl.BlockSpec((r, PC_TN), lambda i, j: (0, k * nj + j))
    out = pl.BlockSpec((PC_TM, PC_TN), lambda i, j: (i, j))
    b2, cb2 = b.reshape(1, 3 * D), conv_b.reshape(1, 3 * D)
    return pl.pallas_call(
        _hy_in_kernel,
        grid=(NTOK // PC_TM, nj),
        in_specs=_pc_specs(0) + [wsp(0), wsp(1), wsp(2), row(1, 0), row(1, 1), row(1, 2),
                                 row(3, 0), row(3, 1), row(3, 2), row(1, 0), row(1, 1), row(1, 2)],
        out_specs=[out, out],
        out_shape=[jax.ShapeDtypeStruct((NTOK, D), F32)] * 2,
        scratch_shapes=[pltpu.VMEM((PC_TM + 2 * PC_HALO, D), BF16)],
        compiler_params=_params("arbitrary", "arbitrary"),
        name="hy_in",
    )(x, x, x, gain.reshape(1, D), shift, scale, w, w, w, b2, b2, b2, conv_w, conv_w, conv_w, cb2, cb2, cb2)


def _ab_qkv_kernel(xc_ref, xp_ref, xn_ref, g_ref, sh_ref, sc_ref, w_ref, cw_ref, o_ref, h_ref):
    i, j = pl.program_id(0), pl.program_id(1)

    @pl.when(j == 0)
    def _():
        _pc_norm_rows(i, xc_ref, xp_ref, xn_ref, g_ref, sh_ref, sc_ref, h_ref)

    y = _silu(_pc_proj_conv(i, h_ref, w_ref, None, cw_ref))
    is_qk = j < 2 * HW // PC_TN
    qscale = jnp.where(j < HW // PC_TN, DK ** -0.5, 1.0)
    for blk in range(PC_TN // DK):
        cols = slice(blk * DK, (blk + 1) * DK)
        yb = y[:, cols]
        nrm = yb * (lax.rsqrt(jnp.sum(yb * yb, axis=-1, keepdims=True) + EPS) * qscale)
        o_ref[:, cols] = jnp.where(is_qk, nrm, yb)


def _ab_qkv(x, gain, shift, scale, w, conv_w):
    n = 3 * HW
    return pl.pallas_call(
        _ab_qkv_kernel,
        grid=(NTOK // PC_TM, n // PC_TN),
        in_specs=_pc_specs(0) + [pl.BlockSpec((D, PC_TN), lambda i, j: (0, j)),
                                 pl.BlockSpec((3, PC_TN), lambda i, j: (0, j))],
        out_specs=pl.BlockSpec((PC_TM, PC_TN), lambda i, j: (i, j)),
        out_shape=jax.ShapeDtypeStruct((NTOK, n), F32),
        scratch_shapes=[pltpu.VMEM((PC_TM + 2 * PC_HALO, D), BF16)],
        compiler_params=_params("arbitrary", "arbitrary"),
        name="ab_qkv",
    )(x, x, x, gain.reshape(1, D), shift, scale, w, conv_w)


def _gates_kernel(x_ref, alog_ref, dtb_ref, o_ref):
    x = x_ref[...]
    lane = lax.broadcasted_iota(jnp.int32, x.shape, 1)
    row = lax.broadcasted_iota(jnp.int32, x.shape, 0) & (CHUNK - 1)
    t = x + dtb_ref[...]
    softplus = jnp.maximum(t, 0.0) + jnp.log(1.0 + jnp.exp(-jnp.abs(t)))
    g = -jnp.exp(alog_ref[...]) * softplus
    pre, suf = g, g
    s = 1
    while s < CHUNK:
        pre = pre + jnp.where(row >= s, pltpu.roll(pre, s, 0), 0.0)
        suf = suf + jnp.where(row < CHUNK - s, pltpu.roll(suf, ROWBLK - s, 0), 0.0)
        s *= 2
    gc = jnp.where(lane < HEADS, pre, suf)
    o_ref[...] = jnp.where(lane < 2 * HEADS, gc, jax.nn.sigmoid(x))


def _gates(proj, a_log, dt_bias):
    pad = lambda v: jnp.pad(v.reshape(1, 2 * HEADS), ((0, 0), (0, LANE - 2 * HEADS)))
    return pl.pallas_call(
        _gates_kernel,
        grid=(NGROUP,),
        in_specs=[pl.BlockSpec((ROWBLK, LANE), lambda i: (i, CB_GATE)),
                  pl.BlockSpec((1, LANE), lambda i: (0, 0)),
                  pl.BlockSpec((1, LANE), lambda i: (0, 0))],
        out_specs=pl.BlockSpec((ROWBLK, LANE), lambda i: (i, 0)),
        out_shape=jax.ShapeDtypeStruct((NTOK, LANE), F32),
        compiler_params=_params("arbitrary"),
        name="ab_gates",
    )(proj, pad(a_log), pad(dt_bias))


DP_RB = 512
DP_NC = DP_RB // CHUNK
TRI_BASE = 8


def _split(x):
    hi = x.astype(BF16)
    return hi, (x - hi.astype(F32)).astype(BF16)


def _dot3(a, b):
    return _dot(a[0], b[0]) + _dot(a[0], b[1]) + _dot(a[1], b[0])


def _unit_tri_inv_batch(lms):
    ii = lax.broadcasted_iota(jnp.int32, (CHUNK, CHUNK), 0)
    jj = lax.broadcasted_iota(jnp.int32, (CHUNK, CHUNK), 1)
    same = lambda b: (ii >> int(math.log2(b))) == (jj >> int(math.log2(b)))
    eye = jnp.where(ii == jj, 1.0, 0.0)
    qs = [jnp.where(same(TRI_BASE), -lm, 0.0) for lm in lms]
    ps = [eye + q for q in qs]
    qs = [_split(q) for q in qs]
    for _ in range(int(math.log2(TRI_BASE)) - 1):
        qs = [_split(_dot3(q, q)) for q in qs]
        ps = [p + _dot3(_split(p), q) for p, q in zip(ps, qs)]
    b = TRI_BASE
    while b < CHUNK:
        off = jnp.logical_and(same(2 * b), jnp.logical_not(same(b)))
        pbs = [p.astype(BF16) for p in ps]
        ts = [_dot(p, jnp.where(off, lm, 0.0).astype(BF16)) for p, lm in zip(pbs, lms)]
        ps = [p - _dot(t.astype(BF16), p2) for p, p2, t in zip(ps, pbs, ts)]
        b *= 2
    return ps


def _delta_prep_kernel(q_ref, k_ref, v_ref, g_ref, gt_ref, u_ref, wq_ref, ak_ref, gl_ref):
    h = pl.program_id(1)
    ii = lax.broadcasted_iota(jnp.int32, (CHUNK, CHUNK), 0)
    jj = lax.broadcasted_iota(jnp.int32, (CHUNK, CHUNK), 1)
    probs = []
    for c in range(DP_NC):
        rows = slice(c * CHUNK, (c + 1) * CHUNK)
        q, k, v, gts = q_ref[rows, :], k_ref[rows, :], v_ref[rows, :], g_ref[rows, :]
        kbf = k.astype(BF16)
        qk = _dot_nt(q.astype(BF16), kbf)
        for d in range(2):
            incl = (ii >= jj) if d == 0 else (ii <= jj)
            strict = (ii > jj) if d == 0 else (ii < jj)
            gcol = _pick_lane(gts, d * HEADS + h)
            bcol = _pick_lane(gts, 2 * HEADS + d * HEADS + h)
            grow = gt_ref[d * HEADS + h][c:c + 1, :]
            dmask = jnp.where(incl, jnp.exp(jnp.where(incl, gcol - grow, 0.0)), 0.0)
            kb = k * bcol
            lm = jnp.where(strict, _dot_nt(kb.astype(BF16), kbf) * dmask, 0.0)
            probs.append((c, d, rows, q, k, v, kb, gcol, bcol, dmask, qk, incl, lm))
    tmats = _unit_tri_inv_batch([p[-1] for p in probs])
    for (c, d, rows, q, k, v, kb, gcol, bcol, dmask, qk, incl, _), tmat in zip(probs, tmats):
        gam = jnp.exp(gcol)
        rhs = jnp.concatenate([v * bcol, kb * gam], axis=1).astype(BF16)
        uw = _dot(tmat.astype(BF16), rhs)
        attn = jnp.where(incl, qk * dmask, 0.0)
        gtot = gcol[CHUNK - 1:CHUNK, :] if d == 0 else gcol[0:1, :]
        kd = k * jnp.exp(gtot - gcol)
        u_ref[d, rows, :] = uw[:, :DK]
        wq_ref[d, c] = jnp.concatenate([uw[:, DK:], q * gam], axis=0).astype(BF16)
        ak_ref[d, c] = jnp.concatenate([attn, kd.T], axis=0).astype(BF16)
        gl_ref[d, c] = jnp.broadcast_to(jnp.exp(gtot), (1, LANE))


def _delta_prep(qkv, gates, gates_t):
    nrb = NTOK // DP_RB
    blk = lambda cb: pl.BlockSpec((DP_RB, LANE), lambda i, h: (i, cb + h))
    return pl.pallas_call(
        _delta_prep_kernel,
        grid=(nrb, HEADS),
        in_specs=[blk(CB_QA), blk(CB_KA), blk(CB_VA),
                  pl.BlockSpec((DP_RB, LANE), lambda i, h: (i, 0)),
                  pl.BlockSpec((None, 2 * HEADS, DP_NC, CHUNK), lambda i, h: (i, 0, 0, 0))],
        out_specs=[pl.BlockSpec((2, DP_RB, LANE), lambda i, h: (0, i, h)),
                   pl.BlockSpec((2, None, DP_NC, 2 * CHUNK, DK), lambda i, h: (0, h, i, 0, 0)),
                   pl.BlockSpec((2, None, DP_NC, CHUNK + DK, CHUNK), lambda i, h: (0, h, i, 0, 0)),
                   pl.BlockSpec((2, None, DP_NC, 1, LANE), lambda i, h: (0, h, i, 0, 0))],
        out_shape=[jax.ShapeDtypeStruct((2, NTOK, HW), F32),
                   jax.ShapeDtypeStruct((2, HEADS, NCH, 2 * CHUNK, DK), BF16),
                   jax.ShapeDtypeStruct((2, HEADS, NCH, CHUNK + DK, CHUNK), BF16),
                   jax.ShapeDtypeStruct((2, HEADS, NCH, 1, LANE), F32)],
        compiler_params=_params("arbitrary", "arbitrary"),
        name="delta_prep",
    )(qkv, qkv, qkv, gates, gates_t)


def _delta_scan_kernel(*refs, L, zero_init, aliased):
    refs = list(refs)
    u_ref, wq_ref, ak_ref, gl_ref, z_ref, ng_ref = [refs.pop(0) for _ in range(6)]
    s0_ref = None if zero_init else refs.pop(0)
    if aliased:
        refs.pop(0)
    o_ref, sfin_ref, o_scr = refs
    n = L // CHUNK

    def body(c, carry):
        new = []
        for d in range(2):
            s = carry[d]
            cc = c if d == 0 else n - 1 - c
            r0 = pl.multiple_of(cc * CHUNK, CHUNK)
            r1 = _dot(wq_ref[d, cc], s.astype(BF16))
            vnew = u_ref[d, pl.ds(r0, CHUNK), :] - r1[:CHUNK]
            r2 = _dot(ak_ref[d, cc], vnew.astype(BF16))
            o_scr[d, pl.ds(r0, CHUNK), :] = r1[CHUNK:] + r2[:CHUNK]
            new.append(s * gl_ref[d, cc] + r2[CHUNK:])
        return tuple(new)

    if zero_init:
        init = (jnp.zeros((DK, DK), F32), jnp.zeros((DK, DK), F32))
    else:
        init = (s0_ref[0], s0_ref[1])
    fin = lax.fori_loop(0, n, body, init)
    sfin_ref[0] = fin[0]
    sfin_ref[1] = fin[1]
    o = o_scr[0] + o_scr[1]
    o = o * lax.rsqrt(jnp.mean(o * o, axis=-1, keepdims=True) + EPS) * ng_ref[...]
    o_ref[...] = o * _silu(z_ref[...])


def _delta_scan(stream, u, wq, ak, gl, proj, norm_gain, s0, prev_out):
    nseq, L = stream["nseq"], stream["L"]
    rb0 = stream["row0"] // L
    n = L // CHUNK
    in_specs = [pl.BlockSpec((2, L, LANE), lambda s, h: (0, rb0 + s, h)),
                pl.BlockSpec((2, None, n, 2 * CHUNK, DK), lambda s, h: (0, h, rb0 + s, 0, 0)),
                pl.BlockSpec((2, None, n, CHUNK + DK, CHUNK), lambda s, h: (0, h, rb0 + s, 0, 0)),
                pl.BlockSpec((2, None, n, 1, LANE), lambda s, h: (0, h, rb0 + s, 0, 0)),
                pl.BlockSpec((L, LANE), lambda s, h: (rb0 + s, CB_ZA + h)),
                pl.BlockSpec((1, LANE), lambda s, h: (0, 0))]
    args = [u, wq, ak, gl, proj, norm_gain.reshape(1, DK)]
    if s0 is not None:
        in_specs.append(pl.BlockSpec((None, 2, None, DK, DK), lambda s, h: (s, 0, h, 0, 0)))
        args.append(s0)
    aliases = {}
    if prev_out is not None:
        in_specs.append(pl.BlockSpec(memory_space=pl.ANY))
        aliases = {len(args): 0}
        args.append(prev_out)
    return pl.pallas_call(
        functools.partial(_delta_scan_kernel, L=L, zero_init=s0 is None, aliased=prev_out is not None),
        grid=(nseq, HEADS),
        in_specs=in_specs,
        out_specs=[pl.BlockSpec((L, LANE), lambda s, h: (rb0 + s, h)),
                   pl.BlockSpec((None, 2, None, DK, DK), lambda s, h: (s, 0, h, 0, 0))],
        out_shape=[jax.ShapeDtypeStruct((NTOK, HW), F32),
                   jax.ShapeDtypeStruct((nseq, 2, HEADS, DK, DK), F32)],
        scratch_shapes=[pltpu.VMEM((2, L, LANE), F32)],
        input_output_aliases=aliases,
        compiler_params=_params("arbitrary", "arbitrary"),
        name="delta_scan_%d" % L,
    )(*args)


def _ret_scan_kernel(*refs, L, zero_init, aliased):
    refs = list(refs)
    q_ref, k_ref, v_ref, gb_ref, dec_ref, ng_ref = [refs.pop(0) for _ in range(6)]
    r0_ref = None if zero_init else refs.pop(0)
    if aliased:
        refs.pop(0)
    o_ref, rfin_ref, o_scr = refs
    h = pl.program_id(1)
    n = L // CHUNK
    ii = lax.broadcasted_iota(jnp.int32, (CHUNK, CHUNK), 0)
    jj = lax.broadcasted_iota(jnp.int32, (CHUNK, CHUNK), 1)
    ci = lax.broadcasted_iota(jnp.int32, (CHUNK, 1), 0).astype(F32)
    consts = []
    for d in range(2):
        lg = -jnp.exp(_pick_lane(dec_ref[...], d * HEADS + h))
        dist = (ii - jj) if d == 0 else (jj - ii)
        dmat = jnp.where(dist >= 0, jnp.exp(jnp.maximum(dist, 0).astype(F32) * lg), 0.0)
        cross = jnp.exp(((ci + 1.0) if d == 0 else (CHUNK - ci)) * lg)
        sdec = jnp.exp(((CHUNK - 1.0 - ci) if d == 0 else ci) * lg)
        consts.append((dmat, cross, sdec, jnp.exp(CHUNK * lg)))

    def body(c, carry):
        new = []
        for d in range(2):
            dmat, cross, sdec, cdec = consts[d]
            r = carry[d]
            cc = c if d == 0 else n - 1 - c
            rows = pl.ds(pl.multiple_of(cc * CHUNK, CHUNK), CHUNK)
            q = q_ref[rows, :] * DK ** -0.5
            k, v = k_ref[rows, :], v_ref[rows, :]
            vb = v.astype(BF16)
            inner = _dot_nt(q.astype(BF16), k.astype(BF16)) * dmat
            o_scr[d, rows, :] = _dot(inner.astype(BF16), vb) + _dot((q * cross).astype(BF16), r.astype(BF16))
            new.append(r * cdec + _dot_tn((k * sdec).astype(BF16), vb))
        return tuple(new)

    if zero_init:
        init = (jnp.zeros((DK, DK), F32), jnp.zeros((DK, DK), F32))
    else:
        init = (r0_ref[0], r0_ref[1])
    fin = lax.fori_loop(0, n, body, init)
    rfin_ref[0] = fin[0]
    rfin_ref[1] = fin[1]
    o = o_scr[0] + o_scr[1]
    mu = jnp.mean(o, axis=-1, keepdims=True)
    var = jnp.mean(jnp.square(o - mu), axis=-1, keepdims=True)
    o = (o - mu) * lax.rsqrt(var + EPS) * ng_ref[...]
    o_ref[...] = o * _silu(gb_ref[...])


def _ret_scan(stream, proj, ret_decay, norm_gain, r0, prev_out):
    nseq, L = stream["nseq"], stream["L"]
    rb0 = stream["row0"] // L
    blk = lambda cb: pl.BlockSpec((L, LANE), lambda s, h: (rb0 + s, cb + h))
    in_specs = [blk(CB_QB), blk(CB_KB), blk(CB_VB), blk(CB_GB),
                pl.BlockSpec((1, LANE), lambda s, h: (0, 0)),
                pl.BlockSpec((1, LANE), lambda s, h: (0, 0))]
    dec = jnp.pad(ret_decay.reshape(1, 2 * HEADS), ((0, 0), (0, LANE - 2 * HEADS)))
    args = [proj, proj, proj, proj, dec, norm_gain.reshape(1, DK)]
    if r0 is not None:
        in_specs.append(pl.BlockSpec((None, 2, None, DK, DK), lambda s, h: (s, 0, h, 0, 0)))
        args.append(r0)
    aliases = {}
    if prev_out is not None:
        in_specs.append(pl.BlockSpec(memory_space=pl.ANY))
        aliases = {len(args): 0}
        args.append(prev_out)
    return pl.pallas_call(
        functools.partial(_ret_scan_kernel, L=L, zero_init=r0 is None, aliased=prev_out is not None),
        grid=(nseq, HEADS),
        in_specs=in_specs,
        out_specs=[pl.BlockSpec((L, LANE), lambda s, h: (rb0 + s, h)),
                   pl.BlockSpec((None, 2, None, DK, DK), lambda s, h: (s, 0, h, 0, 0))],
        out_shape=[jax.ShapeDtypeStruct((NTOK, HW), F32),
                   jax.ShapeDtypeStruct((nseq, 2, HEADS, DK, DK), F32)],
        scratch_shapes=[pltpu.VMEM((2, L, LANE), F32)],
        input_output_aliases=aliases,
        compiler_params=_params("arbitrary", "arbitrary"),
        name="ret_scan_%d" % L,
    )(*args)


SCAN_G = LCTX // CHUNK
SCAN_RB = SCAN_G * CHUNK
SCAN_CTX_STEPS = TOK_CTX // SCAN_RB
SCAN_LAT_STEPS = LLAT // SCAN_RB
assert SCAN_RB == LCTX and SCAN_CTX_STEPS % SCAN_LAT_STEPS == 0


def _scan_bwd_group(t):
    i = t % SCAN_LAT_STEPS
    return jnp.where(t < SCAN_CTX_STEPS, t, t - i + (SCAN_LAT_STEPS - 1 - i))


def _scan_seq(t):
    return jnp.where(t < SCAN_CTX_STEPS, t, NCTX + (t - SCAN_CTX_STEPS) // SCAN_LAT_STEPS)


def _ab_scan_kernel(uf_ref, ub_ref, wqf_ref, wqb_ref, akf_ref, akb_ref, glf_ref, glb_ref,
                    qf_ref, kf_ref, vf_ref, qb_ref, kb_ref, vb_ref, dec_ref, s0_ref, r0_ref,
                    odf_ref, odb_ref, orf_ref, orb_ref, sfin_ref, rfin_ref, s_scr, r_scr):
    t = pl.program_id(0)
    in_ctx = t < SCAN_CTX_STEPS
    first = jnp.logical_or(in_ctx, t % SCAN_LAT_STEPS == 0)
    last = jnp.logical_or(in_ctx, t % SCAN_LAT_STEPS == SCAN_LAT_STEPS - 1)

    @pl.when(first)
    def _():
        s_scr[...] = jnp.where(in_ctx, 0.0, s0_ref[...])
        r_scr[...] = jnp.where(in_ctx, 0.0, r0_ref[...])

    dirs = ((uf_ref, wqf_ref, akf_ref, glf_ref, qf_ref, kf_ref, vf_ref, odf_ref, orf_ref),
            (ub_ref, wqb_ref, akb_ref, glb_ref, qb_ref, kb_ref, vb_ref, odb_ref, orb_ref))

    ii = lax.broadcasted_iota(jnp.int32, (SCAN_RB, SCAN_RB), 0)
    jj = lax.broadcasted_iota(jnp.int32, (SCAN_RB, SCAN_RB), 1)
    ci = lax.broadcasted_iota(jnp.int32, (SCAN_RB, 1), 0).astype(F32)
    for d in range(2):
        q_ref, k_ref, v_ref, or_ref = dirs[d][4], dirs[d][5], dirs[d][6], dirs[d][8]
        dist = (ii - jj) if d == 0 else (jj - ii)
        for h in range(HEADS):
            cols = slice(h * DK, (h + 1) * DK)
            lane = d * HEADS + h
            lg = -jnp.exp(dec_ref[:, lane:lane + 1])
            dmat = jnp.where(dist >= 0, jnp.exp(jnp.maximum(dist, 0).astype(F32) * lg), 0.0)
            cross = jnp.exp(((ci + 1.0) if d == 0 else (SCAN_RB - ci)) * lg)
            sdec = jnp.exp(((SCAN_RB - 1.0 - ci) if d == 0 else ci) * lg)
            r = r_scr[d, h]
            q = q_ref[:, cols] * DK ** -0.5
            k = k_ref[:, cols]
            vb = v_ref[:, cols].astype(BF16)
            inner = _dot_nt(q.astype(BF16), k.astype(BF16)) * dmat
            or_ref[:, cols] = _dot(inner.astype(BF16), vb) + _dot((q * cross).astype(BF16), r.astype(BF16))
            r_scr[d, h] = r * jnp.exp(SCAN_RB * lg) + _dot_tn((k * sdec).astype(BF16), vb)

    s_cur = {(d, h): s_scr[d, h] for d in range(2) for h in range(HEADS)}
    for c in range(SCAN_G):
        for d in range(2):
            u_ref, wq_ref, ak_ref, gl_ref, q_ref, k_ref, v_ref, od_ref, or_ref = dirs[d]
            cc = c if d == 0 else SCAN_G - 1 - c
            rows = slice(cc * CHUNK, (cc + 1) * CHUNK)
            for h in range(HEADS):
                cols = slice(h * DK, (h + 1) * DK)
                s = s_cur[d, h]
                r1 = _dot(wq_ref[h, cc], s.astype(BF16))
                vnew = u_ref[rows, cols] - r1[:CHUNK]
                r2 = _dot(ak_ref[h, cc], vnew.astype(BF16))
                od_ref[rows, cols] = r1[CHUNK:] + r2[:CHUNK]
                s_cur[d, h] = s * gl_ref[h, cc] + r2[CHUNK:]
    for d in range(2):
        for h in range(HEADS):
            s_scr[d, h] = s_cur[d, h]

    @pl.when(last)
    def _():
        sfin_ref[...] = s_scr[...]
        rfin_ref[...] = r_scr[...]


def _ab_scan(u, wq, ak, gl, proj, ret_decay, s0, r0):
    nseq = NCTX + NLAT
    fwd = lambda t: t
    row = lambda g, cb: pl.BlockSpec((SCAN_RB, HW), lambda t: (g(t), cb))
    u_spec = lambda d, g: pl.BlockSpec((None, SCAN_RB, HW), lambda t: (d, g(t), 0))
    op_spec = lambda d, g, a, b: pl.BlockSpec((None, HEADS, SCAN_G, a, b), lambda t: (d, 0, g(t), 0, 0))
    state = pl.BlockSpec((None, 2, HEADS, DK, DK), lambda t: (_scan_seq(t), 0, 0, 0, 0))
    state0 = pl.BlockSpec((None, 2, HEADS, DK, DK), lambda t: (jnp.maximum(_scan_seq(t) - NCTX, 0), 0, 0, 0, 0))
    in_specs = [u_spec(0, fwd), u_spec(1, _scan_bwd_group),
                op_spec(0, fwd, 2 * CHUNK, DK), op_spec(1, _scan_bwd_group, 2 * CHUNK, DK),
                op_spec(0, fwd, CHUNK + DK, CHUNK), op_spec(1, _scan_bwd_group, CHUNK + DK, CHUNK),
                op_spec(0, fwd, 1, LANE), op_spec(1, _scan_bwd_group, 1, LANE),
                row(fwd, CB_QB // HEADS), row(fwd, CB_KB // HEADS), row(fwd, CB_VB // HEADS),
                row(_scan_bwd_group, CB_QB // HEADS), row(_scan_bwd_group, CB_KB // HEADS),
                row(_scan_bwd_group, CB_VB // HEADS),
                pl.BlockSpec((1, LANE), lambda t: (0, 0)), state0, state0]
    dec = jnp.pad(ret_decay.reshape(1, 2 * HEADS), ((0, 0), (0, LANE - 2 * HEADS)))
    o_sds = jax.ShapeDtypeStruct((NTOK, HW), F32)
    st_sds = jax.ShapeDtypeStruct((nseq, 2, HEADS, DK, DK), F32)
    return pl.pallas_call(
        _ab_scan_kernel,
        grid=(NTOK // SCAN_RB,),
        in_specs=in_specs,
        out_specs=[row(fwd, 0), row(_scan_bwd_group, 0), row(fwd, 0), row(_scan_bwd_group, 0), state, state],
        out_shape=[o_sds, o_sds, o_sds, o_sds, st_sds, st_sds],
        scratch_shapes=[pltpu.VMEM((2, HEADS, DK, DK), F32), pltpu.VMEM((2, HEADS, DK, DK), F32)],
        compiler_params=_params("arbitrary"),
        name="ab_scan",
    )(u, u, wq, wq, ak, ak, gl, gl, proj, proj, proj, proj, proj, proj, dec, s0, r0)


def _ab_out_kernel(odf_ref, odb_ref, orf_ref, orb_ref, z_ref, gb_ref, na_ref, nb_ref, w_ref, x_ref, gate_ref,
                   o_ref):
    oa = odf_ref[...] + odb_ref[...]
    ob = orf_ref[...] + orb_ref[...]
    z, gb = z_ref[...], gb_ref[...]
    parts = []
    for h in range(HEADS):
        cols = slice(h * DK, (h + 1) * DK)
        a = oa[:, cols]
        a = a * lax.rsqrt(jnp.mean(a * a, axis=-1, keepdims=True) + EPS) * na_ref[...]
        parts.append(a * _silu(z[:, cols]))
    for h in range(HEADS):
        cols = slice(h * DK, (h + 1) * DK)
        b = ob[:, cols]
        mu = jnp.mean(b, axis=-1, keepdims=True)
        var = jnp.mean(jnp.square(b - mu), axis=-1, keepdims=True)
        parts.append((b - mu) * lax.rsqrt(var + EPS) * nb_ref[...] * _silu(gb[:, cols]))
    a = jnp.concatenate(parts, axis=1).astype(BF16)
    o_ref[...] = x_ref[...] + gate_ref[...] * _dot(a, w_ref[...])


def _ab_out(odf, odb, orf, orb, proj, norm_a, norm_b, w_out, x, gate):
    grp = lambda i: (i * MMR_TM // ROWBLK, 0, 0)
    tile = lambda cb: pl.BlockSpec((MMR_TM, HW), lambda i: (i, cb))
    vec = pl.BlockSpec((1, DK), lambda i: (0, 0))
    return pl.pallas_call(
        _ab_out_kernel,
        grid=(NTOK // MMR_TM,),
        in_specs=[tile(0), tile(0), tile(0), tile(0), tile(CB_ZA // HEADS), tile(CB_GB // HEADS), vec, vec,
                  pl.BlockSpec((2 * HW, D), lambda i: (0, 0)),
                  pl.BlockSpec((MMR_TM, D), lambda i: (i, 0)), pl.BlockSpec((None, 1, D), grp)],
        out_specs=pl.BlockSpec((MMR_TM, D), lambda i: (i, 0)),
        out_shape=jax.ShapeDtypeStruct((NTOK, D), F32),
        compiler_params=_params("arbitrary"),
        name="ab_out",
    )(odf, odb, orf, orb, proj, proj, norm_a.reshape(1, DK), norm_b.reshape(1, DK), w_out, x, gate)


def _hy_conv_kernel(x0_ref, x1_ref, v_ref, w0_ref, w1_ref, wv_ref, b0_ref, b1_ref, bv_ref, x0o_ref, p_ref):
    i = pl.program_id(0)
    x0o_ref[...] = _conv3(x0_ref[...], w0_ref[...], i) + b0_ref[...]
    x1 = _conv3(x1_ref[...], w1_ref[...], i) + b1_ref[...]
    v = _conv3(v_ref[...], wv_ref[...], i) + bv_ref[...]
    p_ref[...] = v * x1


def _hy_conv(u3, conv_w, conv_b):
    ncb = D // LANE
    xb = lambda k: pl.BlockSpec((ROWBLK, LANE), lambda i, j: (i, k * ncb + j))
    wb = lambda k: pl.BlockSpec((3, LANE), lambda i, j: (0, k * ncb + j))
    bb = lambda k: pl.BlockSpec((1, LANE), lambda i, j: (0, k * ncb + j))
    cb = conv_b.reshape(1, 3 * D)
    return pl.pallas_call(
        _hy_conv_kernel,
        grid=(NGROUP, ncb),
        in_specs=[xb(0), xb(1), xb(2), wb(0), wb(1), wb(2), bb(0), bb(1), bb(2)],
        out_specs=[pl.BlockSpec((ROWBLK, LANE), lambda i, j: (i, j))] * 2,
        out_shape=[jax.ShapeDtypeStruct((NTOK, D), F32)] * 2,
        compiler_params=_params("arbitrary", "arbitrary"),
        name="hy_conv",
    )(u3, u3, u3, conv_w, conv_w, conv_w, cb, cb, cb)


def _filter_features(L):
    r = np.arange(2 * L)
    pos = np.where(r < L, r, 2 * L - r) % L
    t = pos / (L - 1.0)
    bands = np.linspace(1e-4, HY_BANDS - 1, HY_BANDS)
    ang = 2.0 * np.pi * np.outer(pos, bands) / L
    z = np.zeros((2 * L, HY_FW), np.float64)
    z[:, 0] = t
    z[:, 1:1 + HY_BANDS] = np.cos(ang)
    z[:, 1 + HY_BANDS:HY_EMB] = -np.sin(ang)
    z[:, HY_EMB] = (r != L)
    return z.astype(np.float32)


def _filter_kernel(z_ref, w1_ref, b1_ref, f1_ref, w2_ref, b2_ref, f2_ref, w3_ref, dl_ref, o_ref):
    z = z_ref[...]
    hid = jnp.sin(f1_ref[...] * (_dot(z, w1_ref[...], HI) + b1_ref[...]))
    hid = jnp.sin(f2_ref[...] * (_dot(hid, w2_ref[...], HI) + b2_ref[...]))
    filt = _dot(hid, w3_ref[...], HI)
    window = jnp.exp(-z[:, 0:1] * dl_ref[...]) * z[:, HY_EMB:HY_EMB + 1]
    o_ref[...] = filt * window


def _hy_filter(L, w1, b1, f1, w2, b2, f2, w3):
    rb = min(512, L)
    nblk = 2 * L // rb
    z = jnp.asarray(_filter_features(L))
    w1p = jnp.pad(w1, ((0, HY_FW - HY_EMB), (0, 0)))
    min_decay = math.log(HY_TARGET) / HY_SLOW_PCT
    max_decay = math.log(HY_TARGET) / HY_FAST_PCT
    deltas = jnp.asarray(np.abs(np.linspace(min_decay, max_decay, D)).astype(np.float32).reshape(1, D))
    vec = lambda v: v.reshape(1, HY_FW)
    full = lambda shp: pl.BlockSpec(shp, lambda i: (0, 0))
    return pl.pallas_call(
        _filter_kernel,
        grid=(nblk,),
        in_specs=[pl.BlockSpec((rb, HY_FW), lambda i: (i, 0)),
                  full((HY_FW, HY_FW)), full((1, HY_FW)), full((1, HY_FW)),
                  full((HY_FW, HY_FW)), full((1, HY_FW)), full((1, HY_FW)),
                  pl.BlockSpec((HY_FW, D), lambda i: (0, i // (nblk // 2))),
                  full((1, D))],
        out_specs=pl.BlockSpec((rb, D), lambda i: (i, 0)),
        out_shape=jax.ShapeDtypeStruct((2 * L, D), F32),
        compiler_params=_params("arbitrary"),
        name="hy_filter_%d" % L,
    )(z, w1p, vec(b1), vec(f1), w2, vec(b2), vec(f2), w3, deltas)


FFT_RT = 8


def _cis(num, den):
    ang = -2.0 * np.pi * (num % den) / den
    return np.cos(ang), np.sin(ang)


def _fft_consts(L):
    n = 2 * L
    n1 = n // FFT_N2
    k1 = np.arange(n1)
    fr, fi = _cis(np.outer(k1, k1), n1)
    half = n1 // 2
    sig = np.block([[fr[:, :half], -fi[:, :half]], [fi[:, :half], fr[:, :half]]])
    ker = np.concatenate([fr, fi], axis=0)
    cr, ci = fr[:, :half].T, -fi[:, :half].T
    inv = np.stack([np.concatenate([cr, -ci], axis=1), np.concatenate([ci, cr], axis=1)]) / n
    k2 = np.arange(FFT_N2)
    gr, gi = _cis(np.outer(k2, k2), FFT_N2)
    f2 = np.block([[gr, -gi], [gi, gr]])
    f2inv = np.block([[gr, gi], [-gi, gr]])
    tr, ti = _cis(np.outer(k1, k2), n)
    f = lambda a: np.asarray(a, np.float32)
    return dict(n1=n1, sig=f(sig), ker=f(ker), inv=f(inv), f2=f(f2), f2inv=f(f2inv),
                twr=f(tr).reshape(n1, FFT_N2, 1), twi=f(ti).reshape(n1, FFT_N2, 1))


def _lin_rows(mat, rows):
    out = []
    for m in range(mat.shape[0]):
        acc = None
        for k in range(mat.shape[1]):
            cf = float(mat[m, k])
            if abs(cf) < 1e-9:
                continue
            term = rows[k] if abs(cf - 1.0) < 1e-9 else (-rows[k] if abs(cf + 1.0) < 1e-9 else cf * rows[k])
            acc = term if acc is None else acc + term
        out.append(jnp.zeros_like(rows[0]) if acc is None else acc)
    return out


def _twiddle_dft(slab_r, slab_i, twr, twi, f2):
    x = jnp.concatenate([slab_r * twr - slab_i * twi, slab_r * twi + slab_i * twr], axis=0)
    y = _dot3(f2, _split(x))
    return y[:FFT_N2], y[FFT_N2:]


def _filter_idft_twiddle(yr, yi, kr, ki, twr, twi, f2inv):
    z = jnp.concatenate([yr * kr - yi * ki, yr * ki + yi * kr], axis=0)
    w = _dot3(f2inv, _split(z))
    wr, wi = w[:FFT_N2], w[FFT_N2:]
    return wr * twr + wi * twi, wi * twr - wr * twi


def _fft1_kernel(m_ref, *refs):
    o_ref = refs[-1]
    m = _split(m_ref[...])
    n1 = o_ref.shape[1]
    for r in range(FFT_RT):
        parts = [x_ref[c:c + 32, r, :] for x_ref in refs[:-1] for c in range(0, x_ref.shape[0], 32)]
        x = jnp.concatenate(parts, axis=0) if len(parts) > 1 else parts[0]
        y = _dot3(m, _split(x))
        o_ref[0, :, r, :] = y[:n1]
        o_ref[1, :, r, :] = y[n1:]


def _fft_stage1(consts, mat, xs, lead_blocks):
    n1 = consts["n1"]
    rows = mat.shape[1] // len(xs)
    specs = [pl.BlockSpec((rows, FFT_RT, D), functools.partial(lambda j, lb: (lb, j, 0), lb=lb))
             for lb in lead_blocks]
    return pl.pallas_call(
        _fft1_kernel,
        grid=(FFT_N2 // FFT_RT,),
        in_specs=[pl.BlockSpec(mat.shape, lambda j: (0, 0))] + specs,
        out_specs=pl.BlockSpec((2, n1, FFT_RT, D), lambda j: (0, 0, j, 0)),
        out_shape=jax.ShapeDtypeStruct((2, n1, FFT_N2, D), F32),
        compiler_params=_params("arbitrary"),
        name="fft1_%d" % len(xs),
    )(jnp.asarray(mat), *xs)


def _fft2_kernel(a_ref, twr_ref, twi_ref, f2_ref, *rest, conv):
    twr, twi = twr_ref[...], twi_ref[...]
    yr, yi = _twiddle_dft(a_ref[0], a_ref[1], twr, twi, _split(f2_ref[...]))
    if conv:
        kf_ref, f2inv_ref, o_ref = rest
        yr, yi = _filter_idft_twiddle(yr, yi, kf_ref[0], kf_ref[1], twr, twi, _split(f2inv_ref[...]))
    else:
        o_ref, = rest
    o_ref[0] = yr
    o_ref[1] = yi


def _fft_stage2(consts, a, kf):
    n1 = consts["n1"]
    slab = pl.BlockSpec((2, None, FFT_N2, D), lambda k: (0, k, 0, 0))
    tw = pl.BlockSpec((None, FFT_N2, 1), lambda k: (k, 0, 0))
    mat = pl.BlockSpec((2 * FFT_N2, 2 * FFT_N2), lambda k: (0, 0))
    in_specs = [slab, tw, tw, mat]
    args = [a, jnp.asarray(consts["twr"]), jnp.asarray(consts["twi"]), jnp.asarray(consts["f2"])]
    if kf is not None:
        in_specs += [slab, mat]
        args += [kf, jnp.asarray(consts["f2inv"])]
    return pl.pallas_call(
        functools.partial(_fft2_kernel, conv=kf is not None),
        grid=(n1,),
        in_specs=in_specs,
        out_specs=slab,
        out_shape=jax.ShapeDtypeStruct(a.shape, F32),
        compiler_params=_params("arbitrary"),
        name="fft2_%s" % ("conv" if kf is not None else "spec"),
    )(*args)


def _fft3_kernel(m_ref, b_ref, p_ref, x0_ref, bias_ref, *rest):
    o_ref = rest[-1]
    m = _split(m_ref[...])
    bias = bias_ref[...]
    for r in range(FFT_RT):
        x = jnp.concatenate([b_ref[0, :, r, :], b_ref[1, :, r, :]], axis=0)
        y = _dot3(m, _split(x))
        o_ref[:, r, :] = (y + p_ref[:, r, :] * bias) * x0_ref[:, r, :]


def _fft_stage3(consts, b, pv, xv, bias, lead_block, prev_out):
    n1 = consts["n1"]
    half = n1 // 2
    seq = pl.BlockSpec((half, FFT_RT, D), lambda j, s: (lead_block + s, j, 0))
    in_specs = [pl.BlockSpec((None, half, 2 * n1), lambda j, s: (s, 0, 0)),
                pl.BlockSpec((2, n1, FFT_RT, D), lambda j, s: (0, 0, j, 0)),
                seq, seq, pl.BlockSpec((1, D), lambda j, s: (0, 0)),
                pl.BlockSpec(memory_space=pl.ANY)]
    return pl.pallas_call(
        _fft3_kernel,
        grid=(FFT_N2 // FFT_RT, 2),
        in_specs=in_specs,
        out_specs=seq,
        out_shape=jax.ShapeDtypeStruct(pv.shape, F32),
        input_output_aliases={5: 0},
        compiler_params=_params("arbitrary", "arbitrary"),
        name="fft3",
    )(jnp.asarray(consts["inv"]), b, pv, xv, bias.reshape(1, D), prev_out)


def _fft_ctx_kernel(p_ref, x0_ref, kern_ref, twr_ref, twi_ref, f2_ref, f2inv_ref, bias_ref, o_ref, kf_ref,
                    *, n1, sig, ker, inv):
    f2 = _split(f2_ref[...])

    @pl.when(pl.program_id(0) == 0)
    def _():
        a = _lin_rows(ker, [kern_ref[r] for r in range(n1)])
        for k in range(n1):
            kf_ref[0, k], kf_ref[1, k] = _twiddle_dft(a[k], a[n1 + k], twr_ref[k], twi_ref[k], f2)

    f2inv = _split(f2inv_ref[...])
    a = _lin_rows(sig, [p_ref[r] for r in range(n1)])
    br, bi = [], []
    for k in range(n1):
        twr, twi = twr_ref[k], twi_ref[k]
        yr, yi = _twiddle_dft(a[k], a[n1 + k], twr, twi, f2)
        wr, wi = _filter_idft_twiddle(yr, yi, kf_ref[0, k], kf_ref[1, k], twr, twi, f2inv)
        br.append(wr)
        bi.append(wi)
    out = _lin_rows(inv, br + bi)
    bias = bias_ref[...]
    for m, y in enumerate(out):
        o_ref[m] = (y + p_ref[m] * bias) * x0_ref[m]


def _fft_ctx(consts, pv, xv, kern, bias):
    n1 = consts["n1"]
    inv = np.concatenate([consts["inv"][0], consts["inv"][1]], axis=0)
    pair = pl.BlockSpec((n1, FFT_N2, D), lambda q: (q, 0, 0))
    full = lambda shp: pl.BlockSpec(shp, lambda q: (0,) * len(shp))
    return pl.pallas_call(
        functools.partial(_fft_ctx_kernel, n1=n1, sig=consts["sig"], ker=consts["ker"], inv=inv),
        grid=(NCTX // 2,),
        in_specs=[pair, pair, full((n1, FFT_N2, D)), full((n1, FFT_N2, 1)), full((n1, FFT_N2, 1)),
                  full((2 * FFT_N2, 2 * FFT_N2)), full((2 * FFT_N2, 2 * FFT_N2)), full((1, D))],
        out_specs=pair,
        out_shape=jax.ShapeDtypeStruct(pv.shape, F32),
        scratch_shapes=[pltpu.VMEM((2, n1, FFT_N2, D), F32)],
        compiler_params=_params("arbitrary"),
        name="fft_ctx",
    )(pv, xv, kern, jnp.asarray(consts["twr"]), jnp.asarray(consts["twi"]), jnp.asarray(consts["f2"]),
      jnp.asarray(consts["f2inv"]), bias.reshape(1, D))


def _hy_long_conv(p, x0, filt_w, bias):
    slabs = (NTOK // FFT_N2, FFT_N2, D)
    pv, xv = p.reshape(slabs), x0.reshape(slabs)
    cc = _fft_consts(LCTX)
    out = _fft_ctx(cc, pv, xv, _hy_filter(LCTX, *filt_w).reshape(cc["n1"], FFT_N2, D), bias)
    lc = _fft_consts(LLAT)
    n1 = lc["n1"]
    kern = _hy_filter(LLAT, *filt_w).reshape(n1, FFT_N2, D)
    kf = _fft_stage2(lc, _fft_stage1(lc, lc["ker"], [kern], [0]), None)
    lat0 = TOK_CTX // FFT_N2 // (n1 // 2)
    a = _fft_stage1(lc, lc["sig"], [pv, pv], [lat0, lat0 + 1])
    b = _fft_stage2(lc, a, kf)
    return _fft_stage3(lc, b, pv, xv, bias, lat0, out).reshape(NTOK, D)


FFN_TM = 1024
FFN_TF = 256
FFN_HALO = 128


FFN_SR = 32


FFN_CH = 256
FFN_NF = D_FF // FFN_TF
FFN_TILES = NTOK // FFN_TM * FFN_NF


def _ffn_tile(t):
    return jnp.minimum(t, FFN_TILES - 1) // FFN_NF, t % FFN_NF


def _ffn_act_kernel(xc_ref, xp_ref, xn_ref, g_ref, sh_ref, sc_ref, wg_ref, wu_ref, cw_ref, cb_ref,
                    o_ref, h_ref, *slots):
    t = pl.program_id(0)
    i, f = _ffn_tile(t)
    ext = FFN_TM + 2 * FFN_HALO
    is_lat = i >= TOK_CTX // FFN_TM

    @pl.when(t == 0)
    def _():
        for scr in slots[4:]:
            scr[...] = jnp.zeros(scr.shape, F32)

    @pl.when(f == 0)
    def _():
        def norm(x):
            y = x * lax.rsqrt(jnp.mean(x * x, axis=-1, keepdims=True) + EPS) * g_ref[...]
            return (y * (1.0 + sc_ref[...]) + sh_ref[...]).astype(BF16)
        seq_pos = (i * FFN_TM - TOK_CTX) & (LLAT - 1)
        keep_prev = jnp.logical_and(is_lat, seq_pos != 0)
        keep_next = jnp.logical_and(is_lat, seq_pos != LLAT - FFN_TM)
        h_ref[0:FFN_HALO, :] = jnp.where(keep_prev, norm(xp_ref[...]), 0.0).astype(BF16)
        h_ref[FFN_HALO:FFN_HALO + FFN_TM, :] = norm(xc_ref[...])
        h_ref[FFN_HALO + FFN_TM:ext, :] = jnp.where(keep_next, norm(xn_ref[...]), 0.0).astype(BF16)

    def project_gate(scr, lo, hi):
        gate_scr, left_scr, right_scr, _ = scr
        gate = _dot(h_ref[lo:hi, :], wg_ref[...])
        period = jnp.where(is_lat, GRID_W, LCTX)
        r = lax.broadcasted_iota(jnp.int32, (hi - lo, 1), 0)
        col = (r + lo + FFN_HALO) & (period - 1)
        gate_scr[lo:hi, :] = gate
        left_scr[lo:hi, :] = jnp.where(col == 0, 0.0, pltpu.roll(gate, 1, 0))
        right_scr[lo:hi, :] = jnp.where(col == period - 1, 0.0, pltpu.roll(gate, hi - lo - 1, 0))

    def project_up(scr, lo, hi):
        scr[3][lo - FFN_HALO:hi - FFN_HALO, :] = _dot(h_ref[lo:hi, :], wu_ref[...])

    def convolve(scr, strips):
        gate_scr, left_scr, right_scr, up_scr = scr
        ip, _ = _ffn_tile(jnp.maximum(t - 1, 0))
        kidx = lax.broadcasted_iota(jnp.int32, (9, 1), 0)
        mid_row = jnp.logical_and(kidx >= 3, kidx < 6)
        w = jnp.where(jnp.logical_or(ip >= TOK_CTX // FFN_TM, mid_row), cw_ref[...], 0.0)
        bias = cb_ref[...]
        for s in strips:
            z = bias
            for k in range(3):
                lo = s * FFN_SR + FFN_HALO + (k - 1) * GRID_W
                z = z + left_scr[lo:lo + FFN_SR, :] * w[3 * k:3 * k + 1] \
                    + gate_scr[lo:lo + FFN_SR, :] * w[3 * k + 1:3 * k + 2] \
                    + right_scr[lo:lo + FFN_SR, :] * w[3 * k + 2:3 * k + 3]
            out = slice(s * FFN_SR, (s + 1) * FFN_SR)
            o_ref[out, :] = ((0.5 * z) * (1.0 + jnp.tanh(0.5 * z)) * up_scr[out, :]).astype(BF16)

    def step(cur, prev):
        edges = [0] + list(range(FFN_HALO, FFN_HALO + FFN_TM + 1, FFN_CH)) + [ext]
        pieces = []
        for lo, hi in zip(edges[:-1], edges[1:]):
            pieces.append(functools.partial(project_gate, cur, lo, hi))
            if FFN_HALO <= lo < FFN_HALO + FFN_TM:
                pieces.append(functools.partial(project_up, cur, lo, hi))
        nst = FFN_TM // FFN_SR
        for c, piece in enumerate(pieces):
            piece()
            convolve(prev, range(c * nst // len(pieces), (c + 1) * nst // len(pieces)))

    pl.when(t % 2 == 0)(lambda: step(slots[:4], slots[4:]))
    pl.when(t % 2 == 1)(lambda: step(slots[4:], slots[:4]))


def _ffn_act(x, gain, shift, scale, w_gu, conv_w, conv_b):
    hb = FFN_TM // FFN_HALO
    nhb = NTOK // FFN_HALO
    ext = FFN_TM + 2 * FFN_HALO
    row = lambda t: _ffn_tile(t)[0]
    col = lambda t: _ffn_tile(t)[1]
    prev = lambda t: _ffn_tile(jnp.maximum(t - 1, 0))
    vec = pl.BlockSpec((None, 1, D), lambda t: (row(t) * FFN_TM // ROWBLK, 0, 0))
    in_specs = [pl.BlockSpec((FFN_TM, D), lambda t: (row(t), 0)),
                pl.BlockSpec((FFN_HALO, D), lambda t: (jnp.maximum(row(t) * hb - 1, 0), 0)),
                pl.BlockSpec((FFN_HALO, D), lambda t: (jnp.minimum((row(t) + 1) * hb, nhb - 1), 0)),
                pl.BlockSpec((1, D), lambda t: (0, 0)), vec, vec,
                pl.BlockSpec((D, FFN_TF), lambda t: (0, col(t))),
                pl.BlockSpec((D, FFN_TF), lambda t: (0, FFN_NF + col(t))),
                pl.BlockSpec((9, FFN_TF), lambda t: (0, prev(t)[1])),
                pl.BlockSpec((1, FFN_TF), lambda t: (0, prev(t)[1]))]
    return pl.pallas_call(
        _ffn_act_kernel,
        grid=(FFN_TILES + 1,),
        in_specs=in_specs,
        out_specs=pl.BlockSpec((FFN_TM, FFN_TF), lambda t: prev(t)),
        out_shape=jax.ShapeDtypeStruct((NTOK, D_FF), BF16),
        scratch_shapes=[pltpu.VMEM((ext, D), BF16)]
        + 2 * (3 * [pltpu.VMEM((ext, FFN_TF), F32)] + [pltpu.VMEM((FFN_TM, FFN_TF), F32)]),
        compiler_params=_params("arbitrary"),
        name="ffn_act",
    )(x, x, x, gain.reshape(1, D), shift, scale, w_gu, w_gu, conv_w.reshape(9, D_FF), conv_b.reshape(1, D_FF))


def _ab_w_in_cols(w):
    qkv_w, gate_c0 = 3 * HW, 3 * HW + HW
    rest = jnp.concatenate([w[:, qkv_w:gate_c0], w[:, gate_c0 + 4 * HEADS:], w[:, gate_c0:gate_c0 + 4 * HEADS],
                            jnp.zeros((D, LANE - 4 * HEADS), F32)], axis=1)
    return w[:, :qkv_w].astype(BF16), rest.astype(BF16)


def _ab_mixer(qkv, proj, a_log, dt_bias, ret_decay, s_delta0, s_ret0):
    gates = _gates(proj, a_log, dt_bias)
    gates_t = gates[:, :2 * HEADS].reshape(NTOK // DP_RB, DP_NC, CHUNK, 2 * HEADS).transpose(0, 3, 1, 2)
    u, wq, ak, gl = _delta_prep(qkv, gates, gates_t)
    odf, odb, orf, orb, sfin, rfin = _ab_scan(u, wq, ak, gl, proj, ret_decay, s_delta0, s_ret0)
    return (odf, odb, orf, orb), sfin[:NCTX], rfin[:NCTX]


def kernel(x_prompt, x_sample, state_delta, state_ret, c, c_ctx, mod_w, mod_b, norm1, norm2, ab_w_in, ab_conv, ab_a_log, ab_dt_bias, ab_norm_a, ab_norm_b, ab_ret_decay, ab_w_out, hy_w_in, hy_b_in, hy_conv_w, hy_conv_b, hy_f_w1, hy_f_b1, hy_f_freq1, hy_f_w2, hy_f_b2, hy_f_freq2, hy_f_w3, hy_f_bias, hy_w_out, hy_b_out, ffn_w_gate, ffn_w_up, ffn_conv, ffn_conv_b, ffn_w_down, final_norm):
    x = jnp.concatenate([x_prompt.reshape(TOK_CTX, D), x_sample.reshape(TOK_LAT, D)], axis=0)
    cvec = jnp.concatenate([c_ctx[None], c, jnp.zeros((8 - 1 - NLAT, D), F32)], axis=0)
    mod = _mod_all(cvec, mod_w, mod_b)
    new_delta, new_ret = [], []
    for l in range(DEPTH):
        m = [mod[l, :NGROUP, k * D:(k + 1) * D].reshape(NGROUP, 1, D) for k in range(6)]
        j = l // 2
        if l % 2 == 0:
            w_qkv, w_rest = _ab_w_in_cols(ab_w_in[j])
            qkv = _ab_qkv(x, norm1[l], m[0], m[1], w_qkv, ab_conv[j])
            proj = _nmm(x, norm1[l], m[0], m[1], w_rest, None, AB_N // 3, "ab_in")
            heads, sd, sr = _ab_mixer(qkv, proj, ab_a_log[j], ab_dt_bias[j], ab_ret_decay[j],
                                      state_delta[:, j], state_ret[:, j])
            new_delta.append(sd)
            new_ret.append(sr)
            x = _ab_out(*heads, proj, ab_norm_a[j], ab_norm_b[j], ab_w_out[j].astype(BF16), x, m[2])
        else:
            x0, p = _hy_in(x, norm1[l], m[0], m[1], hy_w_in[j].astype(BF16), hy_b_in[j], hy_conv_w[j], hy_conv_b[j])
            filt_w = (hy_f_w1[j], hy_f_b1[j], hy_f_freq1[j], hy_f_w2[j], hy_f_b2[j], hy_f_freq2[j], hy_f_w3[j])
            y = _hy_long_conv(p, x0, filt_w, hy_f_bias[j])
            x = _mmr([y], [hy_w_out[j].astype(BF16)], hy_b_out[j], x, m[2], None, "hy_out")
        w_gu = jnp.concatenate([ffn_w_gate[l], ffn_w_up[l]], axis=1).astype(BF16)
        act = _ffn_act(x, norm2[l], m[3], m[4], w_gu, ffn_conv[l], ffn_conv_b[l])
        x = _mmr([act], [ffn_w_down[l].astype(BF16)], None, x, m[5],
                 final_norm if l == DEPTH - 1 else None, "ffn_out")
    y_prompt = x[:TOK_CTX].reshape(NCTX, LCTX, D)
    y_sample = x[TOK_CTX:].reshape(NLAT, LLAT, D)
    return (y_prompt, y_sample, jnp.stack(new_delta, axis=1), jnp.stack(new_ret, axis=1))
```

```python
import functools
import math

import numpy as np
import jax
import jax.numpy as jnp
from jax import lax
from jax.experimental import pallas as pl
from jax.experimental.pallas import tpu as pltpu

F32, BF16 = jnp.float32, jnp.bfloat16
HI = lax.Precision.HIGHEST

D = 1024
NCTX, LCTX = 16, 256
NLAT, LLAT = 2, 4096
DEPTH = 4
TOK_CTX = NCTX * LCTX
TOK_LAT = NLAT * LLAT
NTOK = TOK_CTX + TOK_LAT
ROWBLK = 4096
NGROUP = NTOK // ROWBLK
GRID_W = 64
CHUNK = 64
NCH = NTOK // CHUNK
EPS = 1e-6
HEADS = 4
DK = 128
HW = HEADS * DK
LANE = 128
D_FF = 2816
HY_EMB = 33
HY_BANDS = 16
HY_FW = 64
HY_TARGET = 1e-2
HY_FAST_PCT = 0.3
HY_SLOW_PCT = 1.5
FFT_N2 = 128
VMEM_LIMIT = 52 * 1024 * 1024

CB_QA, CB_KA, CB_VA = 0, 4, 8
CB_ZA, CB_QB, CB_KB, CB_VB, CB_GB, CB_GATE = 0, 4, 8, 12, 16, 20
AB_N = 21 * LANE

CTX = dict(nseq=NCTX, L=LCTX, row0=0)
LAT = dict(nseq=NLAT, L=LLAT, row0=TOK_CTX)


def _params(*sem):
    return pltpu.CompilerParams(dimension_semantics=sem, vmem_limit_bytes=VMEM_LIMIT)


def _silu(x):
    return x * jax.nn.sigmoid(x)


def _dot(a, b, precision=None):
    return jnp.dot(a, b, preferred_element_type=F32, precision=precision)


def _dot_nt(a, b):
    return lax.dot_general(a, b, (((1,), (1,)), ((), ())), preferred_element_type=F32)


def _dot_tn(a, b):
    return lax.dot_general(a, b, (((0,), (0,)), ((), ())), preferred_element_type=F32)


def _pick_lane(x, lane):
    li = lax.broadcasted_iota(jnp.int32, x.shape, 1)
    return jnp.sum(jnp.where(li == lane, x, 0.0), axis=-1, keepdims=True)


def _mod_kernel(c_ref, w_ref, b_ref, o_ref):
    s = _silu(c_ref[...])
    o_ref[...] = _dot(s.astype(BF16), w_ref[...].astype(BF16)) + b_ref[...]


def _mod_all(cvec, mod_w, mod_b):
    tn = 1536
    return pl.pallas_call(
        _mod_kernel,
        grid=(DEPTH, 6 * D // tn),
        in_specs=[pl.BlockSpec((8, D), lambda l, j: (0, 0)),
                  pl.BlockSpec((None, D, tn), lambda l, j: (l, 0, j)),
                  pl.BlockSpec((None, 1, tn), lambda l, j: (l, 0, j))],
        out_specs=pl.BlockSpec((None, 8, tn), lambda l, j: (l, 0, j)),
        out_shape=jax.ShapeDtypeStruct((DEPTH, 8, 6 * D), F32),
        compiler_params=_params("arbitrary", "arbitrary"),
        name="mod",
    )(cvec, mod_w, mod_b.reshape(DEPTH, 1, 6 * D))


NMM_TM = 1024


def _nmm_kernel(x_ref, g_ref, sh_ref, sc_ref, w_ref, *rest, has_bias):
    if has_bias:
        b_ref, o_ref, h_ref = rest
    else:
        o_ref, h_ref = rest

    @pl.when(pl.program_id(1) == 0)
    def _():
        x = x_ref[...]
        y = x * lax.rsqrt(jnp.mean(x * x, axis=-1, keepdims=True) + EPS) * g_ref[...]
        h_ref[...] = (y * (1.0 + sc_ref[...]) + sh_ref[...]).astype(BF16)

    acc = _dot(h_ref[...], w_ref[...])
    if has_bias:
        acc = acc + b_ref[...]
    o_ref[...] = acc


def _nmm(x, gain, shift, scale, w, bias, tn, name):
    n = w.shape[1]
    grp = lambda i, j: (i * NMM_TM // ROWBLK, 0, 0)
    in_specs = [pl.BlockSpec((NMM_TM, D), lambda i, j: (i, 0)),
                pl.BlockSpec((1, D), lambda i, j: (0, 0)),
                pl.BlockSpec((None, 1, D), grp),
                pl.BlockSpec((None, 1, D), grp),
                pl.BlockSpec((D, tn), lambda i, j: (0, j))]
    args = [x, gain.reshape(1, D), shift, scale, w]
    if bias is not None:
        in_specs.append(pl.BlockSpec((1, tn), lambda i, j: (0, j)))
        args.append(bias.reshape(1, n))
    return pl.pallas_call(
        functools.partial(_nmm_kernel, has_bias=bias is not None),
        grid=(NTOK // NMM_TM, n // tn),
        in_specs=in_specs,
        out_specs=pl.BlockSpec((NMM_TM, tn), lambda i, j: (i, j)),
        out_shape=jax.ShapeDtypeStruct((NTOK, n), F32),
        scratch_shapes=[pltpu.VMEM((NMM_TM, D), BF16)],
        compiler_params=_params("arbitrary", "arbitrary"),
        name=name,
    )(*args)


MMR_TM = 512


def _mmr_kernel(a_ref, w_ref, *rest, has_bias, final):
    rest = list(rest)
    b_ref = rest.pop(0) if has_bias else None
    x_ref, gate_ref = rest.pop(0), rest.pop(0)
    fn_ref = rest.pop(0) if final else None
    o_ref, wb_ref = rest

    @pl.when(pl.program_id(0) == 0)
    def _():
        wb_ref[...] = w_ref[...].astype(BF16)

    acc = _dot(a_ref[...].astype(BF16), wb_ref[...])
    if has_bias:
        acc = acc + b_ref[...]
    y = x_ref[...] + gate_ref[...] * acc
    if final:
        y = y * lax.rsqrt(jnp.mean(y * y, axis=-1, keepdims=True) + EPS) * fn_ref[...]
    o_ref[...] = y


def _mmr(a, w, bias, x, gate, final_gain, name):
    grp = lambda i: (i * MMR_TM // ROWBLK, 0, 0)
    in_specs = [pl.BlockSpec((MMR_TM, a.shape[1]), lambda i: (i, 0)), pl.BlockSpec(w.shape, lambda i: (0, 0))]
    args = [a, w]
    if bias is not None:
        in_specs.append(pl.BlockSpec((1, D), lambda i: (0, 0)))
        args.append(bias.reshape(1, D))
    in_specs += [pl.BlockSpec((MMR_TM, D), lambda i: (i, 0)), pl.BlockSpec((None, 1, D), grp)]
    args += [x, gate]
    if final_gain is not None:
        in_specs.append(pl.BlockSpec((1, D), lambda i: (0, 0)))
        args.append(final_gain.reshape(1, D))
    return pl.pallas_call(
        functools.partial(_mmr_kernel, has_bias=bias is not None, final=final_gain is not None),
        grid=(NTOK // MMR_TM,),
        in_specs=in_specs,
        out_specs=pl.BlockSpec((MMR_TM, D), lambda i: (i, 0)),
        out_shape=jax.ShapeDtypeStruct((NTOK, D), F32),
        scratch_shapes=[pltpu.VMEM(w.shape, BF16)],
        compiler_params=_params("arbitrary"),
        name=name,
    )(*args)


def _seq_shifts(x, i):
    lseq = jnp.where(i == 0, LCTX, LLAT)
    pos = lax.broadcasted_iota(jnp.int32, x.shape, 0) & (lseq - 1)
    prev = jnp.where(pos == 0, 0.0, pltpu.roll(x, 1, 0))
    nxt = jnp.where(pos == lseq - 1, 0.0, pltpu.roll(x, ROWBLK - 1, 0))
    return prev, nxt


def _conv3(x, w, i):
    prev, nxt = _seq_shifts(x, i)
    return prev * w[0:1] + x * w[1:2] + nxt * w[2:3]


def _ab_conv_kernel(x_ref, w_ref, o_ref):
    i, j = pl.program_id(0), pl.program_id(1)
    y = _silu(_conv3(x_ref[...], w_ref[...], i))
    nrm = y * lax.rsqrt(jnp.sum(y * y, axis=-1, keepdims=True) + EPS)
    nrm = nrm * jnp.where(j < CB_KA, DK ** -0.5, 1.0)
    o_ref[...] = jnp.where(j < CB_VA, nrm, y)


def _ab_conv(proj, conv_w):
    ncb = 3 * HEADS
    return pl.pallas_call(
        _ab_conv_kernel,
        grid=(NGROUP, ncb),
        in_specs=[pl.BlockSpec((ROWBLK, LANE), lambda i, j: (i, j)),
                  pl.BlockSpec((3, LANE), lambda i, j: (0, j))],
        out_specs=pl.BlockSpec((ROWBLK, LANE), lambda i, j: (i, j)),
        out_shape=jax.ShapeDtypeStruct((NTOK, ncb * LANE), F32),
        compiler_params=_params("arbitrary", "arbitrary"),
        name="ab_conv",
    )(proj, conv_w)


PC_TM = 1024
PC_HALO = 16
PC_TN = 256


def _pc_norm_rows(i, xc_ref, xp_ref, xn_ref, g_ref, sh_ref, sc_ref, h_ref):
    def norm(x):
        y = x * lax.rsqrt(jnp.mean(x * x, axis=-1, keepdims=True) + EPS) * g_ref[...]
        return (y * (1.0 + sc_ref[...]) + sh_ref[...]).astype(BF16)
    h_ref[0:PC_HALO, :] = norm(xp_ref[...])
    h_ref[PC_HALO:PC_HALO + PC_TM, :] = norm(xc_ref[...])
    h_ref[PC_HALO + PC_TM:PC_TM + 2 * PC_HALO, :] = norm(xn_ref[...])


def _pc_proj_conv(i, h_ref, w_ref, b_ref, cw_ref):
    ext = PC_TM + 2 * PC_HALO
    u = _dot(h_ref[...], w_ref[...])
    if b_ref is not None:
        u = u + b_ref[...]
    lseq = jnp.where(i < TOK_CTX // PC_TM, LCTX, LLAT)
    pos = (i * PC_TM - PC_HALO + lax.broadcasted_iota(jnp.int32, (ext, 1), 0)) & (lseq - 1)
    prev = jnp.where(pos == 0, 0.0, pltpu.roll(u, 1, 0))
    nxt = jnp.where(pos == lseq - 1, 0.0, pltpu.roll(u, ext - 1, 0))
    cw = cw_ref[...]
    y = prev * cw[0:1] + u * cw[1:2] + nxt * cw[2:3]
    return y[PC_HALO:PC_HALO + PC_TM]


def _pc_specs(n_extra):
    hb = PC_TM // PC_HALO
    nhb = NTOK // PC_HALO
    vec = pl.BlockSpec((None, 1, D), lambda i, j: (i * PC_TM // ROWBLK, 0, 0))
    return [pl.BlockSpec((PC_TM, D), lambda i, j: (i, 0)),
            pl.BlockSpec((PC_HALO, D), lambda i, j: (jnp.maximum(i * hb - 1, 0), 0)),
            pl.BlockSpec((PC_HALO, D), lambda i, j: (jnp.minimum((i + 1) * hb, nhb - 1), 0)),
            pl.BlockSpec((1, D), lambda i, j: (0, 0)), vec, vec]


def _hy_in_kernel(xc_ref, xp_ref, xn_ref, g_ref, sh_ref, sc_ref, w0_ref, w1_ref, wv_ref, b0_ref, b1_ref, bv_ref,
                  c0_ref, c1_ref, cv_ref, cb0_ref, cb1_ref, cbv_ref, x0_ref, p_ref, h_ref):
    i = pl.program_id(0)

    @pl.when(pl.program_id(1) == 0)
    def _():
        _pc_norm_rows(i, xc_ref, xp_ref, xn_ref, g_ref, sh_ref, sc_ref, h_ref)

    x0_ref[...] = _pc_proj_conv(i, h_ref, w0_ref, b0_ref, c0_ref) + cb0_ref[...]
    x1 = _pc_proj_conv(i, h_ref, w1_ref, b1_ref, c1_ref) + cb1_ref[...]
    v = _pc_proj_conv(i, h_ref, wv_ref, bv_ref, cv_ref) + cbv_ref[...]
    p_ref[...] = v * x1


def _hy_in(x, gain, shift, scale, w, b, conv_w, conv_b):
    nj = D // PC_TN
    wsp = lambda k: pl.BlockSpec((D, PC_TN), lambda i, j: (0, k * nj + j))
    row = lambda r, k: pl.BlockSpec((r, PC_TN), lambda i, j: (0, k * nj + j))
    out = pl.BlockSpec((PC_TM, PC_TN), lambda i, j: (i, j))
    b2, cb2 = b.reshape(1, 3 * D), conv_b.reshape(1, 3 * D)
    return pl.pallas_call(
        _hy_in_kernel,
        grid=(NTOK // PC_TM, nj),
        in_specs=_pc_specs(0) + [wsp(0), wsp(1), wsp(2), row(1, 0), row(1, 1), row(1, 2),
                                 row(3, 0), row(3, 1), row(3, 2), row(1, 0), row(1, 1), row(1, 2)],
        out_specs=[out, out],
        out_shape=[jax.ShapeDtypeStruct((NTOK, D), F32)] * 2,
        scratch_shapes=[pltpu.VMEM((PC_TM + 2 * PC_HALO, D), BF16)],
        compiler_params=_params("arbitrary", "arbitrary"),
        name="hy_in",
    )(x, x, x, gain.reshape(1, D), shift, scale, w, w, w, b2, b2, b2, conv_w, conv_w, conv_w, cb2, cb2, cb2)


def _ab_qkv_kernel(xc_ref, xp_ref, xn_ref, g_ref, sh_ref, sc_ref, w_ref, cw_ref, o_ref, h_ref):
    i, j = pl.program_id(0), pl.program_id(1)

    @pl.when(j == 0)
    def _():
        _pc_norm_rows(i, xc_ref, xp_ref, xn_ref, g_ref, sh_ref, sc_ref, h_ref)

    y = _silu(_pc_proj_conv(i, h_ref, w_ref, None, cw_ref))
    is_qk = j < 2 * HW // PC_TN
    qscale = jnp.where(j < HW // PC_TN, DK ** -0.5, 1.0)
    for blk in range(PC_TN // DK):
        cols = slice(blk * DK, (blk + 1) * DK)
        yb = y[:, cols]
        nrm = yb * (lax.rsqrt(jnp.sum(yb * yb, axis=-1, keepdims=True) + EPS) * qscale)
        o_ref[:, cols] = jnp.where(is_qk, nrm, yb)


def _ab_qkv(x, gain, shift, scale, w, conv_w):
    n = 3 * HW
    return pl.pallas_call(
        _ab_qkv_kernel,
        grid=(NTOK // PC_TM, n // PC_TN),
        in_specs=_pc_specs(0) + [pl.BlockSpec((D, PC_TN), lambda i, j: (0, j)),
                                 pl.BlockSpec((3, PC_TN), lambda i, j: (0, j))],
        out_specs=pl.BlockSpec((PC_TM, PC_TN), lambda i, j: (i, j)),
        out_shape=jax.ShapeDtypeStruct((NTOK, n), F32),
        scratch_shapes=[pltpu.VMEM((PC_TM + 2 * PC_HALO, D), BF16)],
        compiler_params=_params("arbitrary", "arbitrary"),
        name="ab_qkv",
    )(x, x, x, gain.reshape(1, D), shift, scale, w, conv_w)


def _gates_kernel(x_ref, alog_ref, dtb_ref, o_ref):
    x = x_ref[...]
    lane = lax.broadcasted_iota(jnp.int32, x.shape, 1)
    row = lax.broadcasted_iota(jnp.int32, x.shape, 0) & (CHUNK - 1)
    t = x + dtb_ref[...]
    softplus = jnp.maximum(t, 0.0) + jnp.log(1.0 + jnp.exp(-jnp.abs(t)))
    g = -jnp.exp(alog_ref[...]) * softplus
    pre, suf = g, g
    s = 1
    while s < CHUNK:
        pre = pre + jnp.where(row >= s, pltpu.roll(pre, s, 0), 0.0)
        suf = suf + jnp.where(row < CHUNK - s, pltpu.roll(suf, ROWBLK - s, 0), 0.0)
        s *= 2
    gc = jnp.where(lane < HEADS, pre, suf)
    o_ref[...] = jnp.where(lane < 2 * HEADS, gc, jax.nn.sigmoid(x))


def _gates(proj, a_log, dt_bias):
    pad = lambda v: jnp.pad(v.reshape(1, 2 * HEADS), ((0, 0), (0, LANE - 2 * HEADS)))
    return pl.pallas_call(
        _gates_kernel,
        grid=(NGROUP,),
        in_specs=[pl.BlockSpec((ROWBLK, LANE), lambda i: (i, CB_GATE)),
                  pl.BlockSpec((1, LANE), lambda i: (0, 0)),
                  pl.BlockSpec((1, LANE), lambda i: (0, 0))],
        out_specs=pl.BlockSpec((ROWBLK, LANE), lambda i: (i, 0)),
        out_shape=jax.ShapeDtypeStruct((NTOK, LANE), F32),
        compiler_params=_params("arbitrary"),
        name="ab_gates",
    )(proj, pad(a_log), pad(dt_bias))


DP_RB = 512
DP_NC = DP_RB // CHUNK
TRI_BASE = 8


def _split(x):
    hi = x.astype(BF16)
    return hi, (x - hi.astype(F32)).astype(BF16)


def _dot3(a, b):
    return _dot(a[0], b[0]) + _dot(a[0], b[1]) + _dot(a[1], b[0])


def _unit_tri_inv_batch(lms):
    ii = lax.broadcasted_iota(jnp.int32, (CHUNK, CHUNK), 0)
    jj = lax.broadcasted_iota(jnp.int32, (CHUNK, CHUNK), 1)
    same = lambda b: (ii >> int(math.log2(b))) == (jj >> int(math.log2(b)))
    eye = jnp.where(ii == jj, 1.0, 0.0)
    qs = [jnp.where(same(TRI_BASE), -lm, 0.0) for lm in lms]
    ps = [eye + q for q in qs]
    qs = [_split(q) for q in qs]
    for _ in range(int(math.log2(TRI_BASE)) - 1):
        qs = [_split(_dot3(q, q)) for q in qs]
        ps = [p + _dot3(_split(p), q) for p, q in zip(ps, qs)]
    b = TRI_BASE
    while b < CHUNK:
        off = jnp.logical_and(same(2 * b), jnp.logical_not(same(b)))
        pbs = [p.astype(BF16) for p in ps]
        ts = [_dot(p, jnp.where(off, lm, 0.0).astype(BF16)) for p, lm in zip(pbs, lms)]
        ps = [p - _dot(t.astype(BF16), p2) for p, p2, t in zip(ps, pbs, ts)]
        b *= 2
    return ps


def _delta_prep_kernel(q_ref, k_ref, v_ref, g_ref, gt_ref, u_ref, wq_ref, ak_ref, gl_ref):
    h = pl.program_id(1)
    ii = lax.broadcasted_iota(jnp.int32, (CHUNK, CHUNK), 0)
    jj = lax.broadcasted_iota(jnp.int32, (CHUNK, CHUNK), 1)
    probs = []
    for c in range(DP_NC):
        rows = slice(c * CHUNK, (c + 1) * CHUNK)
        q, k, v, gts = q_ref[rows, :], k_ref[rows, :], v_ref[rows, :], g_ref[rows, :]
        kbf = k.astype(BF16)
        qk = _dot_nt(q.astype(BF16), kbf)
        for d in range(2):
            incl = (ii >= jj) if d == 0 else (ii <= jj)
            strict = (ii > jj) if d == 0 else (ii < jj)
            gcol = _pick_lane(gts, d * HEADS + h)
            bcol = _pick_lane(gts, 2 * HEADS + d * HEADS + h)
            grow = gt_ref[d * HEADS + h][c:c + 1, :]
            dmask = jnp.where(incl, jnp.exp(jnp.where(incl, gcol - grow, 0.0)), 0.0)
            kb = k * bcol
            lm = jnp.where(strict, _dot_nt(kb.astype(BF16), kbf) * dmask, 0.0)
            probs.append((c, d, rows, q, k, v, kb, gcol, bcol, dmask, qk, incl, lm))
    tmats = _unit_tri_inv_batch([p[-1] for p in probs])
    for (c, d, rows, q, k, v, kb, gcol, bcol, dmask, qk, incl, _), tmat in zip(probs, tmats):
        gam = jnp.exp(gcol)
        rhs = jnp.concatenate([v * bcol, kb * gam], axis=1).astype(BF16)
        uw = _dot(tmat.astype(BF16), rhs)
        attn = jnp.where(incl, qk * dmask, 0.0)
        gtot = gcol[CHUNK - 1:CHUNK, :] if d == 0 else gcol[0:1, :]
        kd = k * jnp.exp(gtot - gcol)
        u_ref[d, rows, :] = uw[:, :DK]
        wq_ref[d, c] = jnp.concatenate([uw[:, DK:], q * gam], axis=0).astype(BF16)
        ak_ref[d, c] = jnp.concatenate([attn, kd.T], axis=0).astype(BF16)
        gl_ref[d, c] = jnp.broadcast_to(jnp.exp(gtot), (1, LANE))


def _delta_prep(qkv, gates, gates_t):
    nrb = NTOK // DP_RB
    blk = lambda cb: pl.BlockSpec((DP_RB, LANE), lambda i, h: (i, cb + h))
    return pl.pallas_call(
        _delta_prep_kernel,
        grid=(nrb, HEADS),
        in_specs=[blk(CB_QA), blk(CB_KA), blk(CB_VA),
                  pl.BlockSpec((DP_RB, LANE), lambda i, h: (i, 0)),
                  pl.BlockSpec((None, 2 * HEADS, DP_NC, CHUNK), lambda i, h: (i, 0, 0, 0))],
        out_specs=[pl.BlockSpec((2, DP_RB, LANE), lambda i, h: (0, i, h)),
                   pl.BlockSpec((2, None, DP_NC, 2 * CHUNK, DK), lambda i, h: (0, h, i, 0, 0)),
                   pl.BlockSpec((2, None, DP_NC, CHUNK + DK, CHUNK), lambda i, h: (0, h, i, 0, 0)),
                   pl.BlockSpec((2, None, DP_NC, 1, LANE), lambda i, h: (0, h, i, 0, 0))],
        out_shape=[jax.ShapeDtypeStruct((2, NTOK, HW), F32),
                   jax.ShapeDtypeStruct((2, HEADS, NCH, 2 * CHUNK, DK), BF16),
                   jax.ShapeDtypeStruct((2, HEADS, NCH, CHUNK + DK, CHUNK), BF16),
                   jax.ShapeDtypeStruct((2, HEADS, NCH, 1, LANE), F32)],
        compiler_params=_params("arbitrary", "arbitrary"),
        name="delta_prep",
    )(qkv, qkv, qkv, gates, gates_t)


def _delta_scan_kernel(*refs, L, zero_init, aliased):
    refs = list(refs)
    u_ref, wq_ref, ak_ref, gl_ref, z_ref, ng_ref = [refs.pop(0) for _ in range(6)]
    s0_ref = None if zero_init else refs.pop(0)
    if aliased:
        refs.pop(0)
    o_ref, sfin_ref, o_scr = refs
    n = L // CHUNK

    def body(c, carry):
        new = []
        for d in range(2):
            s = carry[d]
            cc = c if d == 0 else n - 1 - c
            r0 = pl.multiple_of(cc * CHUNK, CHUNK)
            r1 = _dot(wq_ref[d, cc], s.astype(BF16))
            vnew = u_ref[d, pl.ds(r0, CHUNK), :] - r1[:CHUNK]
            r2 = _dot(ak_ref[d, cc], vnew.astype(BF16))
            o_scr[d, pl.ds(r0, CHUNK), :] = r1[CHUNK:] + r2[:CHUNK]
            new.append(s * gl_ref[d, cc] + r2[CHUNK:])
        return tuple(new)

    if zero_init:
        init = (jnp.zeros((DK, DK), F32), jnp.zeros((DK, DK), F32))
    else:
        init = (s0_ref[0], s0_ref[1])
    fin = lax.fori_loop(0, n, body, init)
    sfin_ref[0] = fin[0]
    sfin_ref[1] = fin[1]
    o = o_scr[0] + o_scr[1]
    o = o * lax.rsqrt(jnp.mean(o * o, axis=-1, keepdims=True) + EPS) * ng_ref[...]
    o_ref[...] = o * _silu(z_ref[...])


def _delta_scan(stream, u, wq, ak, gl, proj, norm_gain, s0, prev_out):
    nseq, L = stream["nseq"], stream["L"]
    rb0 = stream["row0"] // L
    n = L // CHUNK
    in_specs = [pl.BlockSpec((2, L, LANE), lambda s, h: (0, rb0 + s, h)),
                pl.BlockSpec((2, None, n, 2 * CHUNK, DK), lambda s, h: (0, h, rb0 + s, 0, 0)),
                pl.BlockSpec((2, None, n, CHUNK + DK, CHUNK), lambda s, h: (0, h, rb0 + s, 0, 0)),
                pl.BlockSpec((2, None, n, 1, LANE), lambda s, h: (0, h, rb0 + s, 0, 0)),
                pl.BlockSpec((L, LANE), lambda s, h: (rb0 + s, CB_ZA + h)),
                pl.BlockSpec((1, LANE), lambda s, h: (0, 0))]
    args = [u, wq, ak, gl, proj, norm_gain.reshape(1, DK)]
    if s0 is not None:
        in_specs.append(pl.BlockSpec((None, 2, None, DK, DK), lambda s, h: (s, 0, h, 0, 0)))
        args.append(s0)
    aliases = {}
    if prev_out is not None:
        in_specs.append(pl.BlockSpec(memory_space=pl.ANY))
        aliases = {len(args): 0}
        args.append(prev_out)
    return pl.pallas_call(
        functools.partial(_delta_scan_kernel, L=L, zero_init=s0 is None, aliased=prev_out is not None),
        grid=(nseq, HEADS),
        in_specs=in_specs,
        out_specs=[pl.BlockSpec((L, LANE), lambda s, h: (rb0 + s, h)),
                   pl.BlockSpec((None, 2, None, DK, DK), lambda s, h: (s, 0, h, 0, 0))],
        out_shape=[jax.ShapeDtypeStruct((NTOK, HW), F32),
                   jax.ShapeDtypeStruct((nseq, 2, HEADS, DK, DK), F32)],
        scratch_shapes=[pltpu.VMEM((2, L, LANE), F32)],
        input_output_aliases=aliases,
        compiler_params=_params("arbitrary", "arbitrary"),
        name="delta_scan_%d" % L,
    )(*args)


def _ret_scan_kernel(*refs, L, zero_init, aliased):
    refs = list(refs)
    q_ref, k_ref, v_ref, gb_ref, dec_ref, ng_ref = [refs.pop(0) for _ in range(6)]
    r0_ref = None if zero_init else refs.pop(0)
    if aliased:
        refs.pop(0)
    o_ref, rfin_ref, o_scr = refs
    h = pl.program_id(1)
    n = L // CHUNK
    ii = lax.broadcasted_iota(jnp.int32, (CHUNK, CHUNK), 0)
    jj = lax.broadcasted_iota(jnp.int32, (CHUNK, CHUNK), 1)
    ci = lax.broadcasted_iota(jnp.int32, (CHUNK, 1), 0).astype(F32)
    consts = []
    for d in range(2):
        lg = -jnp.exp(_pick_lane(dec_ref[...], d * HEADS + h))
        dist = (ii - jj) if d == 0 else (jj - ii)
        dmat = jnp.where(dist >= 0, jnp.exp(jnp.maximum(dist, 0).astype(F32) * lg), 0.0)
        cross = jnp.exp(((ci + 1.0) if d == 0 else (CHUNK - ci)) * lg)
        sdec = jnp.exp(((CHUNK - 1.0 - ci) if d == 0 else ci) * lg)
        consts.append((dmat, cross, sdec, jnp.exp(CHUNK * lg)))

    def body(c, carry):
        new = []
        for d in range(2):
            dmat, cross, sdec, cdec = consts[d]
            r = carry[d]
            cc = c if d == 0 else n - 1 - c
            rows = pl.ds(pl.multiple_of(cc * CHUNK, CHUNK), CHUNK)
            q = q_ref[rows, :] * DK ** -0.5
            k, v = k_ref[rows, :], v_ref[rows, :]
            vb = v.astype(BF16)
            inner = _dot_nt(q.astype(BF16), k.astype(BF16)) * dmat
            o_scr[d, rows, :] = _dot(inner.astype(BF16), vb) + _dot((q * cross).astype(BF16), r.astype(BF16))
            new.append(r * cdec + _dot_tn((k * sdec).astype(BF16), vb))
        return tuple(new)

    if zero_init:
        init = (jnp.zeros((DK, DK), F32), jnp.zeros((DK, DK), F32))
    else:
        init = (r0_ref[0], r0_ref[1])
    fin = lax.fori_loop(0, n, body, init)
    rfin_ref[0] = fin[0]
    rfin_ref[1] = fin[1]
    o = o_scr[0] + o_scr[1]
    mu = jnp.mean(o, axis=-1, keepdims=True)
    var = jnp.mean(jnp.square(o - mu), axis=-1, keepdims=True)
    o = (o - mu) * lax.rsqrt(var + EPS) * ng_ref[...]
    o_ref[...] = o * _silu(gb_ref[...])


def _ret_scan(stream, proj, ret_decay, norm_gain, r0, prev_out):
    nseq, L = stream["nseq"], stream["L"]
    rb0 = stream["row0"] // L
    blk = lambda cb: pl.BlockSpec((L, LANE), lambda s, h: (rb0 + s, cb + h))
    in_specs = [blk(CB_QB), blk(CB_KB), blk(CB_VB), blk(CB_GB),
                pl.BlockSpec((1, LANE), lambda s, h: (0, 0)),
                pl.BlockSpec((1, LANE), lambda s, h: (0, 0))]
    dec = jnp.pad(ret_decay.reshape(1, 2 * HEADS), ((0, 0), (0, LANE - 2 * HEADS)))
    args = [proj, proj, proj, proj, dec, norm_gain.reshape(1, DK)]
    if r0 is not None:
        in_specs.append(pl.BlockSpec((None, 2, None, DK, DK), lambda s, h: (s, 0, h, 0, 0)))
        args.append(r0)
    aliases = {}
    if prev_out is not None:
        in_specs.append(pl.BlockSpec(memory_space=pl.ANY))
        aliases = {len(args): 0}
        args.append(prev_out)
    return pl.pallas_call(
        functools.partial(_ret_scan_kernel, L=L, zero_init=r0 is None, aliased=prev_out is not None),
        grid=(nseq, HEADS),
        in_specs=in_specs,
        out_specs=[pl.BlockSpec((L, LANE), lambda s, h: (rb0 + s, h)),
                   pl.BlockSpec((None, 2, None, DK, DK), lambda s, h: (s, 0, h, 0, 0))],
        out_shape=[jax.ShapeDtypeStruct((NTOK, HW), F32),
                   jax.ShapeDtypeStruct((nseq, 2, HEADS, DK, DK), F32)],
        scratch_shapes=[pltpu.VMEM((2, L, LANE), F32)],
        input_output_aliases=aliases,
        compiler_params=_params("arbitrary", "arbitrary"),
        name="ret_scan_%d" % L,
    )(*args)


SCAN_G = LCTX // CHUNK
SCAN_RB = SCAN_G * CHUNK
SCAN_CTX_STEPS = TOK_CTX // SCAN_RB
SCAN_LAT_STEPS = LLAT // SCAN_RB
assert SCAN_RB == LCTX and SCAN_CTX_STEPS % SCAN_LAT_STEPS == 0


def _scan_bwd_group(t):
    i = t % SCAN_LAT_STEPS
    return jnp.where(t < SCAN_CTX_STEPS, t, t - i + (SCAN_LAT_STEPS - 1 - i))


def _scan_seq(t):
    return jnp.where(t < SCAN_CTX_STEPS, t, NCTX + (t - SCAN_CTX_STEPS) // SCAN_LAT_STEPS)


def _ab_scan_kernel(uf_ref, ub_ref, wqf_ref, wqb_ref, akf_ref, akb_ref, glf_ref, glb_ref,
                    qf_ref, kf_ref, vf_ref, qb_ref, kb_ref, vb_ref, dec_ref, s0_ref, r0_ref,
                    odf_ref, odb_ref, orf_ref, orb_ref, sfin_ref, rfin_ref, s_scr, r_scr):
    t = pl.program_id(0)
    in_ctx = t < SCAN_CTX_STEPS
    first = jnp.logical_or(in_ctx, t % SCAN_LAT_STEPS == 0)
    last = jnp.logical_or(in_ctx, t % SCAN_LAT_STEPS == SCAN_LAT_STEPS - 1)

    @pl.when(first)
    def _():
        s_scr[...] = jnp.where(in_ctx, 0.0, s0_ref[...])
        r_scr[...] = jnp.where(in_ctx, 0.0, r0_ref[...])

    dirs = ((uf_ref, wqf_ref, akf_ref, glf_ref, qf_ref, kf_ref, vf_ref, odf_ref, orf_ref),
            (ub_ref, wqb_ref, akb_ref, glb_ref, qb_ref, kb_ref, vb_ref, odb_ref, orb_ref))

    ii = lax.broadcasted_iota(jnp.int32, (SCAN_RB, SCAN_RB), 0)
    jj = lax.broadcasted_iota(jnp.int32, (SCAN_RB, SCAN_RB), 1)
    ci = lax.broadcasted_iota(jnp.int32, (SCAN_RB, 1), 0).astype(F32)
    for d in range(2):
        q_ref, k_ref, v_ref, or_ref = dirs[d][4], dirs[d][5], dirs[d][6], dirs[d][8]
        dist = (ii - jj) if d == 0 else (jj - ii)
        for h in range(HEADS):
            cols = slice(h * DK, (h + 1) * DK)
            lane = d * HEADS + h
            lg = -jnp.exp(dec_ref[:, lane:lane + 1])
            dmat = jnp.where(dist >= 0, jnp.exp(jnp.maximum(dist, 0).astype(F32) * lg), 0.0)
            cross = jnp.exp(((ci + 1.0) if d == 0 else (SCAN_RB - ci)) * lg)
            sdec = jnp.exp(((SCAN_RB - 1.0 - ci) if d == 0 else ci) * lg)
            r = r_scr[d, h]
            q = q_ref[:, cols] * DK ** -0.5
            k = k_ref[:, cols]
            vb = v_ref[:, cols].astype(BF16)
            inner = _dot_nt(q.astype(BF16), k.astype(BF16)) * dmat
            or_ref[:, cols] = _dot(inner.astype(BF16), vb) + _dot((q * cross).astype(BF16), r.astype(BF16))
            r_scr[d, h] = r * jnp.exp(SCAN_RB * lg) + _dot_tn((k * sdec).astype(BF16), vb)

    s_cur = {(d, h): s_scr[d, h] for d in range(2) for h in range(HEADS)}
    for c in range(SCAN_G):
        for d in range(2):
            u_ref, wq_ref, ak_ref, gl_ref, q_ref, k_ref, v_ref, od_ref, or_ref = dirs[d]
            cc = c if d == 0 else SCAN_G - 1 - c
            rows = slice(cc * CHUNK, (cc + 1) * CHUNK)
            for h in range(HEADS):
                cols = slice(h * DK, (h + 1) * DK)
                s = s_cur[d, h]
                r1 = _dot(wq_ref[h, cc], s.astype(BF16))
                vnew = u_ref[rows, cols] - r1[:CHUNK]
                r2 = _dot(ak_ref[h, cc], vnew.astype(BF16))
                od_ref[rows, cols] = r1[CHUNK:] + r2[:CHUNK]
                s_cur[d, h] = s * gl_ref[h, cc] + r2[CHUNK:]
    for d in range(2):
        for h in range(HEADS):
            s_scr[d, h] = s_cur[d, h]

    @pl.when(last)
    def _():
        sfin_ref[...] = s_scr[...]
        rfin_ref[...] = r_scr[...]


def _ab_scan(u, wq, ak, gl, proj, ret_decay, s0, r0):
    nseq = NCTX + NLAT
    fwd = lambda t: t
    row = lambda g, cb: pl.BlockSpec((SCAN_RB, HW), lambda t: (g(t), cb))
    u_spec = lambda d, g: pl.BlockSpec((None, SCAN_RB, HW), lambda t: (d, g(t), 0))
    op_spec = lambda d, g, a, b: pl.BlockSpec((None, HEADS, SCAN_G, a, b), lambda t: (d, 0, g(t), 0, 0))
    state = pl.BlockSpec((None, 2, HEADS, DK, DK), lambda t: (_scan_seq(t), 0, 0, 0, 0))
    state0 = pl.BlockSpec((None, 2, HEADS, DK, DK), lambda t: (jnp.maximum(_scan_seq(t) - NCTX, 0), 0, 0, 0, 0))
    in_specs = [u_spec(0, fwd), u_spec(1, _scan_bwd_group),
                op_spec(0, fwd, 2 * CHUNK, DK), op_spec(1, _scan_bwd_group, 2 * CHUNK, DK),
                op_spec(0, fwd, CHUNK + DK, CHUNK), op_spec(1, _scan_bwd_group, CHUNK + DK, CHUNK),
                op_spec(0, fwd, 1, LANE), op_spec(1, _scan_bwd_group, 1, LANE),
                row(fwd, CB_QB // HEADS), row(fwd, CB_KB // HEADS), row(fwd, CB_VB // HEADS),
                row(_scan_bwd_group, CB_QB // HEADS), row(_scan_bwd_group, CB_KB // HEADS),
                row(_scan_bwd_group, CB_VB // HEADS),
                pl.BlockSpec((1, LANE), lambda t: (0, 0)), state0, state0]
    dec = jnp.pad(ret_decay.reshape(1, 2 * HEADS), ((0, 0), (0, LANE - 2 * HEADS)))
    o_sds = jax.ShapeDtypeStruct((NTOK, HW), F32)
    st_sds = jax.ShapeDtypeStruct((nseq, 2, HEADS, DK, DK), F32)
    return pl.pallas_call(
        _ab_scan_kernel,
        grid=(NTOK // SCAN_RB,),
        in_specs=in_specs,
        out_specs=[row(fwd, 0), row(_scan_bwd_group, 0), row(fwd, 0), row(_scan_bwd_group, 0), state, state],
        out_shape=[o_sds, o_sds, o_sds, o_sds, st_sds, st_sds],
        scratch_shapes=[pltpu.VMEM((2, HEADS, DK, DK), F32), pltpu.VMEM((2, HEADS, DK, DK), F32)],
        compiler_params=_params("arbitrary"),
        name="ab_scan",
    )(u, u, wq, wq, ak, ak, gl, gl, proj, proj, proj, proj, proj, proj, dec, s0, r0)


def _ab_out_kernel(odf_ref, odb_ref, orf_ref, orb_ref, z_ref, gb_ref, na_ref, nb_ref, w_ref, x_ref, gate_ref,
                   o_ref, wb_ref):
    @pl.when(pl.program_id(0) == 0)
    def _():
        wb_ref[...] = w_ref[...].astype(BF16)

    oa = odf_ref[...] + odb_ref[...]
    ob = orf_ref[...] + orb_ref[...]
    z, gb = z_ref[...], gb_ref[...]
    parts = []
    for h in range(HEADS):
        cols = slice(h * DK, (h + 1) * DK)
        a = oa[:, cols]
        a = a * lax.rsqrt(jnp.mean(a * a, axis=-1, keepdims=True) + EPS) * na_ref[...]
        parts.append(a * _silu(z[:, cols]))
    for h in range(HEADS):
        cols = slice(h * DK, (h + 1) * DK)
        b = ob[:, cols]
        mu = jnp.mean(b, axis=-1, keepdims=True)
        var = jnp.mean(jnp.square(b - mu), axis=-1, keepdims=True)
        parts.append((b - mu) * lax.rsqrt(var + EPS) * nb_ref[...] * _silu(gb[:, cols]))
    a = jnp.concatenate(parts, axis=1).astype(BF16)
    o_ref[...] = x_ref[...] + gate_ref[...] * _dot(a, wb_ref[...])


def _ab_out(odf, odb, orf, orb, proj, norm_a, norm_b, w_out, x, gate):
    grp = lambda i: (i * MMR_TM // ROWBLK, 0, 0)
    tile = lambda cb: pl.BlockSpec((MMR_TM, HW), lambda i: (i, cb))
    vec = pl.BlockSpec((1, DK), lambda i: (0, 0))
    return pl.pallas_call(
        _ab_out_kernel,
        grid=(NTOK // MMR_TM,),
        in_specs=[tile(0), tile(0), tile(0), tile(0), tile(CB_ZA // HEADS), tile(CB_GB // HEADS), vec, vec,
                  pl.BlockSpec((2 * HW, D), lambda i: (0, 0)),
                  pl.BlockSpec((MMR_TM, D), lambda i: (i, 0)), pl.BlockSpec((None, 1, D), grp)],
        out_specs=pl.BlockSpec((MMR_TM, D), lambda i: (i, 0)),
        out_shape=jax.ShapeDtypeStruct((NTOK, D), F32),
        scratch_shapes=[pltpu.VMEM((2 * HW, D), BF16)],
        compiler_params=_params("arbitrary"),
        name="ab_out",
    )(odf, odb, orf, orb, proj, proj, norm_a.reshape(1, DK), norm_b.reshape(1, DK), w_out, x, gate)


def _hy_conv_kernel(x0_ref, x1_ref, v_ref, w0_ref, w1_ref, wv_ref, b0_ref, b1_ref, bv_ref, x0o_ref, p_ref):
    i = pl.program_id(0)
    x0o_ref[...] = _conv3(x0_ref[...], w0_ref[...], i) + b0_ref[...]
    x1 = _conv3(x1_ref[...], w1_ref[...], i) + b1_ref[...]
    v = _conv3(v_ref[...], wv_ref[...], i) + bv_ref[...]
    p_ref[...] = v * x1


def _hy_conv(u3, conv_w, conv_b):
    ncb = D // LANE
    xb = lambda k: pl.BlockSpec((ROWBLK, LANE), lambda i, j: (i, k * ncb + j))
    wb = lambda k: pl.BlockSpec((3, LANE), lambda i, j: (0, k * ncb + j))
    bb = lambda k: pl.BlockSpec((1, LANE), lambda i, j: (0, k * ncb + j))
    cb = conv_b.reshape(1, 3 * D)
    return pl.pallas_call(
        _hy_conv_kernel,
        grid=(NGROUP, ncb),
        in_specs=[xb(0), xb(1), xb(2), wb(0), wb(1), wb(2), bb(0), bb(1), bb(2)],
        out_specs=[pl.BlockSpec((ROWBLK, LANE), lambda i, j: (i, j))] * 2,
        out_shape=[jax.ShapeDtypeStruct((NTOK, D), F32)] * 2,
        compiler_params=_params("arbitrary", "arbitrary"),
        name="hy_conv",
    )(u3, u3, u3, conv_w, conv_w, conv_w, cb, cb, cb)


def _filter_features(L):
    r = np.arange(2 * L)
    pos = np.where(r < L, r, 2 * L - r) % L
    t = pos / (L - 1.0)
    bands = np.linspace(1e-4, HY_BANDS - 1, HY_BANDS)
    ang = 2.0 * np.pi * np.outer(pos, bands) / L
    z = np.zeros((2 * L, HY_FW), np.float64)
    z[:, 0] = t
    z[:, 1:1 + HY_BANDS] = np.cos(ang)
    z[:, 1 + HY_BANDS:HY_EMB] = -np.sin(ang)
    z[:, HY_EMB] = (r != L)
    return z.astype(np.float32)


def _filter_kernel(z_ref, w1_ref, b1_ref, f1_ref, w2_ref, b2_ref, f2_ref, w3_ref, dl_ref, o_ref):
    z = z_ref[...]
    hid = jnp.sin(f1_ref[...] * (_dot(z, w1_ref[...], HI) + b1_ref[...]))
    hid = jnp.sin(f2_ref[...] * (_dot(hid, w2_ref[...], HI) + b2_ref[...]))
    filt = _dot3(_split(hid), _split(w3_ref[...]))
    window =jnp.exp(-z[:, 0:1] * dl_ref[...]) * z[:, HY_EMB:HY_EMB + 1]
    o_ref[...] = filt * window


def _hy_filter(L, w1, b1, f1, w2, b2, f2, w3):
    rb = min(512, L)
    nblk = 2 * L // rb
    z = jnp.asarray(_filter_features(L))
    w1p = jnp.pad(w1, ((0, HY_FW - HY_EMB), (0, 0)))
    min_decay = math.log(HY_TARGET) / HY_SLOW_PCT
    max_decay = math.log(HY_TARGET) / HY_FAST_PCT
    deltas = jnp.asarray(np.abs(np.linspace(min_decay, max_decay, D)).astype(np.float32).reshape(1, D))
    vec = lambda v: v.reshape(1, HY_FW)
    full = lambda shp: pl.BlockSpec(shp, lambda i: (0, 0))
    return pl.pallas_call(
        _filter_kernel,
        grid=(nblk,),
        in_specs=[pl.BlockSpec((rb, HY_FW), lambda i: (i, 0)),
                  full((HY_FW, HY_FW)), full((1, HY_FW)), full((1, HY_FW)),
                  full((HY_FW, HY_FW)), full((1, HY_FW)), full((1, HY_FW)),
                  pl.BlockSpec((HY_FW, D), lambda i: (0, i // (nblk // 2))),
                  full((1, D))],
        out_specs=pl.BlockSpec((rb, D), lambda i: (i, 0)),
        out_shape=jax.ShapeDtypeStruct((2 * L, D), F32),
        compiler_params=_params("arbitrary"),
        name="hy_filter_%d" % L,
    )(z, w1p, vec(b1), vec(f1), w2, vec(b2), vec(f2), w3, deltas)


FFT_RT = 8


def _cis(num, den):
    ang = -2.0 * np.pi * (num % den) / den
    return np.cos(ang), np.sin(ang)


def _fft_consts(L):
    n = 2 * L
    n1 = n // FFT_N2
    k1 = np.arange(n1)
    fr, fi = _cis(np.outer(k1, k1), n1)
    half = n1 // 2
    sig = np.block([[fr[:, :half], -fi[:, :half]], [fi[:, :half], fr[:, :half]]])
    ker = np.concatenate([fr, fi], axis=0)
    cr, ci = fr[:, :half].T, -fi[:, :half].T
    inv = np.stack([np.concatenate([cr, -ci], axis=1), np.concatenate([ci, cr], axis=1)]) / n
    k2 = np.arange(FFT_N2)
    gr, gi = _cis(np.outer(k2, k2), FFT_N2)
    f2 = np.block([[gr, -gi], [gi, gr]])
    f2inv = np.block([[gr, gi], [-gi, gr]])
    tr, ti = _cis(np.outer(k1, k2), n)
    f = lambda a: np.asarray(a, np.float32)
    return dict(n1=n1, sig=f(sig), ker=f(ker), inv=f(inv), f2=f(f2), f2inv=f(f2inv),
                twr=f(tr).reshape(n1, FFT_N2, 1), twi=f(ti).reshape(n1, FFT_N2, 1))


def _lin_rows(mat, rows):
    out = []
    for m in range(mat.shape[0]):
        acc = None
        for k in range(mat.shape[1]):
            cf = float(mat[m, k])
            if abs(cf) < 1e-9:
                continue
            term = rows[k] if abs(cf - 1.0) < 1e-9 else (-rows[k] if abs(cf + 1.0) < 1e-9 else cf * rows[k])
            acc = term if acc is None else acc + term
        out.append(jnp.zeros_like(rows[0]) if acc is None else acc)
    return out


def _twiddle_dft(slab_r, slab_i, twr, twi, f2):
    x = jnp.concatenate([slab_r * twr - slab_i * twi, slab_r * twi + slab_i * twr], axis=0)
    y = _dot3(f2, _split(x))
    return y[:FFT_N2], y[FFT_N2:]


def _filter_idft_twiddle(yr, yi, kr, ki, twr, twi, f2inv):
    z = jnp.concatenate([yr * kr - yi * ki, yr * ki + yi * kr], axis=0)
    w = _dot3(f2inv, _split(z))
    wr, wi = w[:FFT_N2], w[FFT_N2:]
    return wr * twr + wi * twi, wi * twr - wr * twi


def _fft1_kernel(m_ref, *refs):
    o_ref = refs[-1]
    m = _split(m_ref[...])
    n1 = o_ref.shape[1]
    for r in range(FFT_RT):
        parts = [x_ref[c:c + 32, r, :] for x_ref in refs[:-1] for c in range(0, x_ref.shape[0], 32)]
        x = jnp.concatenate(parts, axis=0) if len(parts) > 1 else parts[0]
        y = _dot3(m, _split(x))
        o_ref[0, :, r, :] = y[:n1]
        o_ref[1, :, r, :] = y[n1:]


def _fft_stage1(consts, mat, xs, lead_blocks):
    n1 = consts["n1"]
    rows = mat.shape[1] // len(xs)
    specs = [pl.BlockSpec((rows, FFT_RT, D), functools.partial(lambda j, lb: (lb, j, 0), lb=lb))
             for lb in lead_blocks]
    return pl.pallas_call(
        _fft1_kernel,
        grid=(FFT_N2 // FFT_RT,),
        in_specs=[pl.BlockSpec(mat.shape, lambda j: (0, 0))] + specs,
        out_specs=pl.BlockSpec((2, n1, FFT_RT, D), lambda j: (0, 0, j, 0)),
        out_shape=jax.ShapeDtypeStruct((2, n1, FFT_N2, D), F32),
        compiler_params=_params("arbitrary"),
        name="fft1_%d" % len(xs),
    )(jnp.asarray(mat), *xs)


def _fft2_kernel(a_ref, twr_ref, twi_ref, f2_ref, *rest, conv):
    twr, twi = twr_ref[...], twi_ref[...]
    yr, yi = _twiddle_dft(a_ref[0], a_ref[1], twr, twi, _split(f2_ref[...]))
    if conv:
        kf_ref, f2inv_ref, o_ref = rest
        yr, yi = _filter_idft_twiddle(yr, yi, kf_ref[0], kf_ref[1], twr, twi, _split(f2inv_ref[...]))
    else:
        o_ref, = rest
    o_ref[0] = yr
    o_ref[1] = yi


def _fft_stage2(consts, a, kf):
    n1 = consts["n1"]
    slab = pl.BlockSpec((2, None, FFT_N2, D), lambda k: (0, k, 0, 0))
    tw = pl.BlockSpec((None, FFT_N2, 1), lambda k: (k, 0, 0))
    mat = pl.BlockSpec((2 * FFT_N2, 2 * FFT_N2), lambda k: (0, 0))
    in_specs = [slab, tw, tw, mat]
    args = [a, jnp.asarray(consts["twr"]), jnp.asarray(consts["twi"]), jnp.asarray(consts["f2"])]
    if kf is not None:
        in_specs += [slab, mat]
        args += [kf, jnp.asarray(consts["f2inv"])]
    return pl.pallas_call(
        functools.partial(_fft2_kernel, conv=kf is not None),
        grid=(n1,),
        in_specs=in_specs,
        out_specs=slab,
        out_shape=jax.ShapeDtypeStruct(a.shape, F32),
        compiler_params=_params("arbitrary"),
        name="fft2_%s" % ("conv" if kf is not None else "spec"),
    )(*args)


def _fft3_kernel(m_ref, b_ref, p_ref, x0_ref, bias_ref, *rest):
    o_ref = rest[-1]
    m = _split(m_ref[...])
    bias = bias_ref[...]
    for r in range(FFT_RT):
        x = jnp.concatenate([b_ref[0, :, r, :], b_ref[1, :, r, :]], axis=0)
        y = _dot3(m, _split(x))
        o_ref[:, r, :] = (y + p_ref[:, r, :] * bias) * x0_ref[:, r, :]


def _fft_stage3(consts, b, pv, xv, bias, lead_block, prev_out):
    n1 = consts["n1"]
    half = n1 // 2
    seq = pl.BlockSpec((half, FFT_RT, D), lambda j, s: (lead_block + s, j, 0))
    in_specs = [pl.BlockSpec((None, half, 2 * n1), lambda j, s: (s, 0, 0)),
                pl.BlockSpec((2, n1, FFT_RT, D), lambda j, s: (0, 0, j, 0)),
                seq, seq, pl.BlockSpec((1, D), lambda j, s: (0, 0)),
                pl.BlockSpec(memory_space=pl.ANY)]
    return pl.pallas_call(
        _fft3_kernel,
        grid=(FFT_N2 // FFT_RT, 2),
        in_specs=in_specs,
        out_specs=seq,
        out_shape=jax.ShapeDtypeStruct(pv.shape, F32),
        input_output_aliases={5: 0},
        compiler_params=_params("arbitrary", "arbitrary"),
        name="fft3",
    )(jnp.asarray(consts["inv"]), b, pv, xv, bias.reshape(1, D), prev_out)


def _fft_ctx_kernel(p_ref, x0_ref, kern_ref, twr_ref, twi_ref, f2_ref, f2inv_ref, bias_ref, o_ref, kf_ref,
                    *, n1, sig, ker, inv):
    f2 = _split(f2_ref[...])

    @pl.when(pl.program_id(0) == 0)
    def _():
        a = _lin_rows(ker, [kern_ref[r] for r in range(n1)])
        for k in range(n1):
            kf_ref[0, k], kf_ref[1, k] = _twiddle_dft(a[k], a[n1 + k], twr_ref[k], twi_ref[k], f2)

    f2inv = _split(f2inv_ref[...])
    a = _lin_rows(sig, [p_ref[r] for r in range(n1)])
    br, bi = [], []
    for k in range(n1):
        twr, twi = twr_ref[k], twi_ref[k]
        yr, yi = _twiddle_dft(a[k], a[n1 + k], twr, twi, f2)
        wr, wi = _filter_idft_twiddle(yr, yi, kf_ref[0, k], kf_ref[1, k], twr, twi, f2inv)
        br.append(wr)
        bi.append(wi)
    out = _lin_rows(inv, br + bi)
    bias = bias_ref[...]
    for m, y in enumerate(out):
        o_ref[m] = (y + p_ref[m] * bias) * x0_ref[m]


def _fft_ctx(consts, pv, xv, kern, bias):
    n1 = consts["n1"]
    inv = np.concatenate([consts["inv"][0], consts["inv"][1]], axis=0)
    pair = pl.BlockSpec((n1, FFT_N2, D), lambda q: (q, 0, 0))
    full = lambda shp: pl.BlockSpec(shp, lambda q: (0,) * len(shp))
    return pl.pallas_call(
        functools.partial(_fft_ctx_kernel, n1=n1, sig=consts["sig"], ker=consts["ker"], inv=inv),
        grid=(NCTX // 2,),
        in_specs=[pair, pair, full((n1, FFT_N2, D)), full((n1, FFT_N2, 1)), full((n1, FFT_N2, 1)),
                  full((2 * FFT_N2, 2 * FFT_N2)), full((2 * FFT_N2, 2 * FFT_N2)), full((1, D))],
        out_specs=pair,
        out_shape=jax.ShapeDtypeStruct(pv.shape, F32),
        scratch_shapes=[pltpu.VMEM((2, n1, FFT_N2, D), F32)],
        compiler_params=_params("arbitrary"),
        name="fft_ctx",
    )(pv, xv, kern, jnp.asarray(consts["twr"]), jnp.asarray(consts["twi"]), jnp.asarray(consts["f2"]),
      jnp.asarray(consts["f2inv"]), bias.reshape(1, D))


def _hy_long_conv(p, x0, filt_w, bias):
    slabs = (NTOK // FFT_N2, FFT_N2, D)
    pv, xv = p.reshape(slabs), x0.reshape(slabs)
    cc = _fft_consts(LCTX)
    out = _fft_ctx(cc, pv, xv, _hy_filter(LCTX, *filt_w).reshape(cc["n1"], FFT_N2, D), bias)
    lc = _fft_consts(LLAT)
    n1 = lc["n1"]
    kern = _hy_filter(LLAT, *filt_w).reshape(n1, FFT_N2, D)
    kf = _fft_stage2(lc, _fft_stage1(lc, lc["ker"], [kern], [0]), None)
    lat0 = TOK_CTX // FFT_N2 // (n1 // 2)
    a = _fft_stage1(lc, lc["sig"], [pv, pv], [lat0, lat0 + 1])
    b = _fft_stage2(lc, a, kf)
    return _fft_stage3(lc, b, pv, xv, bias, lat0, out).reshape(NTOK, D)


FFN_TM = 1024
FFN_TF = 256
FFN_HALO = 128


FFN_SR = 32


FFN_CH = 256
FFN_NF = D_FF // FFN_TF
FFN_TILES = NTOK // FFN_TM * FFN_NF


def _ffn_tile(t):
    return jnp.minimum(t, FFN_TILES - 1) // FFN_NF, t % FFN_NF


def _ffn_act_kernel(xc_ref, xp_ref, xn_ref, g_ref, sh_ref, sc_ref, wg32_ref, wu32_ref, cw_ref, cb_ref,
                    o_ref, h_ref, wg_ref, wu_ref, *slots):
    t = pl.program_id(0)
    i, f = _ffn_tile(t)
    ext = FFN_TM + 2 * FFN_HALO
    is_lat = i >= TOK_CTX // FFN_TM

    @pl.when(t == 0)
    def _():
        for scr in slots[4:]:
            scr[...] = jnp.zeros(scr.shape, F32)

    wg_ref[...] = wg32_ref[...].astype(BF16)
    wu_ref[...] = wu32_ref[...].astype(BF16)

    @pl.when(f == 0)
    def _():
        def norm(x):
            y = x * lax.rsqrt(jnp.mean(x * x, axis=-1, keepdims=True) + EPS) * g_ref[...]
            return (y * (1.0 + sc_ref[...]) + sh_ref[...]).astype(BF16)
        seq_pos = (i * FFN_TM - TOK_CTX) & (LLAT - 1)
        keep_prev = jnp.logical_and(is_lat, seq_pos != 0)
        keep_next = jnp.logical_and(is_lat, seq_pos != LLAT - FFN_TM)
        h_ref[0:FFN_HALO, :] = jnp.where(keep_prev, norm(xp_ref[...]), 0.0).astype(BF16)
        h_ref[FFN_HALO:FFN_HALO + FFN_TM, :] = norm(xc_ref[...])
        h_ref[FFN_HALO + FFN_TM:ext, :] = jnp.where(keep_next, norm(xn_ref[...]), 0.0).astype(BF16)

    def project_gate(scr, lo, hi):
        gate_scr, left_scr, right_scr, _ = scr
        gate = _dot(h_ref[lo:hi, :], wg_ref[...])
        period = jnp.where(is_lat, GRID_W, LCTX)
        r = lax.broadcasted_iota(jnp.int32, (hi - lo, 1), 0)
        col = (r + lo + FFN_HALO) & (period - 1)
        gate_scr[lo:hi, :] = gate
        left_scr[lo:hi, :] = jnp.where(col == 0, 0.0, pltpu.roll(gate, 1, 0))
        right_scr[lo:hi, :] = jnp.where(col == period - 1, 0.0, pltpu.roll(gate, hi - lo - 1, 0))

    def project_up(scr, lo, hi):
        scr[3][lo - FFN_HALO:hi - FFN_HALO, :] = _dot(h_ref[lo:hi, :], wu_ref[...])

    def convolve(scr, strips):
        gate_scr, left_scr, right_scr, up_scr = scr
        ip, _ = _ffn_tile(jnp.maximum(t - 1, 0))
        kidx = lax.broadcasted_iota(jnp.int32, (9, 1), 0)
        mid_row = jnp.logical_and(kidx >= 3, kidx < 6)
        w = jnp.where(jnp.logical_or(ip >= TOK_CTX // FFN_TM, mid_row), cw_ref[...], 0.0)
        bias = cb_ref[...]
        for s in strips:
            z = bias
            for k in range(3):
                lo = s * FFN_SR + FFN_HALO + (k - 1) * GRID_W
                z = z + left_scr[lo:lo + FFN_SR, :] * w[3 * k:3 * k + 1] \
                    + gate_scr[lo:lo + FFN_SR, :] * w[3 * k + 1:3 * k + 2] \
                    + right_scr[lo:lo + FFN_SR, :] * w[3 * k + 2:3 * k + 3]
            out = slice(s * FFN_SR, (s + 1) * FFN_SR)
            o_ref[out, :] = ((0.5 * z) * (1.0 + jnp.tanh(0.5 * z)) * up_scr[out, :]).astype(BF16)

    def step(cur, prev):
        edges = [0] + list(range(FFN_HALO + FFN_CH, FFN_HALO + FFN_TM, FFN_CH)) + [ext]
        pieces = []
        for lo, hi in zip(edges[:-1], edges[1:]):
            pieces.append(functools.partial(project_gate, cur, lo, hi))
            pieces.append(functools.partial(project_up, cur, max(lo, FFN_HALO), min(hi, FFN_HALO + FFN_TM)))
        nst = FFN_TM // FFN_SR
        for c, piece in enumerate(pieces):
            piece()
            convolve(prev, range(c * nst // len(pieces), (c + 1) * nst // len(pieces)))

    pl.when(t % 2 == 0)(lambda: step(slots[:4], slots[4:]))
    pl.when(t % 2 == 1)(lambda: step(slots[4:], slots[:4]))


def _ffn_act(x, gain, shift, scale, w_gate, w_up, conv_w, conv_b):
    hb = FFN_TM // FFN_HALO
    nhb = NTOK // FFN_HALO
    ext = FFN_TM + 2 * FFN_HALO
    row = lambda t: _ffn_tile(t)[0]
    col = lambda t: _ffn_tile(t)[1]
    prev = lambda t: _ffn_tile(jnp.maximum(t - 1, 0))
    vec = pl.BlockSpec((None, 1, D), lambda t: (row(t) * FFN_TM // ROWBLK, 0, 0))
    in_specs = [pl.BlockSpec((FFN_TM, D), lambda t: (row(t), 0)),
                pl.BlockSpec((FFN_HALO, D), lambda t: (jnp.maximum(row(t) * hb - 1, 0), 0)),
                pl.BlockSpec((FFN_HALO, D), lambda t: (jnp.minimum((row(t) + 1) * hb, nhb - 1), 0)),
                pl.BlockSpec((1, D), lambda t: (0, 0)), vec, vec,
                pl.BlockSpec((D, FFN_TF), lambda t: (0, col(t))),
                pl.BlockSpec((D, FFN_TF), lambda t: (0, col(t))),
                pl.BlockSpec((9, FFN_TF), lambda t: (0, prev(t)[1])),
                pl.BlockSpec((1, FFN_TF), lambda t: (0, prev(t)[1]))]
    return pl.pallas_call(
        _ffn_act_kernel,
        grid=(FFN_TILES + 1,),
        in_specs=in_specs,
        out_specs=pl.BlockSpec((FFN_TM, FFN_TF), lambda t: prev(t)),
        out_shape=jax.ShapeDtypeStruct((NTOK, D_FF), BF16),
        scratch_shapes=[pltpu.VMEM((ext, D), BF16), pltpu.VMEM((D, FFN_TF), BF16), pltpu.VMEM((D, FFN_TF), BF16)]
        + 2 * (3 * [pltpu.VMEM((ext, FFN_TF), F32)] + [pltpu.VMEM((FFN_TM, FFN_TF), F32)]),
        compiler_params=_params("arbitrary"),
        name="ffn_act",
    )(x, x, x, gain.reshape(1, D), shift, scale, w_gate, w_up, conv_w.reshape(9, D_FF), conv_b.reshape(1, D_FF))


def _ab_w_in_cols(w):
    qkv_w, gate_c0 = 3 * HW, 3 * HW + HW
    rest = jnp.concatenate([w[:, qkv_w:gate_c0], w[:, gate_c0 + 4 * HEADS:], w[:, gate_c0:gate_c0 + 4 * HEADS],
                            jnp.zeros((D, LANE - 4 * HEADS), F32)], axis=1)
    return w[:, :qkv_w].astype(BF16), rest.astype(BF16)


def _ab_mixer(qkv, proj, a_log, dt_bias, ret_decay, s_delta0, s_ret0):
    gates = _gates(proj, a_log, dt_bias)
    gates_t = gates[:, :2 * HEADS].reshape(NTOK // DP_RB, DP_NC, CHUNK, 2 * HEADS).transpose(0, 3, 1, 2)
    u, wq, ak, gl = _delta_prep(qkv, gates, gates_t)
    odf, odb, orf, orb, sfin, rfin = _ab_scan(u, wq, ak, gl, proj, ret_decay, s_delta0, s_ret0)
    return (odf, odb, orf, orb), sfin[:NCTX], rfin[:NCTX]


def kernel(x_prompt, x_sample, state_delta, state_ret, c, c_ctx, mod_w, mod_b, norm1, norm2, ab_w_in, ab_conv, ab_a_log, ab_dt_bias, ab_norm_a, ab_norm_b, ab_ret_decay, ab_w_out, hy_w_in, hy_b_in, hy_conv_w, hy_conv_b, hy_f_w1, hy_f_b1, hy_f_freq1, hy_f_w2, hy_f_b2, hy_f_freq2, hy_f_w3, hy_f_bias, hy_w_out, hy_b_out, ffn_w_gate, ffn_w_up, ffn_conv, ffn_conv_b, ffn_w_down, final_norm):
    x = jnp.concatenate([x_prompt.reshape(TOK_CTX, D), x_sample.reshape(TOK_LAT, D)], axis=0)
    cvec = jnp.concatenate([c_ctx[None], c, jnp.zeros((8 - 1 - NLAT, D), F32)], axis=0)
    mod = _mod_all(cvec, mod_w, mod_b)
    new_delta, new_ret = [], []
    for l in range(DEPTH):
        m = [mod[l, :NGROUP, k * D:(k + 1) * D].reshape(NGROUP, 1, D) for k in range(6)]
        j = l // 2
        if l % 2 == 0:
            w_qkv, w_rest = _ab_w_in_cols(ab_w_in[j])
            qkv = _ab_qkv(x, norm1[l], m[0], m[1], w_qkv, ab_conv[j])
            proj = _nmm(x, norm1[l], m[0], m[1], w_rest, None, AB_N // 3, "ab_in")
            heads, sd, sr = _ab_mixer(qkv, proj, ab_a_log[j], ab_dt_bias[j], ab_ret_decay[j],
                                      state_delta[:, j], state_ret[:, j])
            new_delta.append(sd)
            new_ret.append(sr)
            x = _ab_out(*heads, proj, ab_norm_a[j], ab_norm_b[j], ab_w_out[j], x, m[2])
        else:
            x0, p = _hy_in(x, norm1[l], m[0], m[1], hy_w_in[j].astype(BF16), hy_b_in[j], hy_conv_w[j], hy_conv_b[j])
            filt_w = (hy_f_w1[j], hy_f_b1[j], hy_f_freq1[j], hy_f_w2[j], hy_f_b2[j], hy_f_freq2[j], hy_f_w3[j])
            y = _hy_long_conv(p, x0, filt_w, hy_f_bias[j])
            x = _mmr(y, hy_w_out[j], hy_b_out[j], x, m[2], None, "hy_out")
        act = _ffn_act(x, norm2[l], m[3], m[4], ffn_w_gate[l], ffn_w_up[l], ffn_conv[l], ffn_conv_b[l])
        x = _mmr(act, ffn_w_down[l], None, x, m[5], final_norm if l == DEPTH - 1 else None, "ffn_out")
    y_prompt = x[:TOK_CTX].reshape(NCTX, LCTX, D)
    y_sample = x[TOK_CTX:].reshape(NLAT, LLAT, D)
    return (y_prompt, y_sample, jnp.stack(new_delta, axis=1), jnp.stack(new_ret, axis=1))
```

```python
import functools
import math

import numpy as np
import jax
import jax.numpy as jnp
from jax import lax
from jax.experimental import pallas as pl
from jax.experimental.pallas import tpu as pltpu

F32, BF16 = jnp.float32, jnp.bfloat16
HI = lax.Precision.HIGHEST

D = 1024
NCTX, LCTX = 16, 256
NLAT, LLAT = 2, 4096
DEPTH = 4
TOK_CTX = NCTX * LCTX
TOK_LAT = NLAT * LLAT
NTOK = TOK_CTX + TOK_LAT
ROWBLK = 4096
NGROUP = NTOK // ROWBLK
GRID_W = 64
CHUNK = 64
NCH = NTOK // CHUNK
EPS = 1e-6
HEADS = 4
DK = 128
HW = HEADS * DK
LANE = 128
D_FF = 2816
HY_EMB = 33
HY_BANDS = 16
HY_FW = 64
HY_TARGET = 1e-2
HY_FAST_PCT = 0.3
HY_SLOW_PCT = 1.5
FFT_N2 = 128
VMEM_LIMIT = 52 * 1024 * 1024

CB_QA, CB_KA, CB_VA = 0, 4, 8
CB_ZA, CB_QB, CB_KB, CB_VB, CB_GB, CB_GATE = 0, 4, 8, 12, 16, 20
AB_N = 21 * LANE


def _params(*sem):
    return pltpu.CompilerParams(dimension_semantics=sem, vmem_limit_bytes=VMEM_LIMIT)


def _silu(x):
    return x * jax.nn.sigmoid(x)


def _dot(a, b, precision=None):
    return jnp.dot(a, b, preferred_element_type=F32, precision=precision)


def _dot_nt(a, b):
    return lax.dot_general(a, b, (((1,), (1,)), ((), ())), preferred_element_type=F32)


def _dot_tn(a, b):
    return lax.dot_general(a, b, (((0,), (0,)), ((), ())), preferred_element_type=F32)


def _pick_lane(x, lane):
    li = lax.broadcasted_iota(jnp.int32, x.shape, 1)
    return jnp.sum(jnp.where(li == lane, x, 0.0), axis=-1, keepdims=True)


def _mod_kernel(c_ref, w_ref, b_ref, o_ref):
    s = _silu(c_ref[...])
    o_ref[...] = _dot(s.astype(BF16), w_ref[...].astype(BF16)) + b_ref[...]


def _mod_all(cvec, mod_w, mod_b):
    tn = 1536
    return pl.pallas_call(
        _mod_kernel,
        grid=(DEPTH, 6 * D // tn),
        in_specs=[pl.BlockSpec((8, D), lambda l, j: (0, 0)),
                  pl.BlockSpec((None, D, tn), lambda l, j: (l, 0, j)),
                  pl.BlockSpec((None, 1, tn), lambda l, j: (l, 0, j))],
        out_specs=pl.BlockSpec((None, 8, tn), lambda l, j: (l, 0, j)),
        out_shape=jax.ShapeDtypeStruct((DEPTH, 8, 6 * D), F32),
        compiler_params=_params("arbitrary", "arbitrary"),
        name="mod",
    )(cvec, mod_w, mod_b.reshape(DEPTH, 1, 6 * D))


NMM_TM = 1024


def _nmm_kernel(x_ref, g_ref, sh_ref, sc_ref, w_ref, *rest, has_bias):
    if has_bias:
        b_ref, o_ref, h_ref = rest
    else:
        o_ref, h_ref = rest

    @pl.when(pl.program_id(1) == 0)
    def _():
        x = x_ref[...]
        y = x * lax.rsqrt(jnp.mean(x * x, axis=-1, keepdims=True) + EPS) * g_ref[...]
        h_ref[...] = (y * (1.0 + sc_ref[...]) + sh_ref[...]).astype(BF16)

    acc = _dot(h_ref[...], w_ref[...])
    if has_bias:
        acc = acc + b_ref[...]
    o_ref[...] = acc


def _nmm(x, gain, shift, scale, w, bias, tn, name):
    n = w.shape[1]
    grp = lambda i, j: (i * NMM_TM // ROWBLK, 0, 0)
    in_specs = [pl.BlockSpec((NMM_TM, D), lambda i, j: (i, 0)),
                pl.BlockSpec((1, D), lambda i, j: (0, 0)),
                pl.BlockSpec((None, 1, D), grp),
                pl.BlockSpec((None, 1, D), grp),
                pl.BlockSpec((D, tn), lambda i, j: (0, j))]
    args = [x, gain.reshape(1, D), shift, scale, w]
    if bias is not None:
        in_specs.append(pl.BlockSpec((1, tn), lambda i, j: (0, j)))
        args.append(bias.reshape(1, n))
    return pl.pallas_call(
        functools.partial(_nmm_kernel, has_bias=bias is not None),
        grid=(NTOK // NMM_TM, n // tn),
        in_specs=in_specs,
        out_specs=pl.BlockSpec((NMM_TM, tn), lambda i, j: (i, j)),
        out_shape=jax.ShapeDtypeStruct((NTOK, n), F32),
        scratch_shapes=[pltpu.VMEM((NMM_TM, D), BF16)],
        compiler_params=_params("arbitrary", "arbitrary"),
        name=name,
    )(*args)


MMR_TM = 512


def _mmr_kernel(*refs, split, has_bias, final):
    rest = list(refs)
    a_refs = [rest.pop(0) for _ in range(2 if split else 1)]
    w_ref = rest.pop(0)
    b_ref = rest.pop(0) if has_bias else None
    x_ref, gate_ref = rest.pop(0), rest.pop(0)
    fn_ref = rest.pop(0) if final else None
    o_ref, wb_ref = rest
    i = pl.program_id(0)

    @pl.when(i == 0)
    def _():
        wb_ref[...] = w_ref[...].astype(BF16)

    a = a_refs[0][...]
    if split:
        a = jnp.where(i < TOK_CTX // MMR_TM, a, a_refs[1][...])
    acc = _dot(a.astype(BF16), wb_ref[...])
    if has_bias:
        acc = acc + b_ref[...]
    y = x_ref[...] + gate_ref[...] * acc
    if final:
        y = y * lax.rsqrt(jnp.mean(y * y, axis=-1, keepdims=True) + EPS) * fn_ref[...]
    o_ref[...] = y


def _mmr(a, w_all, layer, bias, x, gate, final_gain, name):
    grp = lambda i: (i * MMR_TM // ROWBLK, 0, 0)
    split = isinstance(a, tuple)
    nctx = TOK_CTX // MMR_TM
    if split:
        k = a[0].shape[1]
        in_specs = [pl.BlockSpec((MMR_TM, k), lambda i: (jnp.minimum(i, nctx - 1), 0)),
                    pl.BlockSpec((MMR_TM, k), lambda i: (jnp.maximum(i - nctx, 0), 0))]
        args = list(a)
    else:
        k = a.shape[1]
        in_specs = [pl.BlockSpec((MMR_TM, k), lambda i: (i, 0))]
        args = [a]
    in_specs.append(pl.BlockSpec((None, k, D), lambda i: (layer, 0, 0)))
    args.append(w_all)
    if bias is not None:
        in_specs.append(pl.BlockSpec((1, D), lambda i: (0, 0)))
        args.append(bias.reshape(1, D))
    in_specs += [pl.BlockSpec((MMR_TM, D), lambda i: (i, 0)), pl.BlockSpec((None, 1, D), grp)]
    args += [x, gate]
    if final_gain is not None:
        in_specs.append(pl.BlockSpec((1, D), lambda i: (0, 0)))
        args.append(final_gain.reshape(1, D))
    return pl.pallas_call(
        functools.partial(_mmr_kernel, split=split, has_bias=bias is not None, final=final_gain is not None),
        grid=(NTOK // MMR_TM,),
        in_specs=in_specs,
        out_specs=pl.BlockSpec((MMR_TM, D), lambda i: (i, 0)),
        out_shape=jax.ShapeDtypeStruct((NTOK, D), F32),
        scratch_shapes=[pltpu.VMEM((k, D), BF16)],
        compiler_params=_params("arbitrary"),
        name=name,
    )(*args)


PC_TM = 1024
PC_HALO = 16
PC_TN = 256


def _pc_norm_rows(i, xc_ref, xp_ref, xn_ref, g_ref, sh_ref, sc_ref, h_ref):
    def norm(x):
        y = x * lax.rsqrt(jnp.mean(x * x, axis=-1, keepdims=True) + EPS) * g_ref[...]
        return (y * (1.0 + sc_ref[...]) + sh_ref[...]).astype(BF16)
    h_ref[0:PC_HALO, :] = norm(xp_ref[...])
    h_ref[PC_HALO:PC_HALO + PC_TM, :] = norm(xc_ref[...])
    h_ref[PC_HALO + PC_TM:PC_TM + 2 * PC_HALO, :] = norm(xn_ref[...])


def _pc_proj_conv(i, h_ref, w_ref, b_ref, cw_ref):
    ext = PC_TM + 2 * PC_HALO
    u = _dot(h_ref[...], w_ref[...])
    if b_ref is not None:
        u = u + b_ref[...]
    lseq = jnp.where(i < TOK_CTX // PC_TM, LCTX, LLAT)
    pos = (i * PC_TM - PC_HALO + lax.broadcasted_iota(jnp.int32, (ext, 1), 0)) & (lseq - 1)
    prev = jnp.where(pos == 0, 0.0, pltpu.roll(u, 1, 0))
    nxt = jnp.where(pos == lseq - 1, 0.0, pltpu.roll(u, ext - 1, 0))
    cw = cw_ref[...]
    y = prev * cw[0:1] + u * cw[1:2] + nxt * cw[2:3]
    return y[PC_HALO:PC_HALO + PC_TM]


def _pc_specs():
    hb = PC_TM // PC_HALO
    nhb = NTOK // PC_HALO
    vec = pl.BlockSpec((None, 1, D), lambda i, j: (i * PC_TM // ROWBLK, 0, 0))
    return [pl.BlockSpec((PC_TM, D), lambda i, j: (i, 0)),
            pl.BlockSpec((PC_HALO, D), lambda i, j: (jnp.maximum(i * hb - 1, 0), 0)),
            pl.BlockSpec((PC_HALO, D), lambda i, j: (jnp.minimum((i + 1) * hb, nhb - 1), 0)),
            pl.BlockSpec((1, D), lambda i, j: (0, 0)), vec, vec]


def _hy_in_kernel(xc_ref, xp_ref, xn_ref, g_ref, sh_ref, sc_ref, w0_ref, w1_ref, wv_ref, b0_ref, b1_ref, bv_ref,
                  c0_ref, c1_ref, cv_ref, cb0_ref, cb1_ref, cbv_ref, x0_ref, p_ref, h_ref):
    i = pl.program_id(0)

    @pl.when(pl.program_id(1) == 0)
    def _():
        _pc_norm_rows(i, xc_ref, xp_ref, xn_ref, g_ref, sh_ref, sc_ref, h_ref)

    x0_ref[...] = _pc_proj_conv(i, h_ref, w0_ref, b0_ref, c0_ref) + cb0_ref[...]
    x1 = _pc_proj_conv(i, h_ref, w1_ref, b1_ref, c1_ref) + cb1_ref[...]
    v = _pc_proj_conv(i, h_ref, wv_ref, bv_ref, cv_ref) + cbv_ref[...]
    p_ref[...] = v * x1


def _hy_in(x, gain, shift, scale, w, b, conv_w, conv_b):
    nj = D // PC_TN
    wsp = lambda k: pl.BlockSpec((D, PC_TN), lambda i, j: (0, k * nj + j))
    row = lambda r, k: pl.BlockSpec((r, PC_TN), lambda i, j: (0, k * nj + j))
    out = pl.BlockSpec((PC_TM, PC_TN), lambda i, j: (i, j))
    b2, cb2 = b.reshape(1, 3 * D), conv_b.reshape(1, 3 * D)
    return pl.pallas_call(
        _hy_in_kernel,
        grid=(NTOK // PC_TM, nj),
        in_specs=_pc_specs() +[wsp(0), wsp(1), wsp(2), row(1, 0), row(1, 1), row(1, 2),
                                 row(3, 0), row(3, 1), row(3, 2), row(1, 0), row(1, 1), row(1, 2)],
        out_specs=[out, out],
        out_shape=[jax.ShapeDtypeStruct((NTOK, D), F32)] * 2,
        scratch_shapes=[pltpu.VMEM((PC_TM + 2 * PC_HALO, D), BF16)],
        compiler_params=_params("arbitrary", "arbitrary"),
        name="hy_in",
    )(x, x, x, gain.reshape(1, D), shift, scale, w, w, w, b2, b2, b2, conv_w, conv_w, conv_w, cb2, cb2, cb2)


def _ab_qkv_kernel(xc_ref, xp_ref, xn_ref, g_ref, sh_ref, sc_ref, w_ref, cw_ref, o_ref, h_ref):
    i, j = pl.program_id(0), pl.program_id(1)

    @pl.when(j == 0)
    def _():
        _pc_norm_rows(i, xc_ref, xp_ref, xn_ref, g_ref, sh_ref, sc_ref, h_ref)

    y = _silu(_pc_proj_conv(i, h_ref, w_ref, None, cw_ref))
    is_qk = j < 2 * HW // PC_TN
    qscale = jnp.where(j < HW // PC_TN, DK ** -0.5, 1.0)
    for blk in range(PC_TN // DK):
        cols = slice(blk * DK, (blk + 1) * DK)
        yb = y[:, cols]
        nrm = yb * (lax.rsqrt(jnp.sum(yb * yb, axis=-1, keepdims=True) + EPS) * qscale)
        o_ref[:, cols] = jnp.where(is_qk, nrm, yb)


def _ab_qkv(x, gain, shift, scale, w, conv_w):
    n = 3 * HW
    return pl.pallas_call(
        _ab_qkv_kernel,
        grid=(NTOK // PC_TM, n // PC_TN),
        in_specs=_pc_specs() +[pl.BlockSpec((D, PC_TN), lambda i, j: (0, j)),
                                 pl.BlockSpec((3, PC_TN), lambda i, j: (0, j))],
        out_specs=pl.BlockSpec((PC_TM, PC_TN), lambda i, j: (i, j)),
        out_shape=jax.ShapeDtypeStruct((NTOK, n), F32),
        scratch_shapes=[pltpu.VMEM((PC_TM + 2 * PC_HALO, D), BF16)],
        compiler_params=_params("arbitrary", "arbitrary"),
        name="ab_qkv",
    )(x, x, x, gain.reshape(1, D), shift, scale, w, conv_w)


def _gates_kernel(x_ref, alog_ref, dtb_ref, o_ref):
    x = x_ref[...]
    lane = lax.broadcasted_iota(jnp.int32, x.shape, 1)
    row = lax.broadcasted_iota(jnp.int32, x.shape, 0) & (CHUNK - 1)
    t = x + dtb_ref[...]
    softplus = jnp.maximum(t, 0.0) + jnp.log(1.0 + jnp.exp(-jnp.abs(t)))
    g = -jnp.exp(alog_ref[...]) * softplus
    pre, suf = g, g
    s = 1
    while s < CHUNK:
        pre = pre + jnp.where(row >= s, pltpu.roll(pre, s, 0), 0.0)
        suf = suf + jnp.where(row < CHUNK - s, pltpu.roll(suf, ROWBLK - s, 0), 0.0)
        s *= 2
    gc = jnp.where(lane < HEADS, pre, suf)
    o_ref[...] = jnp.where(lane < 2 * HEADS, gc, jax.nn.sigmoid(x))


def _gates(proj, a_log, dt_bias):
    pad = lambda v: jnp.pad(v.reshape(1, 2 * HEADS), ((0, 0), (0, LANE - 2 * HEADS)))
    return pl.pallas_call(
        _gates_kernel,
        grid=(NGROUP,),
        in_specs=[pl.BlockSpec((ROWBLK, LANE), lambda i: (i, CB_GATE)),
                  pl.BlockSpec((1, LANE), lambda i: (0, 0)),
                  pl.BlockSpec((1, LANE), lambda i: (0, 0))],
        out_specs=pl.BlockSpec((ROWBLK, LANE), lambda i: (i, 0)),
        out_shape=jax.ShapeDtypeStruct((NTOK, LANE), F32),
        compiler_params=_params("arbitrary"),
        name="ab_gates",
    )(proj, pad(a_log), pad(dt_bias))


DP_RB = 512
DP_NC = DP_RB // CHUNK
TRI_BASE = 8


def _split(x):
    hi = x.astype(BF16)
    return hi, (x - hi.astype(F32)).astype(BF16)


def _dot3(a, b):
    return _dot(a[0], b[0]) + _dot(a[0], b[1]) + _dot(a[1], b[0])


def _unit_tri_inv_batch(lms):
    ii = lax.broadcasted_iota(jnp.int32, (CHUNK, CHUNK), 0)
    jj = lax.broadcasted_iota(jnp.int32, (CHUNK, CHUNK), 1)
    same = lambda b: (ii >> int(math.log2(b))) == (jj >> int(math.log2(b)))
    eye = jnp.where(ii == jj, 1.0, 0.0)
    qs = [jnp.where(same(TRI_BASE), -lm, 0.0) for lm in lms]
    ps = [eye + q for q in qs]
    qs = [_split(q) for q in qs]
    for _ in range(int(math.log2(TRI_BASE)) - 1):
        qs = [_split(_dot3(q, q)) for q in qs]
        ps = [p + _dot3(_split(p), q) for p, q in zip(ps, qs)]
    b = TRI_BASE
    while b < CHUNK:
        off = jnp.logical_and(same(2 * b), jnp.logical_not(same(b)))
        pbs = [p.astype(BF16) for p in ps]
        ts = [_dot(p, jnp.where(off, lm, 0.0).astype(BF16)) for p, lm in zip(pbs, lms)]
        ps = [p - _dot(t.astype(BF16), p2) for p, p2, t in zip(ps, pbs, ts)]
        b *= 2
    return ps


def _delta_prep_kernel(q_ref, k_ref, v_ref, g_ref, gt_ref, u_ref, wq_ref, ak_ref, gl_ref):
    h = pl.program_id(1)
    ii = lax.broadcasted_iota(jnp.int32, (CHUNK, CHUNK), 0)
    jj = lax.broadcasted_iota(jnp.int32, (CHUNK, CHUNK), 1)
    probs = []
    for c in range(DP_NC):
        rows = slice(c * CHUNK, (c + 1) * CHUNK)
        q, k, v, gts = q_ref[rows, :], k_ref[rows, :], v_ref[rows, :], g_ref[rows, :]
        kbf = k.astype(BF16)
        qk = _dot_nt(q.astype(BF16), kbf)
        for d in range(2):
            incl = (ii >= jj) if d == 0 else (ii <= jj)
            strict = (ii > jj) if d == 0 else (ii < jj)
            gcol = _pick_lane(gts, d * HEADS + h)
            bcol = _pick_lane(gts, 2 * HEADS + d * HEADS + h)
            grow = gt_ref[d * HEADS + h][c:c + 1, :]
            dmask = jnp.where(incl, jnp.exp(jnp.where(incl, gcol - grow, 0.0)), 0.0)
            kb = k * bcol
            lm = jnp.where(strict, _dot_nt(kb.astype(BF16), kbf) * dmask, 0.0)
            probs.append((c, d, rows, q, k, v, kb, gcol, bcol, dmask, qk, incl, lm))
    tmats = _unit_tri_inv_batch([p[-1] for p in probs])
    for (c, d, rows, q, k, v, kb, gcol, bcol, dmask, qk, incl, _), tmat in zip(probs, tmats):
        gam = jnp.exp(gcol)
        rhs = jnp.concatenate([v * bcol, kb * gam], axis=1).astype(BF16)
        uw = _dot(tmat.astype(BF16), rhs)
        attn = jnp.where(incl, qk * dmask, 0.0)
        gtot = gcol[CHUNK - 1:CHUNK, :] if d == 0 else gcol[0:1, :]
        kd = k * jnp.exp(gtot - gcol)
        u_ref[d, rows, :] = uw[:, :DK]
        wq_ref[d, c] = jnp.concatenate([uw[:, DK:], q * gam], axis=0).astype(BF16)
        ak_ref[d, c] = jnp.concatenate([attn, kd.T], axis=0).astype(BF16)
        gl_ref[d, c] = jnp.broadcast_to(jnp.exp(gtot), (1, LANE))


def _delta_prep(qkv, gates, gates_t):
    nrb = NTOK // DP_RB
    blk = lambda cb: pl.BlockSpec((DP_RB, LANE), lambda i, h: (i, cb + h))
    return pl.pallas_call(
        _delta_prep_kernel,
        grid=(nrb, HEADS),
        in_specs=[blk(CB_QA), blk(CB_KA), blk(CB_VA),
                  pl.BlockSpec((DP_RB, LANE), lambda i, h: (i, 0)),
                  pl.BlockSpec((None, 2 * HEADS, DP_NC, CHUNK), lambda i, h: (i, 0, 0, 0))],
        out_specs=[pl.BlockSpec((2, DP_RB, LANE), lambda i, h: (0, i, h)),
                   pl.BlockSpec((2, None, DP_NC, 2 * CHUNK, DK), lambda i, h: (0, h, i, 0, 0)),
                   pl.BlockSpec((2, None, DP_NC, CHUNK + DK, CHUNK), lambda i, h: (0, h, i, 0, 0)),
                   pl.BlockSpec((2, None, DP_NC, 1, LANE), lambda i, h: (0, h, i, 0, 0))],
        out_shape=[jax.ShapeDtypeStruct((2, NTOK, HW), F32),
                   jax.ShapeDtypeStruct((2, HEADS, NCH, 2 * CHUNK, DK), BF16),
                   jax.ShapeDtypeStruct((2, HEADS, NCH, CHUNK + DK, CHUNK), BF16),
                   jax.ShapeDtypeStruct((2, HEADS, NCH, 1, LANE), F32)],
        compiler_params=_params("arbitrary", "arbitrary"),
        name="delta_prep",
    )(qkv, qkv, qkv, gates, gates_t)


SCAN_G = LCTX // CHUNK
SCAN_RB = SCAN_G * CHUNK
SCAN_CTX_STEPS = TOK_CTX // SCAN_RB
SCAN_LAT_STEPS = LLAT // SCAN_RB
assert SCAN_RB == LCTX and SCAN_CTX_STEPS % SCAN_LAT_STEPS == 0


def _scan_bwd_group(t):
    i = t % SCAN_LAT_STEPS
    return jnp.where(t < SCAN_CTX_STEPS, t, t - i + (SCAN_LAT_STEPS - 1 - i))


def _scan_seq(t):
    return jnp.where(t < SCAN_CTX_STEPS, t, NCTX + (t - SCAN_CTX_STEPS) // SCAN_LAT_STEPS)


def _ab_scan_kernel(uf_ref, ub_ref, wqf_ref, wqb_ref, akf_ref, akb_ref, glf_ref, glb_ref,
                    qf_ref, kf_ref, vf_ref, qb_ref, kb_ref, vb_ref, dec_ref, s0_ref, r0_ref,
                    odf_ref, odb_ref, orf_ref, orb_ref, sfin_ref, rfin_ref, s_scr, r_scr):
    t = pl.program_id(0)
    in_ctx = t < SCAN_CTX_STEPS
    first = jnp.logical_or(in_ctx, t % SCAN_LAT_STEPS == 0)
    last = jnp.logical_or(in_ctx, t % SCAN_LAT_STEPS == SCAN_LAT_STEPS - 1)

    @pl.when(first)
    def _():
        s_scr[...] = jnp.where(in_ctx, 0.0, s0_ref[...])
        r_scr[...] = jnp.where(in_ctx, 0.0, r0_ref[...])

    dirs = ((uf_ref, wqf_ref, akf_ref, glf_ref, qf_ref, kf_ref, vf_ref, odf_ref, orf_ref),
            (ub_ref, wqb_ref, akb_ref, glb_ref, qb_ref, kb_ref, vb_ref, odb_ref, orb_ref))

    ii = lax.broadcasted_iota(jnp.int32, (SCAN_RB, SCAN_RB), 0)
    jj = lax.broadcasted_iota(jnp.int32, (SCAN_RB, SCAN_RB), 1)
    ci = lax.broadcasted_iota(jnp.int32, (SCAN_RB, 1), 0).astype(F32)
    for d in range(2):
        q_ref, k_ref, v_ref, or_ref = dirs[d][4], dirs[d][5], dirs[d][6], dirs[d][8]
        dist = (ii - jj) if d == 0 else (jj - ii)
        for h in range(HEADS):
            cols = slice(h * DK, (h + 1) * DK)
            lane = d * HEADS + h
            lg = -jnp.exp(dec_ref[:, lane:lane + 1])
            dmat = jnp.where(dist >= 0, jnp.exp(jnp.maximum(dist, 0).astype(F32) * lg), 0.0)
            cross = jnp.exp(((ci + 1.0) if d == 0 else (SCAN_RB - ci)) * lg)
            sdec = jnp.exp(((SCAN_RB - 1.0 - ci) if d == 0 else ci) * lg)
            r = r_scr[d, h]
            q = q_ref[:, cols] * DK ** -0.5
            k = k_ref[:, cols]
            vb = v_ref[:, cols].astype(BF16)
            inner = _dot_nt(q.astype(BF16), k.astype(BF16)) * dmat
            or_ref[:, cols] = _dot(inner.astype(BF16), vb) + _dot((q * cross).astype(BF16), r.astype(BF16))
            r_scr[d, h] = r * jnp.exp(SCAN_RB * lg) + _dot_tn((k * sdec).astype(BF16), vb)

    s_cur = {(d, h): s_scr[d, h] for d in range(2) for h in range(HEADS)}
    for c in range(SCAN_G):
        for d in range(2):
            u_ref, wq_ref, ak_ref, gl_ref, q_ref, k_ref, v_ref, od_ref, or_ref = dirs[d]
            cc = c if d == 0 else SCAN_G - 1 - c
            rows = slice(cc * CHUNK, (cc + 1) * CHUNK)
            for h in range(HEADS):
                cols = slice(h * DK, (h + 1) * DK)
                s = s_cur[d, h]
                r1 = _dot(wq_ref[h, cc], s.astype(BF16))
                vnew = u_ref[rows, cols] - r1[:CHUNK]
                r2 = _dot(ak_ref[h, cc], vnew.astype(BF16))
                od_ref[rows, cols] = r1[CHUNK:] + r2[:CHUNK]
                s_cur[d, h] = s * gl_ref[h, cc] + r2[CHUNK:]
    for d in range(2):
        for h in range(HEADS):
            s_scr[d, h] = s_cur[d, h]

    @pl.when(last)
    def _():
        sfin_ref[...] = s_scr[...]
        rfin_ref[...] = r_scr[...]


def _ab_scan(u, wq, ak, gl, proj, ret_decay, s0, r0):
    nseq = NCTX + NLAT
    fwd = lambda t: t
    row = lambda g, cb: pl.BlockSpec((SCAN_RB, HW), lambda t: (g(t), cb))
    u_spec = lambda d, g: pl.BlockSpec((None, SCAN_RB, HW), lambda t: (d, g(t), 0))
    op_spec = lambda d, g, a, b: pl.BlockSpec((None, HEADS, SCAN_G, a, b), lambda t: (d, 0, g(t), 0, 0))
    state = pl.BlockSpec((None, 2, HEADS, DK, DK), lambda t: (_scan_seq(t), 0, 0, 0, 0))
    state0 = pl.BlockSpec((None, 2, HEADS, DK, DK), lambda t: (jnp.maximum(_scan_seq(t) - NCTX, 0), 0, 0, 0, 0))
    in_specs = [u_spec(0, fwd), u_spec(1, _scan_bwd_group),
                op_spec(0, fwd, 2 * CHUNK, DK), op_spec(1, _scan_bwd_group, 2 * CHUNK, DK),
                op_spec(0, fwd, CHUNK + DK, CHUNK), op_spec(1, _scan_bwd_group, CHUNK + DK, CHUNK),
                op_spec(0, fwd, 1, LANE), op_spec(1, _scan_bwd_group, 1, LANE),
                row(fwd, CB_QB // HEADS), row(fwd, CB_KB // HEADS), row(fwd, CB_VB // HEADS),
                row(_scan_bwd_group, CB_QB // HEADS), row(_scan_bwd_group, CB_KB // HEADS),
                row(_scan_bwd_group, CB_VB // HEADS),
                pl.BlockSpec((1, LANE), lambda t: (0, 0)), state0, state0]
    dec = jnp.pad(ret_decay.reshape(1, 2 * HEADS), ((0, 0), (0, LANE - 2 * HEADS)))
    o_sds = jax.ShapeDtypeStruct((NTOK, HW), F32)
    st_sds = jax.ShapeDtypeStruct((nseq, 2, HEADS, DK, DK), F32)
    return pl.pallas_call(
        _ab_scan_kernel,
        grid=(NTOK // SCAN_RB,),
        in_specs=in_specs,
        out_specs=[row(fwd, 0), row(_scan_bwd_group, 0), row(fwd, 0), row(_scan_bwd_group, 0), state, state],
        out_shape=[o_sds, o_sds, o_sds, o_sds, st_sds, st_sds],
        scratch_shapes=[pltpu.VMEM((2, HEADS, DK, DK), F32), pltpu.VMEM((2, HEADS, DK, DK), F32)],
        compiler_params=_params("arbitrary"),
        name="ab_scan",
    )(u, u, wq, wq, ak, ak, gl, gl, proj, proj, proj, proj, proj, proj, dec, s0, r0)


def _ab_out_kernel(odf_ref, odb_ref, orf_ref, orb_ref, z_ref, gb_ref, na_ref, nb_ref, w_ref, x_ref, gate_ref,
                   o_ref, wb_ref):
    @pl.when(pl.program_id(0) == 0)
    def _():
        wb_ref[...] = w_ref[...].astype(BF16)

    oa = odf_ref[...] + odb_ref[...]
    ob = orf_ref[...] + orb_ref[...]
    z, gb = z_ref[...], gb_ref[...]
    parts = []
    for h in range(HEADS):
        cols = slice(h * DK, (h + 1) * DK)
        a = oa[:, cols]
        a = a * lax.rsqrt(jnp.mean(a * a, axis=-1, keepdims=True) + EPS) * na_ref[...]
        parts.append(a * _silu(z[:, cols]))
    for h in range(HEADS):
        cols = slice(h * DK, (h + 1) * DK)
        b = ob[:, cols]
        mu = jnp.mean(b, axis=-1, keepdims=True)
        var = jnp.mean(jnp.square(b - mu), axis=-1, keepdims=True)
        parts.append((b - mu) * lax.rsqrt(var + EPS) * nb_ref[...] * _silu(gb[:, cols]))
    a = jnp.concatenate(parts, axis=1).astype(BF16)
    o_ref[...] = x_ref[...] + gate_ref[...] * _dot(a, wb_ref[...])


def _ab_out(odf, odb, orf, orb, proj, norm_a, norm_b, w_out_all, layer, x, gate):
    grp = lambda i: (i * MMR_TM // ROWBLK, 0, 0)
    tile = lambda cb: pl.BlockSpec((MMR_TM, HW), lambda i: (i, cb))
    vec = pl.BlockSpec((1, DK), lambda i: (0, 0))
    return pl.pallas_call(
        _ab_out_kernel,
        grid=(NTOK // MMR_TM,),
        in_specs=[tile(0), tile(0), tile(0), tile(0), tile(CB_ZA // HEADS), tile(CB_GB // HEADS), vec, vec,
                  pl.BlockSpec((None, 2 * HW, D), lambda i: (layer, 0, 0)),
                  pl.BlockSpec((MMR_TM, D), lambda i: (i, 0)), pl.BlockSpec((None, 1, D), grp)],
        out_specs=pl.BlockSpec((MMR_TM, D), lambda i: (i, 0)),
        out_shape=jax.ShapeDtypeStruct((NTOK, D), F32),
        scratch_shapes=[pltpu.VMEM((2 * HW, D), BF16)],
        compiler_params=_params("arbitrary"),
        name="ab_out",
    )(odf, odb, orf, orb, proj, proj, norm_a.reshape(1, DK), norm_b.reshape(1, DK), w_out_all, x, gate)


def _filter_features(L):
    r = np.arange(2 * L)
    pos = np.where(r < L, r, 2 * L - r) % L
    t = pos / (L - 1.0)
    bands = np.linspace(1e-4, HY_BANDS - 1, HY_BANDS)
    ang = 2.0 * np.pi * np.outer(pos, bands) / L
    z = np.zeros((2 * L, HY_FW), np.float64)
    z[:, 0] = t
    z[:, 1:1 + HY_BANDS] = np.cos(ang)
    z[:, 1 + HY_BANDS:HY_EMB] = -np.sin(ang)
    z[:, HY_EMB] = (r != L)
    return z.astype(np.float32)


def _filter_kernel(z_ref, w1_ref, b1_ref, f1_ref, w2_ref, b2_ref, f2_ref, w3_ref, dl_ref, o_ref):
    z = z_ref[...]
    hid = jnp.sin(f1_ref[...] * (_dot(z, w1_ref[...], HI) + b1_ref[...]))
    hid = jnp.sin(f2_ref[...] * (_dot(hid, w2_ref[...], HI) + b2_ref[...]))
    filt = _dot3(_split(hid), _split(w3_ref[...]))
    window =jnp.exp(-z[:, 0:1] * dl_ref[...]) * z[:, HY_EMB:HY_EMB + 1]
    o_ref[...] = filt * window


def _hy_filter(L, w1, b1, f1, w2, b2, f2, w3):
    rb = min(512, L)
    nblk = 2 * L // rb
    z = jnp.asarray(_filter_features(L))
    w1p = jnp.pad(w1, ((0, HY_FW - HY_EMB), (0, 0)))
    min_decay = math.log(HY_TARGET) / HY_SLOW_PCT
    max_decay = math.log(HY_TARGET) / HY_FAST_PCT
    deltas = jnp.asarray(np.abs(np.linspace(min_decay, max_decay, D)).astype(np.float32).reshape(1, D))
    vec = lambda v: v.reshape(1, HY_FW)
    full = lambda shp: pl.BlockSpec(shp, lambda i: (0, 0))
    return pl.pallas_call(
        _filter_kernel,
        grid=(nblk,),
        in_specs=[pl.BlockSpec((rb, HY_FW), lambda i: (i, 0)),
                  full((HY_FW, HY_FW)), full((1, HY_FW)), full((1, HY_FW)),
                  full((HY_FW, HY_FW)), full((1, HY_FW)), full((1, HY_FW)),
                  pl.BlockSpec((HY_FW, D), lambda i: (0, i // (nblk // 2))),
                  full((1, D))],
        out_specs=pl.BlockSpec((rb, D), lambda i: (i, 0)),
        out_shape=jax.ShapeDtypeStruct((2 * L, D), F32),
        compiler_params=_params("arbitrary"),
        name="hy_filter_%d" % L,
    )(z, w1p, vec(b1), vec(f1), w2, vec(b2), vec(f2), w3, deltas)


FFT_RT = 8


def _cis(num, den):
    ang = -2.0 * np.pi * (num % den) / den
    return np.cos(ang), np.sin(ang)


def _fft_consts(L):
    n = 2 * L
    n1 = n // FFT_N2
    k1 = np.arange(n1)
    fr, fi = _cis(np.outer(k1, k1), n1)
    half = n1 // 2
    sig = np.block([[fr[:, :half], -fi[:, :half]], [fi[:, :half], fr[:, :half]]])
    ker = np.concatenate([fr, fi], axis=0)
    cr, ci = fr[:, :half].T, -fi[:, :half].T
    inv = np.stack([np.concatenate([cr, -ci], axis=1), np.concatenate([ci, cr], axis=1)]) / n
    k2 = np.arange(FFT_N2)
    gr, gi = _cis(np.outer(k2, k2), FFT_N2)
    f2 = np.block([[gr, -gi], [gi, gr]])
    f2inv = np.block([[gr, gi], [-gi, gr]])
    tr, ti = _cis(np.outer(k1, k2), n)
    f = lambda a: np.asarray(a, np.float32)
    return dict(n1=n1, sig=f(sig), ker=f(ker), inv=f(inv), f2=f(f2), f2inv=f(f2inv),
                twr=f(tr).reshape(n1, FFT_N2, 1), twi=f(ti).reshape(n1, FFT_N2, 1))


def _lin_rows(mat, rows):
    out = []
    for m in range(mat.shape[0]):
        acc = None
        for k in range(mat.shape[1]):
            cf = float(mat[m, k])
            if abs(cf) < 1e-9:
                continue
            term = rows[k] if abs(cf - 1.0) < 1e-9 else (-rows[k] if abs(cf + 1.0) < 1e-9 else cf * rows[k])
            acc = term if acc is None else acc + term
        out.append(jnp.zeros_like(rows[0]) if acc is None else acc)
    return out


def _twiddle_dft(slab_r, slab_i, twr, twi, f2):
    x = jnp.concatenate([slab_r * twr - slab_i * twi, slab_r * twi + slab_i * twr], axis=0)
    y = _dot3(f2, _split(x))
    return y[:FFT_N2], y[FFT_N2:]


def _filter_idft_twiddle(yr, yi, kr, ki, twr, twi, f2inv):
    z = jnp.concatenate([yr * kr - yi * ki, yr * ki + yi * kr], axis=0)
    w = _dot3(f2inv, _split(z))
    wr, wi = w[:FFT_N2], w[FFT_N2:]
    return wr * twr + wi * twi, wi * twr - wr * twi


def _fft1_kernel(m_ref, *refs):
    o_ref = refs[-1]
    m = _split(m_ref[...])
    n1 = o_ref.shape[1]
    for r in range(FFT_RT):
        parts = [x_ref[c:c + 32, r, :] for x_ref in refs[:-1] for c in range(0, x_ref.shape[0], 32)]
        x = jnp.concatenate(parts, axis=0) if len(parts) > 1 else parts[0]
        y = _dot3(m, _split(x))
        o_ref[0, :, r, :] = y[:n1]
        o_ref[1, :, r, :] = y[n1:]


def _fft_stage1(consts, mat, xs, lead_blocks):
    n1 = consts["n1"]
    rows = mat.shape[1] // len(xs)
    specs = [pl.BlockSpec((rows, FFT_RT, D), functools.partial(lambda j, lb: (lb, j, 0), lb=lb))
             for lb in lead_blocks]
    return pl.pallas_call(
        _fft1_kernel,
        grid=(FFT_N2 // FFT_RT,),
        in_specs=[pl.BlockSpec(mat.shape, lambda j: (0, 0))] + specs,
        out_specs=pl.BlockSpec((2, n1, FFT_RT, D), lambda j: (0, 0, j, 0)),
        out_shape=jax.ShapeDtypeStruct((2, n1, FFT_N2, D), F32),
        compiler_params=_params("arbitrary"),
        name="fft1_%d" % len(xs),
    )(jnp.asarray(mat), *xs)


def _fft2_kernel(a_ref, twr_ref, twi_ref, f2_ref, *rest, conv):
    twr, twi = twr_ref[...], twi_ref[...]
    yr, yi = _twiddle_dft(a_ref[0], a_ref[1], twr, twi, _split(f2_ref[...]))
    if conv:
        kf_ref, f2inv_ref, o_ref = rest
        yr, yi = _filter_idft_twiddle(yr, yi, kf_ref[0], kf_ref[1], twr, twi, _split(f2inv_ref[...]))
    else:
        o_ref, = rest
    o_ref[0] = yr
    o_ref[1] = yi


def _fft_stage2(consts, a, kf):
    n1 = consts["n1"]
    slab = pl.BlockSpec((2, None, FFT_N2, D), lambda k: (0, k, 0, 0))
    tw = pl.BlockSpec((None, FFT_N2, 1), lambda k: (k, 0, 0))
    mat = pl.BlockSpec((2 * FFT_N2, 2 * FFT_N2), lambda k: (0, 0))
    in_specs = [slab, tw, tw, mat]
    args = [a, jnp.asarray(consts["twr"]), jnp.asarray(consts["twi"]), jnp.asarray(consts["f2"])]
    if kf is not None:
        in_specs += [slab, mat]
        args += [kf, jnp.asarray(consts["f2inv"])]
    return pl.pallas_call(
        functools.partial(_fft2_kernel, conv=kf is not None),
        grid=(n1,),
        in_specs=in_specs,
        out_specs=slab,
        out_shape=jax.ShapeDtypeStruct(a.shape, F32),
        compiler_params=_params("arbitrary"),
        name="fft2_%s" % ("conv" if kf is not None else "spec"),
    )(*args)


def _fft3_kernel(m_ref, b_ref, p_ref, x0_ref, bias_ref, *rest):
    o_ref = rest[-1]
    m = _split(m_ref[...])
    bias = bias_ref[...]
    for r in range(FFT_RT):
        x = jnp.concatenate([b_ref[0, :, r, :], b_ref[1, :, r, :]], axis=0)
        y = _dot3(m, _split(x))
        o_ref[:, r, :] = (y + p_ref[:, r, :] * bias) * x0_ref[:, r, :]


def _fft_stage3(consts, b, pv, xv, bias, lead_block):
    n1 = consts["n1"]
    half = n1 // 2
    seq = pl.BlockSpec((half, FFT_RT, D), lambda j, s: (lead_block + s, j, 0))
    in_specs = [pl.BlockSpec((None, half, 2 * n1), lambda j, s: (s, 0, 0)),
                pl.BlockSpec((2, n1, FFT_RT, D), lambda j, s: (0, 0, j, 0)),
                seq, seq, pl.BlockSpec((1, D), lambda j, s: (0, 0))]
    return pl.pallas_call(
        _fft3_kernel,
        grid=(FFT_N2 // FFT_RT, 2),
        in_specs=in_specs,
        out_specs=pl.BlockSpec((half, FFT_RT, D), lambda j, s: (s, j, 0)),
        out_shape=jax.ShapeDtypeStruct((n1, FFT_N2, D), F32),
        compiler_params=_params("arbitrary", "arbitrary"),
        name="fft3",
    )(jnp.asarray(consts["inv"]), b, pv, xv, bias.reshape(1, D))


def _fft_ctx_kernel(p_ref, x0_ref, kern_ref, twr_ref, twi_ref, f2_ref, f2inv_ref, bias_ref, o_ref, kf_ref,
                    *, n1, sig, ker, inv):
    f2 = _split(f2_ref[...])

    @pl.when(pl.program_id(0) == 0)
    def _():
        a = _lin_rows(ker, [kern_ref[r] for r in range(n1)])
        for k in range(n1):
            kf_ref[0, k], kf_ref[1, k] = _twiddle_dft(a[k], a[n1 + k], twr_ref[k], twi_ref[k], f2)

    f2inv = _split(f2inv_ref[...])
    a = _lin_rows(sig, [p_ref[r] for r in range(n1)])
    br, bi = [], []
    for k in range(n1):
        twr, twi = twr_ref[k], twi_ref[k]
        yr, yi = _twiddle_dft(a[k], a[n1 + k], twr, twi, f2)
        wr, wi = _filter_idft_twiddle(yr, yi, kf_ref[0, k], kf_ref[1, k], twr, twi, f2inv)
        br.append(wr)
        bi.append(wi)
    out = _lin_rows(inv, br + bi)
    bias = bias_ref[...]
    for m, y in enumerate(out):
        o_ref[m] = (y + p_ref[m] * bias) * x0_ref[m]


def _fft_ctx(consts, pv, xv, kern, bias):
    n1 = consts["n1"]
    inv = np.concatenate([consts["inv"][0], consts["inv"][1]], axis=0)
    pair = pl.BlockSpec((n1, FFT_N2, D), lambda q: (q, 0, 0))
    full = lambda shp: pl.BlockSpec(shp, lambda q: (0,) * len(shp))
    return pl.pallas_call(
        functools.partial(_fft_ctx_kernel, n1=n1, sig=consts["sig"], ker=consts["ker"], inv=inv),
        grid=(NCTX // 2,),
        in_specs=[pair, pair, full((n1, FFT_N2, D)), full((n1, FFT_N2, 1)), full((n1, FFT_N2, 1)),
                  full((2 * FFT_N2, 2 * FFT_N2)), full((2 * FFT_N2, 2 * FFT_N2)), full((1, D))],
        out_specs=pair,
        out_shape=jax.ShapeDtypeStruct((TOK_CTX // FFT_N2, FFT_N2, D), F32),
        scratch_shapes=[pltpu.VMEM((2, n1, FFT_N2, D), F32)],
        compiler_params=_params("arbitrary"),
        name="fft_ctx",
    )(pv, xv, kern, jnp.asarray(consts["twr"]), jnp.asarray(consts["twi"]), jnp.asarray(consts["f2"]),
      jnp.asarray(consts["f2inv"]), bias.reshape(1, D))


def _hy_long_conv(p, x0, filt_w, bias):
    slabs = (NTOK // FFT_N2, FFT_N2, D)
    pv, xv = p.reshape(slabs), x0.reshape(slabs)
    cc = _fft_consts(LCTX)
    y_ctx = _fft_ctx(cc, pv, xv, _hy_filter(LCTX, *filt_w).reshape(cc["n1"], FFT_N2, D), bias)
    lc = _fft_consts(LLAT)
    n1 = lc["n1"]
    kern = _hy_filter(LLAT, *filt_w).reshape(n1, FFT_N2, D)
    kf = _fft_stage2(lc, _fft_stage1(lc, lc["ker"], [kern], [0]), None)
    lat0 = TOK_CTX // FFT_N2 // (n1 // 2)
    a = _fft_stage1(lc, lc["sig"], [pv, pv], [lat0, lat0 + 1])
    b = _fft_stage2(lc, a, kf)
    y_lat = _fft_stage3(lc, b, pv, xv, bias, lat0)
    return y_ctx.reshape(TOK_CTX, D), y_lat.reshape(TOK_LAT, D)


FFN_TM = 1024
FFN_TF = 256
FFN_HALO = 128


FFN_SR = 32


FFN_CH = 256
FFN_NF = D_FF // FFN_TF
FFN_TILES = NTOK // FFN_TM * FFN_NF


def _ffn_tile(t):
    return jnp.minimum(t, FFN_TILES - 1) // FFN_NF, t % FFN_NF


def _ffn_act_kernel(xc_ref, xp_ref, xn_ref, g_ref, sh_ref, sc_ref, wg_ref, wu_ref, cw_ref, cb_ref,
                    o_ref, h_ref, *slots):
    t = pl.program_id(0)
    i, f = _ffn_tile(t)
    ext = FFN_TM + 2 * FFN_HALO
    is_lat = i >= TOK_CTX // FFN_TM

    @pl.when(t == 0)
    def _():
        for scr in slots[4:]:
            scr[...] = jnp.zeros(scr.shape, F32)

    @pl.when(f == 0)
    def _():
        def norm(x):
            y = x * lax.rsqrt(jnp.mean(x * x, axis=-1, keepdims=True) + EPS) * g_ref[...]
            return (y * (1.0 + sc_ref[...]) + sh_ref[...]).astype(BF16)
        seq_pos = (i * FFN_TM - TOK_CTX) & (LLAT - 1)
        keep_prev = jnp.logical_and(is_lat, seq_pos != 0)
        keep_next = jnp.logical_and(is_lat, seq_pos != LLAT - FFN_TM)
        h_ref[0:FFN_HALO, :] = jnp.where(keep_prev, norm(xp_ref[...]), 0.0).astype(BF16)
        h_ref[FFN_HALO:FFN_HALO + FFN_TM, :] = norm(xc_ref[...])
        h_ref[FFN_HALO + FFN_TM:ext, :] = jnp.where(keep_next, norm(xn_ref[...]), 0.0).astype(BF16)

    def project_gate(scr, lo, hi):
        gate_scr, left_scr, right_scr, _ = scr
        gate = _dot(h_ref[lo:hi, :], wg_ref[...])
        period = jnp.where(is_lat, GRID_W, LCTX)
        r = lax.broadcasted_iota(jnp.int32, (hi - lo, 1), 0)
        col = (r + lo + FFN_HALO) & (period - 1)
        gate_scr[lo:hi, :] = gate
        left_scr[lo:hi, :] = jnp.where(col == 0, 0.0, pltpu.roll(gate, 1, 0))
        right_scr[lo:hi, :] = jnp.where(col == period - 1, 0.0, pltpu.roll(gate, hi - lo - 1, 0))

    def project_up(scr, lo, hi):
        scr[3][lo - FFN_HALO:hi - FFN_HALO, :] = _dot(h_ref[lo:hi, :], wu_ref[...])

    def convolve(scr, strips):
        gate_scr, left_scr, right_scr, up_scr = scr
        ip, _ = _ffn_tile(jnp.maximum(t - 1, 0))
        kidx = lax.broadcasted_iota(jnp.int32, (9, 1), 0)
        mid_row = jnp.logical_and(kidx >= 3, kidx < 6)
        w = jnp.where(jnp.logical_or(ip >= TOK_CTX // FFN_TM, mid_row), cw_ref[...], 0.0)
        bias = cb_ref[...]
        for s in strips:
            z = bias
            for k in range(3):
                lo = s * FFN_SR + FFN_HALO + (k - 1) * GRID_W
                z = z + left_scr[lo:lo + FFN_SR, :] * w[3 * k:3 * k + 1] \
                    + gate_scr[lo:lo + FFN_SR, :] * w[3 * k + 1:3 * k + 2] \
                    + right_scr[lo:lo + FFN_SR, :] * w[3 * k + 2:3 * k + 3]
            out = slice(s * FFN_SR, (s + 1) * FFN_SR)
            o_ref[out, :] = ((0.5 * z) * (1.0 + jnp.tanh(0.5 * z)) * up_scr[out, :]).astype(BF16)

    def step(cur, prev):
        edges = [0] + list(range(FFN_HALO, FFN_HALO + FFN_TM + 1, FFN_CH)) + [ext]
        pieces = []
        for lo, hi in zip(edges[:-1], edges[1:]):
            pieces.append(functools.partial(project_gate, cur, lo, hi))
            if FFN_HALO <= lo < FFN_HALO + FFN_TM:
                pieces.append(functools.partial(project_up, cur, lo, hi))
        nst = FFN_TM // FFN_SR
        for c, piece in enumerate(pieces):
            piece()
            convolve(prev, range(c * nst // len(pieces), (c + 1) * nst // len(pieces)))

    pl.when(t % 2 == 0)(lambda: step(slots[:4], slots[4:]))
    pl.when(t % 2 == 1)(lambda: step(slots[4:], slots[:4]))


def _ffn_act(x, gain, shift, scale, w_gu, conv_w, conv_b):
    hb = FFN_TM // FFN_HALO
    nhb = NTOK // FFN_HALO
    ext = FFN_TM + 2 * FFN_HALO
    row = lambda t: _ffn_tile(t)[0]
    col = lambda t: _ffn_tile(t)[1]
    prev = lambda t: _ffn_tile(jnp.maximum(t - 1, 0))
    vec = pl.BlockSpec((None, 1, D), lambda t: (row(t) * FFN_TM // ROWBLK, 0, 0))
    in_specs = [pl.BlockSpec((FFN_TM, D), lambda t: (row(t), 0)),
                pl.BlockSpec((FFN_HALO, D), lambda t: (jnp.maximum(row(t) * hb - 1, 0), 0)),
                pl.BlockSpec((FFN_HALO, D), lambda t: (jnp.minimum((row(t) + 1) * hb, nhb - 1), 0)),
                pl.BlockSpec((1, D), lambda t: (0, 0)), vec, vec,
                pl.BlockSpec((D, FFN_TF), lambda t: (0, col(t))),
                pl.BlockSpec((D, FFN_TF), lambda t: (0, FFN_NF + col(t))),
                pl.BlockSpec((9, FFN_TF), lambda t: (0, prev(t)[1])),
                pl.BlockSpec((1, FFN_TF), lambda t: (0, prev(t)[1]))]
    return pl.pallas_call(
        _ffn_act_kernel,
        grid=(FFN_TILES + 1,),
        in_specs=in_specs,
        out_specs=pl.BlockSpec((FFN_TM, FFN_TF), lambda t: prev(t)),
        out_shape=jax.ShapeDtypeStruct((NTOK, D_FF), BF16),
        scratch_shapes=[pltpu.VMEM((ext, D), BF16)]
        + 2 * (3 * [pltpu.VMEM((ext, FFN_TF), F32)] + [pltpu.VMEM((FFN_TM, FFN_TF), F32)]),
        compiler_params=_params("arbitrary"),
        name="ffn_act",
    )(x, x, x, gain.reshape(1, D), shift, scale, w_gu, w_gu, conv_w.reshape(9, D_FF), conv_b.reshape(1, D_FF))


def _ab_w_in_cols(w):
    qkv_w, gate_c0 = 3 * HW, 3 * HW + HW
    rest = jnp.concatenate([w[:, qkv_w:gate_c0], w[:, gate_c0 + 4 * HEADS:], w[:, gate_c0:gate_c0 + 4 * HEADS],
                            jnp.zeros((D, LANE - 4 * HEADS), F32)], axis=1)
    return w[:, :qkv_w].astype(BF16), rest.astype(BF16)


def _ab_mixer(qkv, proj, a_log, dt_bias, ret_decay, s_delta0, s_ret0):
    gates = _gates(proj, a_log, dt_bias)
    gates_t = gates[:, :2 * HEADS].reshape(NTOK // DP_RB, DP_NC, CHUNK, 2 * HEADS).transpose(0, 3, 1, 2)
    u, wq, ak, gl = _delta_prep(qkv, gates, gates_t)
    odf, odb, orf, orb, sfin, rfin = _ab_scan(u, wq, ak, gl, proj, ret_decay, s_delta0, s_ret0)
    return (odf, odb, orf, orb), sfin[:NCTX], rfin[:NCTX]


def kernel(x_prompt, x_sample, state_delta, state_ret, c, c_ctx, mod_w, mod_b, norm1, norm2, ab_w_in, ab_conv, ab_a_log, ab_dt_bias, ab_norm_a, ab_norm_b, ab_ret_decay, ab_w_out, hy_w_in, hy_b_in, hy_conv_w, hy_conv_b, hy_f_w1, hy_f_b1, hy_f_freq1, hy_f_w2, hy_f_b2, hy_f_freq2, hy_f_w3, hy_f_bias, hy_w_out, hy_b_out, ffn_w_gate, ffn_w_up, ffn_conv, ffn_conv_b, ffn_w_down, final_norm):
    x = jnp.concatenate([x_prompt.reshape(TOK_CTX, D), x_sample.reshape(TOK_LAT, D)], axis=0)
    cvec = jnp.concatenate([c_ctx[None], c, jnp.zeros((8 - 1 - NLAT, D), F32)], axis=0)
    mod = _mod_all(cvec, mod_w, mod_b)
    new_delta, new_ret = [], []
    for l in range(DEPTH):
        m = [mod[l, :NGROUP, k * D:(k + 1) * D].reshape(NGROUP, 1, D) for k in range(6)]
        j = l // 2
        if l % 2 == 0:
            w_qkv, w_rest = _ab_w_in_cols(ab_w_in[j])
            qkv = _ab_qkv(x, norm1[l], m[0], m[1], w_qkv, ab_conv[j])
            proj = _nmm(x, norm1[l], m[0], m[1], w_rest, None, AB_N // 3, "ab_in")
            heads, sd, sr = _ab_mixer(qkv, proj, ab_a_log[j], ab_dt_bias[j], ab_ret_decay[j],
                                      state_delta[:, j], state_ret[:, j])
            new_delta.append(sd)
            new_ret.append(sr)
            x = _ab_out(*heads, proj, ab_norm_a[j], ab_norm_b[j], ab_w_out, j, x, m[2])
        else:
            x0, p = _hy_in(x, norm1[l], m[0], m[1], hy_w_in[j].astype(BF16), hy_b_in[j], hy_conv_w[j], hy_conv_b[j])
            filt_w = (hy_f_w1[j], hy_f_b1[j], hy_f_freq1[j], hy_f_w2[j], hy_f_b2[j], hy_f_freq2[j], hy_f_w3[j])
            y = _hy_long_conv(p, x0, filt_w, hy_f_bias[j])
            x = _mmr(y, hy_w_out, j, hy_b_out[j], x, m[2], None, "hy_out")
        w_gu = jnp.concatenate([ffn_w_gate[l], ffn_w_up[l]], axis=1).astype(BF16)
        act = _ffn_act(x, norm2[l], m[3], m[4], w_gu, ffn_conv[l], ffn_conv_b[l])
        x = _mmr(act, ffn_w_down, l, None, x, m[5], final_norm if l == DEPTH - 1 else None, "ffn_out")
    y_prompt = x[:TOK_CTX].reshape(NCTX, LCTX, D)
    y_sample = x[TOK_CTX:].reshape(NLAT, LLAT, D)
    return (y_prompt, y_sample, jnp.stack(new_delta, axis=1), jnp.stack(new_ret, axis=1))
```

```python
import functools
import math

import numpy as np
import jax
import jax.numpy as jnp
from jax import lax
from jax.experimental import pallas as pl
from jax.experimental.pallas import tpu as pltpu

F32, BF16 = jnp.float32, jnp.bfloat16
HI = lax.Precision.HIGHEST

D = 1024
NCTX, LCTX = 16, 256
NLAT, LLAT = 2, 4096
DEPTH = 4
TOK_CTX = NCTX * LCTX
TOK_LAT = NLAT * LLAT
NTOK = TOK_CTX + TOK_LAT
ROWBLK = 4096
NGROUP = NTOK // ROWBLK
GRID_W = 64
CHUNK = 64
NCH = NTOK // CHUNK
EPS = 1e-6
HEADS = 4
DK = 128
HW = HEADS * DK
LANE = 128
D_FF = 2816
HY_EMB = 33
HY_BANDS = 16
HY_FW = 64
HY_TARGET = 1e-2
HY_FAST_PCT = 0.3
HY_SLOW_PCT = 1.5
FFT_N2 = 128
VMEM_LIMIT = 52 * 1024 * 1024

CB_QA, CB_KA, CB_VA = 0, 4, 8
CB_ZA, CB_QB, CB_KB, CB_VB, CB_GB, CB_GATE = 0, 4, 8, 12, 16, 20
AB_N = 21 * LANE


def _params(*sem):
    return pltpu.CompilerParams(dimension_semantics=sem, vmem_limit_bytes=VMEM_LIMIT)


def _silu(x):
    return x * jax.nn.sigmoid(x)


def _dot(a, b, precision=None):
    return jnp.dot(a, b, preferred_element_type=F32, precision=precision)


def _dot_nt(a, b):
    return lax.dot_general(a, b, (((1,), (1,)), ((), ())), preferred_element_type=F32)


def _dot_tn(a, b):
    return lax.dot_general(a, b, (((0,), (0,)), ((), ())), preferred_element_type=F32)


def _pick_lane(x, lane):
    li = lax.broadcasted_iota(jnp.int32, x.shape, 1)
    return jnp.sum(jnp.where(li == lane, x, 0.0), axis=-1, keepdims=True)


def _mod_kernel(c_ref, w_ref, b_ref, o_ref):
    s = _silu(c_ref[...])
    o_ref[...] = _dot(s.astype(BF16), w_ref[...].astype(BF16)) + b_ref[...]


def _mod_all(cvec, mod_w, mod_b):
    tn = 1536
    return pl.pallas_call(
        _mod_kernel,
        grid=(DEPTH, 6 * D // tn),
        in_specs=[pl.BlockSpec((8, D), lambda l, j: (0, 0)),
                  pl.BlockSpec((None, D, tn), lambda l, j: (l, 0, j)),
                  pl.BlockSpec((None, 1, tn), lambda l, j: (l, 0, j))],
        out_specs=pl.BlockSpec((None, 8, tn), lambda l, j: (l, 0, j)),
        out_shape=jax.ShapeDtypeStruct((DEPTH, 8, 6 * D), F32),
        compiler_params=_params("arbitrary", "arbitrary"),
        name="mod",
    )(cvec, mod_w, mod_b.reshape(DEPTH, 1, 6 * D))


NMM_TM = 1024


def _nmm_kernel(x_ref, g_ref, sh_ref, sc_ref, w_ref, *rest, has_bias):
    if has_bias:
        b_ref, o_ref, h_ref = rest
    else:
        o_ref, h_ref = rest

    @pl.when(pl.program_id(1) == 0)
    def _():
        x = x_ref[...]
        y = x * lax.rsqrt(jnp.mean(x * x, axis=-1, keepdims=True) + EPS) * g_ref[...]
        h_ref[...] = (y * (1.0 + sc_ref[...]) + sh_ref[...]).astype(BF16)

    acc = _dot(h_ref[...], w_ref[...])
    if has_bias:
        acc = acc + b_ref[...]
    o_ref[...] = acc


def _nmm(x, gain, shift, scale, w, bias, tn, name):
    n = w.shape[1]
    grp = lambda i, j: (i * NMM_TM // ROWBLK, 0, 0)
    in_specs = [pl.BlockSpec((NMM_TM, D), lambda i, j: (i, 0)),
                pl.BlockSpec((1, D), lambda i, j: (0, 0)),
                pl.BlockSpec((None, 1, D), grp),
                pl.BlockSpec((None, 1, D), grp),
                pl.BlockSpec((D, tn), lambda i, j: (0, j))]
    args = [x, gain.reshape(1, D), shift, scale, w]
    if bias is not None:
        in_specs.append(pl.BlockSpec((1, tn), lambda i, j: (0, j)))
        args.append(bias.reshape(1, n))
    return pl.pallas_call(
        functools.partial(_nmm_kernel, has_bias=bias is not None),
        grid=(NTOK // NMM_TM, n // tn),
        in_specs=in_specs,
        out_specs=pl.BlockSpec((NMM_TM, tn), lambda i, j: (i, j)),
        out_shape=jax.ShapeDtypeStruct((NTOK, n), F32),
        scratch_shapes=[pltpu.VMEM((NMM_TM, D), BF16)],
        compiler_params=_params("arbitrary", "arbitrary"),
        name=name,
    )(*args)


MMR_TM = 512


def _mmr_kernel(*refs, split, has_bias, final):
    rest = list(refs)
    a_refs = [rest.pop(0) for _ in range(2 if split else 1)]
    w_ref = rest.pop(0)
    b_ref = rest.pop(0) if has_bias else None
    x_ref, gate_ref = rest.pop(0), rest.pop(0)
    fn_ref = rest.pop(0) if final else None
    o_ref, wb_ref = rest
    i = pl.program_id(0)

    @pl.when(i == 0)
    def _():
        wb_ref[...] = w_ref[...].astype(BF16)

    a = a_refs[0][...]
    if split:
        a = jnp.where(i < TOK_CTX // MMR_TM, a, a_refs[1][...])
    acc = _dot(a.astype(BF16), wb_ref[...])
    if has_bias:
        acc = acc + b_ref[...]
    y = x_ref[...] + gate_ref[...] * acc
    if final:
        y = y * lax.rsqrt(jnp.mean(y * y, axis=-1, keepdims=True) + EPS) * fn_ref[...]
    o_ref[...] = y


def _mmr(a, w_all, layer, bias, x, gate, final_gain, name):
    grp = lambda i: (i * MMR_TM // ROWBLK, 0, 0)
    split = isinstance(a, tuple)
    nctx = TOK_CTX // MMR_TM
    if split:
        k = a[0].shape[1]
        in_specs = [pl.BlockSpec((MMR_TM, k), lambda i: (jnp.minimum(i, nctx - 1), 0)),
                    pl.BlockSpec((MMR_TM, k), lambda i: (jnp.maximum(i - nctx, 0), 0))]
        args = list(a)
    else:
        k = a.shape[1]
        in_specs = [pl.BlockSpec((MMR_TM, k), lambda i: (i, 0))]
        args = [a]
    in_specs.append(pl.BlockSpec((None, k, D), lambda i: (layer, 0, 0)))
    args.append(w_all)
    if bias is not None:
        in_specs.append(pl.BlockSpec((1, D), lambda i: (0, 0)))
        args.append(bias.reshape(1, D))
    in_specs += [pl.BlockSpec((MMR_TM, D), lambda i: (i, 0)), pl.BlockSpec((None, 1, D), grp)]
    args += [x, gate]
    if final_gain is not None:
        in_specs.append(pl.BlockSpec((1, D), lambda i: (0, 0)))
        args.append(final_gain.reshape(1, D))
    return pl.pallas_call(
        functools.partial(_mmr_kernel, split=split, has_bias=bias is not None, final=final_gain is not None),
        grid=(NTOK // MMR_TM,),
        in_specs=in_specs,
        out_specs=pl.BlockSpec((MMR_TM, D), lambda i: (i, 0)),
        out_shape=jax.ShapeDtypeStruct((NTOK, D), F32),
        scratch_shapes=[pltpu.VMEM((k, D), BF16)],
        compiler_params=_params("arbitrary"),
        name=name,
    )(*args)


PC_TM = 1024
PC_HALO = 16
PC_TN = 256


def _pc_norm_rows(i, xc_ref, xp_ref, xn_ref, g_ref, sh_ref, sc_ref, h_ref):
    def norm(x):
        y = x * lax.rsqrt(jnp.mean(x * x, axis=-1, keepdims=True) + EPS) * g_ref[...]
        return (y * (1.0 + sc_ref[...]) + sh_ref[...]).astype(BF16)
    h_ref[0:PC_HALO, :] = norm(xp_ref[...])
    h_ref[PC_HALO:PC_HALO + PC_TM, :] = norm(xc_ref[...])
    h_ref[PC_HALO + PC_TM:PC_TM + 2 * PC_HALO, :] = norm(xn_ref[...])


def _pc_proj_conv(i, h_ref, w_ref, b_ref, cw_ref):
    ext = PC_TM + 2 * PC_HALO
    u = _dot(h_ref[...], w_ref[...])
    if b_ref is not None:
        u = u + b_ref[...]
    lseq = jnp.where(i < TOK_CTX // PC_TM, LCTX, LLAT)
    pos = (i * PC_TM - PC_HALO + lax.broadcasted_iota(jnp.int32, (ext, 1), 0)) & (lseq - 1)
    prev = jnp.where(pos == 0, 0.0, pltpu.roll(u, 1, 0))
    nxt = jnp.where(pos == lseq - 1, 0.0, pltpu.roll(u, ext - 1, 0))
    cw = cw_ref[...]
    y = prev * cw[0:1] + u * cw[1:2] + nxt * cw[2:3]
    return y[PC_HALO:PC_HALO + PC_TM]


def _pc_specs():
    hb = PC_TM // PC_HALO
    nhb = NTOK // PC_HALO
    vec = pl.BlockSpec((None, 1, D), lambda i, j: (i * PC_TM // ROWBLK, 0, 0))
    return [pl.BlockSpec((PC_TM, D), lambda i, j: (i, 0)),
            pl.BlockSpec((PC_HALO, D), lambda i, j: (jnp.maximum(i * hb - 1, 0), 0)),
            pl.BlockSpec((PC_HALO, D), lambda i, j: (jnp.minimum((i + 1) * hb, nhb - 1), 0)),
            pl.BlockSpec((1, D), lambda i, j: (0, 0)), vec, vec]


def _hy_in_kernel(xc_ref, xp_ref, xn_ref, g_ref, sh_ref, sc_ref, w0_ref, w1_ref, wv_ref, b0_ref, b1_ref, bv_ref,
                  c0_ref, c1_ref, cv_ref, cb0_ref, cb1_ref, cbv_ref, x0_ref, p_ref, h_ref):
    i = pl.program_id(0)

    @pl.when(pl.program_id(1) == 0)
    def _():
        _pc_norm_rows(i, xc_ref, xp_ref, xn_ref, g_ref, sh_ref, sc_ref, h_ref)

    x0_ref[...] = _pc_proj_conv(i, h_ref, w0_ref, b0_ref, c0_ref) + cb0_ref[...]
    x1 = _pc_proj_conv(i, h_ref, w1_ref, b1_ref, c1_ref) + cb1_ref[...]
    v = _pc_proj_conv(i, h_ref, wv_ref, bv_ref, cv_ref) + cbv_ref[...]
    p_ref[...] = v * x1


def _hy_in(x, gain, shift, scale, w, b, conv_w, conv_b):
    nj = D // PC_TN
    wsp = lambda k: pl.BlockSpec((D, PC_TN), lambda i, j: (0, k * nj + j))
    row = lambda r, k: pl.BlockSpec((r, PC_TN), lambda i, j: (0, k * nj + j))
    out = pl.BlockSpec((PC_TM, PC_TN), lambda i, j: (i, j))
    b2, cb2 = b.reshape(1, 3 * D), conv_b.reshape(1, 3 * D)
    return pl.pallas_call(
        _hy_in_kernel,
        grid=(NTOK // PC_TM, nj),
        in_specs=_pc_specs() +[wsp(0), wsp(1), wsp(2), row(1, 0), row(1, 1), row(1, 2),
                                 row(3, 0), row(3, 1), row(3, 2), row(1, 0), row(1, 1), row(1, 2)],
        out_specs=[out, out],
        out_shape=[jax.ShapeDtypeStruct((NTOK, D), F32)] * 2,
        scratch_shapes=[pltpu.VMEM((PC_TM + 2 * PC_HALO, D), BF16)],
        compiler_params=_params("arbitrary", "arbitrary"),
        name="hy_in",
    )(x, x, x, gain.reshape(1, D), shift, scale, w, w, w, b2, b2, b2, conv_w, conv_w, conv_w, cb2, cb2, cb2)


def _ab_qkv_kernel(xc_ref, xp_ref, xn_ref, g_ref, sh_ref, sc_ref, w_ref, cw_ref, o_ref, h_ref):
    i, j = pl.program_id(0), pl.program_id(1)

    @pl.when(j == 0)
    def _():
        _pc_norm_rows(i, xc_ref, xp_ref, xn_ref, g_ref, sh_ref, sc_ref, h_ref)

    y = _silu(_pc_proj_conv(i, h_ref, w_ref, None, cw_ref))
    is_qk = j < 2 * HW // PC_TN
    qscale = jnp.where(j < HW // PC_TN, DK ** -0.5, 1.0)
    for blk in range(PC_TN // DK):
        cols = slice(blk * DK, (blk + 1) * DK)
        yb = y[:, cols]
        nrm = yb * (lax.rsqrt(jnp.sum(yb * yb, axis=-1, keepdims=True) + EPS) * qscale)
        o_ref[:, cols] = jnp.where(is_qk, nrm, yb)


def _ab_qkv(x, gain, shift, scale, w, conv_w):
    n = 3 * HW
    return pl.pallas_call(
        _ab_qkv_kernel,
        grid=(NTOK // PC_TM, n // PC_TN),
        in_specs=_pc_specs() +[pl.BlockSpec((D, PC_TN), lambda i, j: (0, j)),
                                 pl.BlockSpec((3, PC_TN), lambda i, j: (0, j))],
        out_specs=pl.BlockSpec((PC_TM, PC_TN), lambda i, j: (i, j)),
        out_shape=jax.ShapeDtypeStruct((NTOK, n), F32),
        scratch_shapes=[pltpu.VMEM((PC_TM + 2 * PC_HALO, D), BF16)],
        compiler_params=_params("arbitrary", "arbitrary"),
        name="ab_qkv",
    )(x, x, x, gain.reshape(1, D), shift, scale, w, conv_w)


def _gates_kernel(x_ref, alog_ref, dtb_ref, o_ref):
    x = x_ref[...]
    lane = lax.broadcasted_iota(jnp.int32, x.shape, 1)
    row = lax.broadcasted_iota(jnp.int32, x.shape, 0) & (CHUNK - 1)
    t = x + dtb_ref[...]
    softplus = jnp.maximum(t, 0.0) + jnp.log(1.0 + jnp.exp(-jnp.abs(t)))
    g = -jnp.exp(alog_ref[...]) * softplus
    pre, suf = g, g
    s = 1
    while s < CHUNK:
        pre = pre + jnp.where(row >= s, pltpu.roll(pre, s, 0), 0.0)
        suf = suf + jnp.where(row < CHUNK - s, pltpu.roll(suf, ROWBLK - s, 0), 0.0)
        s *= 2
    gc = jnp.where(lane < HEADS, pre, suf)
    o_ref[...] = jnp.where(lane < 2 * HEADS, gc, jax.nn.sigmoid(x))


def _gates(proj, a_log, dt_bias):
    pad = lambda v: jnp.pad(v.reshape(1, 2 * HEADS), ((0, 0), (0, LANE - 2 * HEADS)))
    return pl.pallas_call(
        _gates_kernel,
        grid=(NGROUP,),
        in_specs=[pl.BlockSpec((ROWBLK, LANE), lambda i: (i, CB_GATE)),
                  pl.BlockSpec((1, LANE), lambda i: (0, 0)),
                  pl.BlockSpec((1, LANE), lambda i: (0, 0))],
        out_specs=pl.BlockSpec((ROWBLK, LANE), lambda i: (i, 0)),
        out_shape=jax.ShapeDtypeStruct((NTOK, LANE), F32),
        compiler_params=_params("arbitrary"),
        name="ab_gates",
    )(proj, pad(a_log), pad(dt_bias))


DP_RB = 512
DP_NC = DP_RB // CHUNK
TRI_BASE = 8


def _split(x):
    hi = x.astype(BF16)
    return hi, (x - hi.astype(F32)).astype(BF16)


def _dot3(a, b):
    return _dot(a[0], b[0]) + _dot(a[0], b[1]) + _dot(a[1], b[0])


def _unit_tri_inv_batch(lms):
    ii = lax.broadcasted_iota(jnp.int32, (CHUNK, CHUNK), 0)
    jj = lax.broadcasted_iota(jnp.int32, (CHUNK, CHUNK), 1)
    same = lambda b: (ii >> int(math.log2(b))) == (jj >> int(math.log2(b)))
    eye = jnp.where(ii == jj, 1.0, 0.0)
    qs = [jnp.where(same(TRI_BASE), -lm, 0.0) for lm in lms]
    ps = [eye + q for q in qs]
    qs = [_split(q) for q in qs]
    for _ in range(int(math.log2(TRI_BASE)) - 1):
        qs = [_split(_dot3(q, q)) for q in qs]
        ps = [p + _dot3(_split(p), q) for p, q in zip(ps, qs)]
    b = TRI_BASE
    while b < CHUNK:
        off = jnp.logical_and(same(2 * b), jnp.logical_not(same(b)))
        pbs = [p.astype(BF16) for p in ps]
        ts = [_dot(p, jnp.where(off, lm, 0.0).astype(BF16)) for p, lm in zip(pbs, lms)]
        ps = [p - _dot(t.astype(BF16), p2) for p, p2, t in zip(ps, pbs, ts)]
        b *= 2
    return ps


def _delta_prep_kernel(q_ref, k_ref, v_ref, g_ref, gt_ref, u_ref, wq_ref, ak_ref, gl_ref):
    h = pl.program_id(1)
    ii = lax.broadcasted_iota(jnp.int32, (CHUNK, CHUNK), 0)
    jj = lax.broadcasted_iota(jnp.int32, (CHUNK, CHUNK), 1)
    probs = []
    for c in range(DP_NC):
        rows = slice(c * CHUNK, (c + 1) * CHUNK)
        q, k, v, gts = q_ref[rows, :], k_ref[rows, :], v_ref[rows, :], g_ref[rows, :]
        kbf = k.astype(BF16)
        qk = _dot_nt(q.astype(BF16), kbf)
        for d in range(2):
            incl = (ii >= jj) if d == 0 else (ii <= jj)
            strict = (ii > jj) if d == 0 else (ii < jj)
            gcol = _pick_lane(gts, d * HEADS + h)
            bcol = _pick_lane(gts, 2 * HEADS + d * HEADS + h)
            grow = gt_ref[d * HEADS + h][c:c + 1, :]
            dmask = jnp.where(incl, jnp.exp(jnp.where(incl, gcol - grow, 0.0)), 0.0)
            kb = k * bcol
            lm = jnp.where(strict, _dot_nt(kb.astype(BF16), kbf) * dmask, 0.0)
            probs.append((c, d, rows, q, k, v, kb, gcol, bcol, dmask, qk, incl, lm))
    tmats = _unit_tri_inv_batch([p[-1] for p in probs])
    for (c, d, rows, q, k, v, kb, gcol, bcol, dmask, qk, incl, _), tmat in zip(probs, tmats):
        gam = jnp.exp(gcol)
        rhs = jnp.concatenate([v * bcol, kb * gam], axis=1).astype(BF16)
        uw = _dot(tmat.astype(BF16), rhs)
        attn = jnp.where(incl, qk * dmask, 0.0)
        gtot = gcol[CHUNK - 1:CHUNK, :] if d == 0 else gcol[0:1, :]
        kd = k * jnp.exp(gtot - gcol)
        u_ref[d, rows, :] = uw[:, :DK]
        wq_ref[d, c] = jnp.concatenate([uw[:, DK:], q * gam], axis=0).astype(BF16)
        ak_ref[d, c] = jnp.concatenate([attn, kd.T], axis=0).astype(BF16)
        gl_ref[d, c] = jnp.broadcast_to(jnp.exp(gtot), (1, LANE))


def _delta_prep(qkv, gates, gates_t):
    nrb = NTOK // DP_RB
    blk = lambda cb: pl.BlockSpec((DP_RB, LANE), lambda i, h: (i, cb + h))
    return pl.pallas_call(
        _delta_prep_kernel,
        grid=(nrb, HEADS),
        in_specs=[blk(CB_QA), blk(CB_KA), blk(CB_VA),
                  pl.BlockSpec((DP_RB, LANE), lambda i, h: (i, 0)),
                  pl.BlockSpec((None, 2 * HEADS, DP_NC, CHUNK), lambda i, h: (i, 0, 0, 0))],
        out_specs=[pl.BlockSpec((2, DP_RB, LANE), lambda i, h: (0, i, h)),
                   pl.BlockSpec((2, None, DP_NC, 2 * CHUNK, DK), lambda i, h: (0, h, i, 0, 0)),
                   pl.BlockSpec((2, None, DP_NC, CHUNK + DK, CHUNK), lambda i, h: (0, h, i, 0, 0)),
                   pl.BlockSpec((2, None, DP_NC, 1, LANE), lambda i, h: (0, h, i, 0, 0))],
        out_shape=[jax.ShapeDtypeStruct((2, NTOK, HW), F32),
                   jax.ShapeDtypeStruct((2, HEADS, NCH, 2 * CHUNK, DK), BF16),
                   jax.ShapeDtypeStruct((2, HEADS, NCH, CHUNK + DK, CHUNK), BF16),
                   jax.ShapeDtypeStruct((2, HEADS, NCH, 1, LANE), F32)],
        compiler_params=_params("arbitrary", "arbitrary"),
        name="delta_prep",
    )(qkv, qkv, qkv, gates, gates_t)


SCAN_G = LCTX // CHUNK
SCAN_RB = SCAN_G * CHUNK
SCAN_CTX_STEPS = TOK_CTX // SCAN_RB
SCAN_LAT_STEPS = LLAT // SCAN_RB
assert SCAN_RB == LCTX and SCAN_CTX_STEPS % SCAN_LAT_STEPS == 0


def _scan_bwd_group(t):
    i = t % SCAN_LAT_STEPS
    return jnp.where(t < SCAN_CTX_STEPS, t, t - i + (SCAN_LAT_STEPS - 1 - i))


def _scan_seq(t):
    return jnp.where(t < SCAN_CTX_STEPS, t, NCTX + (t - SCAN_CTX_STEPS) // SCAN_LAT_STEPS)


def _ab_scan_kernel(uf_ref, ub_ref, wqf_ref, wqb_ref, akf_ref, akb_ref, glf_ref, glb_ref,
                    qf_ref, kf_ref, vf_ref, qb_ref, kb_ref, vb_ref, dec_ref, s0_ref, r0_ref,
                    odf_ref, odb_ref, orf_ref, orb_ref, sfin_ref, rfin_ref, s_scr, r_scr):
    t = pl.program_id(0)
    in_ctx = t < SCAN_CTX_STEPS
    first = jnp.logical_or(in_ctx, t % SCAN_LAT_STEPS == 0)
    last = jnp.logical_or(in_ctx, t % SCAN_LAT_STEPS == SCAN_LAT_STEPS - 1)

    @pl.when(first)
    def _():
        s_scr[...] = jnp.where(in_ctx, 0.0, s0_ref[...])
        r_scr[...] = jnp.where(in_ctx, 0.0, r0_ref[...])

    dirs = ((uf_ref, wqf_ref, akf_ref, glf_ref, qf_ref, kf_ref, vf_ref, odf_ref, orf_ref),
            (ub_ref, wqb_ref, akb_ref, glb_ref, qb_ref, kb_ref, vb_ref, odb_ref, orb_ref))

    ii = lax.broadcasted_iota(jnp.int32, (SCAN_RB, SCAN_RB), 0)
    jj = lax.broadcasted_iota(jnp.int32, (SCAN_RB, SCAN_RB), 1)
    ci = lax.broadcasted_iota(jnp.int32, (SCAN_RB, 1), 0).astype(F32)
    for d in range(2):
        q_ref, k_ref, v_ref, or_ref = dirs[d][4], dirs[d][5], dirs[d][6], dirs[d][8]
        dist = (ii - jj) if d == 0 else (jj - ii)
        for h in range(HEADS):
            cols = slice(h * DK, (h + 1) * DK)
            lane = d * HEADS + h
            lg = -jnp.exp(dec_ref[:, lane:lane + 1])
            dmat = jnp.where(dist >= 0, jnp.exp(jnp.maximum(dist, 0).astype(F32) * lg), 0.0)
            cross = jnp.exp(((ci + 1.0) if d == 0 else (SCAN_RB - ci)) * lg)
            sdec = jnp.exp(((SCAN_RB - 1.0 - ci) if d == 0 else ci) * lg)
            r = r_scr[d, h]
            q = q_ref[:, cols] * DK ** -0.5
            k = k_ref[:, cols]
            vb = v_ref[:, cols].astype(BF16)
            inner = _dot_nt(q.astype(BF16), k.astype(BF16)) * dmat
            or_ref[:, cols] = _dot(inner.astype(BF16), vb) + _dot((q * cross).astype(BF16), r.astype(BF16))
            r_scr[d, h] = r * jnp.exp(SCAN_RB * lg) + _dot_tn((k * sdec).astype(BF16), vb)

    s_cur = {(d, h): s_scr[d, h] for d in range(2) for h in range(HEADS)}
    for c in range(SCAN_G):
        for d in range(2):
            u_ref, wq_ref, ak_ref, gl_ref, q_ref, k_ref, v_ref, od_ref, or_ref = dirs[d]
            cc = c if d == 0 else SCAN_G - 1 - c
            rows = slice(cc * CHUNK, (cc + 1) * CHUNK)
            for h in range(HEADS):
                cols = slice(h * DK, (h + 1) * DK)
                s = s_cur[d, h]
                r1 = _dot(wq_ref[h, cc], s.astype(BF16))
                vnew = u_ref[rows, cols] - r1[:CHUNK]
                r2 = _dot(ak_ref[h, cc], vnew.astype(BF16))
                od_ref[rows, cols] = r1[CHUNK:] + r2[:CHUNK]
                s_cur[d, h] = s * gl_ref[h, cc] + r2[CHUNK:]
    for d in range(2):
        for h in range(HEADS):
            s_scr[d, h] = s_cur[d, h]

    @pl.when(last)
    def _():
        sfin_ref[...] = s_scr[...]
        rfin_ref[...] = r_scr[...]


def _ab_scan(u, wq, ak, gl, proj, ret_decay, s0, r0):
    nseq = NCTX + NLAT
    fwd = lambda t: t
    row = lambda g, cb: pl.BlockSpec((SCAN_RB, HW), lambda t: (g(t), cb))
    u_spec = lambda d, g: pl.BlockSpec((None, SCAN_RB, HW), lambda t: (d, g(t), 0))
    op_spec = lambda d, g, a, b: pl.BlockSpec((None, HEADS, SCAN_G, a, b), lambda t: (d, 0, g(t), 0, 0))
    state = pl.BlockSpec((None, 2, HEADS, DK, DK), lambda t: (_scan_seq(t), 0, 0, 0, 0))
    state0 = pl.BlockSpec((None, 2, HEADS, DK, DK), lambda t: (jnp.maximum(_scan_seq(t) - NCTX, 0), 0, 0, 0, 0))
    in_specs = [u_spec(0, fwd), u_spec(1, _scan_bwd_group),
                op_spec(0, fwd, 2 * CHUNK, DK), op_spec(1, _scan_bwd_group, 2 * CHUNK, DK),
                op_spec(0, fwd, CHUNK + DK, CHUNK), op_spec(1, _scan_bwd_group, CHUNK + DK, CHUNK),
                op_spec(0, fwd, 1, LANE), op_spec(1, _scan_bwd_group, 1, LANE),
                row(fwd, CB_QB // HEADS), row(fwd, CB_KB // HEADS), row(fwd, CB_VB // HEADS),
                row(_scan_bwd_group, CB_QB // HEADS), row(_scan_bwd_group, CB_KB // HEADS),
                row(_scan_bwd_group, CB_VB // HEADS),
                pl.BlockSpec((1, LANE), lambda t: (0, 0)), state0, state0]
    dec = jnp.pad(ret_decay.reshape(1, 2 * HEADS), ((0, 0), (0, LANE - 2 * HEADS)))
    o_sds = jax.ShapeDtypeStruct((NTOK, HW), F32)
    st_sds = jax.ShapeDtypeStruct((nseq, 2, HEADS, DK, DK), F32)
    return pl.pallas_call(
        _ab_scan_kernel,
        grid=(NTOK // SCAN_RB,),
        in_specs=in_specs,
        out_specs=[row(fwd, 0), row(_scan_bwd_group, 0), row(fwd, 0), row(_scan_bwd_group, 0), state, state],
        out_shape=[o_sds, o_sds, o_sds, o_sds, st_sds, st_sds],
        scratch_shapes=[pltpu.VMEM((2, HEADS, DK, DK), F32), pltpu.VMEM((2, HEADS, DK, DK), F32)],
        compiler_params=_params("arbitrary"),
        name="ab_scan",
    )(u, u, wq, wq, ak, ak, gl, gl, proj, proj, proj, proj, proj, proj, dec, s0, r0)


def _ab_out_kernel(odf_ref, odb_ref, orf_ref, orb_ref, z_ref, gb_ref, na_ref, nb_ref, w_ref, x_ref, gate_ref,
                   o_ref, wb_ref):
    @pl.when(pl.program_id(0) == 0)
    def _():
        wb_ref[...] = w_ref[...].astype(BF16)

    oa = odf_ref[...] + odb_ref[...]
    ob = orf_ref[...] + orb_ref[...]
    z, gb = z_ref[...], gb_ref[...]
    parts = []
    for h in range(HEADS):
        cols = slice(h * DK, (h + 1) * DK)
        a = oa[:, cols]
        a = a * lax.rsqrt(jnp.mean(a * a, axis=-1, keepdims=True) + EPS) * na_ref[...]
        parts.append(a * _silu(z[:, cols]))
    for h in range(HEADS):
        cols = slice(h * DK, (h + 1) * DK)
        b = ob[:, cols]
        mu = jnp.mean(b, axis=-1, keepdims=True)
        var = jnp.mean(jnp.square(b - mu), axis=-1, keepdims=True)
        parts.append((b - mu) * lax.rsqrt(var + EPS) * nb_ref[...] * _silu(gb[:, cols]))
    a = jnp.concatenate(parts, axis=1).astype(BF16)
    o_ref[...] = x_ref[...] + gate_ref[...] * _dot(a, wb_ref[...])


def _ab_out(odf, odb, orf, orb, proj, norm_a, norm_b, w_out_all, layer, x, gate):
    grp = lambda i: (i * MMR_TM // ROWBLK, 0, 0)
    tile = lambda cb: pl.BlockSpec((MMR_TM, HW), lambda i: (i, cb))
    vec = pl.BlockSpec((1, DK), lambda i: (0, 0))
    return pl.pallas_call(
        _ab_out_kernel,
        grid=(NTOK // MMR_TM,),
        in_specs=[tile(0), tile(0), tile(0), tile(0), tile(CB_ZA // HEADS), tile(CB_GB // HEADS), vec, vec,
                  pl.BlockSpec((None, 2 * HW, D), lambda i: (layer, 0, 0)),
                  pl.BlockSpec((MMR_TM, D), lambda i: (i, 0)), pl.BlockSpec((None, 1, D), grp)],
        out_specs=pl.BlockSpec((MMR_TM, D), lambda i: (i, 0)),
        out_shape=jax.ShapeDtypeStruct((NTOK, D), F32),
        scratch_shapes=[pltpu.VMEM((2 * HW, D), BF16)],
        compiler_params=_params("arbitrary"),
        name="ab_out",
    )(odf, odb, orf, orb, proj, proj, norm_a.reshape(1, DK), norm_b.reshape(1, DK), w_out_all, x, gate)


def _filter_features(L):
    r = np.arange(2 * L)
    pos = np.where(r < L, r, 2 * L - r) % L
    t = pos / (L - 1.0)
    bands = np.linspace(1e-4, HY_BANDS - 1, HY_BANDS)
    ang = 2.0 * np.pi * np.outer(pos, bands) / L
    z = np.zeros((2 * L, HY_FW), np.float64)
    z[:, 0] = t
    z[:, 1:1 + HY_BANDS] = np.cos(ang)
    z[:, 1 + HY_BANDS:HY_EMB] = -np.sin(ang)
    z[:, HY_EMB] = (r != L)
    return z.astype(np.float32)


def _filter_kernel(z_ref, w1_ref, b1_ref, f1_ref, w2_ref, b2_ref, f2_ref, w3_ref, dl_ref, o_ref):
    z = z_ref[...]
    hid = jnp.sin(f1_ref[...] * (_dot(z, w1_ref[...], HI) + b1_ref[...]))
    hid = jnp.sin(f2_ref[...] * (_dot(hid, w2_ref[...], HI) + b2_ref[...]))
    filt = _dot3(_split(hid), _split(w3_ref[...]))
    window =jnp.exp(-z[:, 0:1] * dl_ref[...]) * z[:, HY_EMB:HY_EMB + 1]
    o_ref[...] = filt * window


def _hy_filter(L, w1, b1, f1, w2, b2, f2, w3):
    rb = min(512, L)
    nblk = 2 * L // rb
    z = jnp.asarray(_filter_features(L))
    w1p = jnp.pad(w1, ((0, HY_FW - HY_EMB), (0, 0)))
    min_decay = math.log(HY_TARGET) / HY_SLOW_PCT
    max_decay = math.log(HY_TARGET) / HY_FAST_PCT
    deltas = jnp.asarray(np.abs(np.linspace(min_decay, max_decay, D)).astype(np.float32).reshape(1, D))
    vec = lambda v: v.reshape(1, HY_FW)
    full = lambda shp: pl.BlockSpec(shp, lambda i: (0, 0))
    return pl.pallas_call(
        _filter_kernel,
        grid=(nblk,),
        in_specs=[pl.BlockSpec((rb, HY_FW), lambda i: (i, 0)),
                  full((HY_FW, HY_FW)), full((1, HY_FW)), full((1, HY_FW)),
                  full((HY_FW, HY_FW)), full((1, HY_FW)), full((1, HY_FW)),
                  pl.BlockSpec((HY_FW, D), lambda i: (0, i // (nblk // 2))),
                  full((1, D))],
        out_specs=pl.BlockSpec((rb, D), lambda i: (i, 0)),
        out_shape=jax.ShapeDtypeStruct((2 * L, D), F32),
        compiler_params=_params("arbitrary"),
        name="hy_filter_%d" % L,
    )(z, w1p, vec(b1), vec(f1), w2, vec(b2), vec(f2), w3, deltas)


FFT_RT = 8


def _cis(num, den):
    ang = -2.0 * np.pi * (num % den) / den
    return np.cos(ang), np.sin(ang)


def _fft_consts(L):
    n = 2 * L
    n1 = n // FFT_N2
    k1 = np.arange(n1)
    fr, fi = _cis(np.outer(k1, k1), n1)
    half = n1 // 2
    sig = np.block([[fr[:, :half], -fi[:, :half]], [fi[:, :half], fr[:, :half]]])
    ker = np.concatenate([fr, fi], axis=0)
    cr, ci = fr[:, :half].T, -fi[:, :half].T
    inv = np.stack([np.concatenate([cr, -ci], axis=1), np.concatenate([ci, cr], axis=1)]) / n
    k2 = np.arange(FFT_N2)
    gr, gi = _cis(np.outer(k2, k2), FFT_N2)
    f2 = np.block([[gr, -gi], [gi, gr]])
    f2inv = np.block([[gr, gi], [-gi, gr]])
    tr, ti = _cis(np.outer(k1, k2), n)
    f = lambda a: np.asarray(a, np.float32)
    return dict(n1=n1, sig=f(sig), ker=f(ker), inv=f(inv), f2=f(f2), f2inv=f(f2inv),
                twr=f(tr).reshape(n1, FFT_N2, 1), twi=f(ti).reshape(n1, FFT_N2, 1))


def _lin_rows(mat, rows):
    out = []
    for m in range(mat.shape[0]):
        acc = None
        for k in range(mat.shape[1]):
            cf = float(mat[m, k])
            if abs(cf) < 1e-9:
                continue
            term = rows[k] if abs(cf - 1.0) < 1e-9 else (-rows[k] if abs(cf + 1.0) < 1e-9 else cf * rows[k])
            acc = term if acc is None else acc + term
        out.append(jnp.zeros_like(rows[0]) if acc is None else acc)
    return out


def _twiddle_dft(slab_r, slab_i, twr, twi, f2):
    x = jnp.concatenate([slab_r * twr - slab_i * twi, slab_r * twi + slab_i * twr], axis=0)
    y = _dot3(f2, _split(x))
    return y[:FFT_N2], y[FFT_N2:]


def _filter_idft_twiddle(yr, yi, kr, ki, twr, twi, f2inv):
    z = jnp.concatenate([yr * kr - yi * ki, yr * ki + yi * kr], axis=0)
    w = _dot3(f2inv, _split(z))
    wr, wi = w[:FFT_N2], w[FFT_N2:]
    return wr * twr + wi * twi, wi * twr - wr * twi


def _fft1_kernel(m_ref, *refs):
    o_ref = refs[-1]
    m = _split(m_ref[...])
    n1 = o_ref.shape[1]
    for r in range(FFT_RT):
        parts = [x_ref[c:c + 32, r, :] for x_ref in refs[:-1] for c in range(0, x_ref.shape[0], 32)]
        x = jnp.concatenate(parts, axis=0) if len(parts) > 1 else parts[0]
        y = _dot3(m, _split(x))
        o_ref[0, :, r, :] = y[:n1]
        o_ref[1, :, r, :] = y[n1:]


def _fft_stage1(consts, mat, xs, lead_blocks):
    n1 = consts["n1"]
    rows = mat.shape[1] // len(xs)
    specs = [pl.BlockSpec((rows, FFT_RT, D), functools.partial(lambda j, lb: (lb, j, 0), lb=lb))
             for lb in lead_blocks]
    return pl.pallas_call(
        _fft1_kernel,
        grid=(FFT_N2 // FFT_RT,),
        in_specs=[pl.BlockSpec(mat.shape, lambda j: (0, 0))] + specs,
        out_specs=pl.BlockSpec((2, n1, FFT_RT, D), lambda j: (0, 0, j, 0)),
        out_shape=jax.ShapeDtypeStruct((2, n1, FFT_N2, D), F32),
        compiler_params=_params("arbitrary"),
        name="fft1_%d" % len(xs),
    )(jnp.asarray(mat), *xs)


FFT_KS = 2


def _fft2_kernel(a_ref, twr_ref, twi_ref, f2_ref, *rest, conv):
    f2 = _split(f2_ref[...])
    if conv:
        kf_ref, f2inv_ref, o_ref = rest
        f2inv = _split(f2inv_ref[...])
    else:
        o_ref, = rest
    for k in range(FFT_KS):
        twr, twi = twr_ref[k], twi_ref[k]
        yr, yi = _twiddle_dft(a_ref[0, k], a_ref[1, k], twr, twi, f2)
        if conv:
            yr, yi = _filter_idft_twiddle(yr, yi, kf_ref[0, k], kf_ref[1, k], twr, twi, f2inv)
        o_ref[0, k] = yr
        o_ref[1, k] = yi


def _fft_stage2(consts, a, kf):
    n1 = consts["n1"]
    slab = pl.BlockSpec((2, FFT_KS, FFT_N2, D), lambda k: (0, k, 0, 0))
    tw = pl.BlockSpec((FFT_KS, FFT_N2, 1), lambda k: (k, 0, 0))
    mat = pl.BlockSpec((2 * FFT_N2, 2 * FFT_N2), lambda k: (0, 0))
    in_specs = [slab, tw, tw, mat]
    args = [a, jnp.asarray(consts["twr"]), jnp.asarray(consts["twi"]), jnp.asarray(consts["f2"])]
    if kf is not None:
        in_specs += [slab, mat]
        args += [kf, jnp.asarray(consts["f2inv"])]
    return pl.pallas_call(
        functools.partial(_fft2_kernel, conv=kf is not None),
        grid=(n1 // FFT_KS,),
        in_specs=in_specs,
        out_specs=slab,
        out_shape=jax.ShapeDtypeStruct(a.shape, F32),
        compiler_params=_params("arbitrary"),
        name="fft2_%s" % ("conv" if kf is not None else "spec"),
    )(*args)


def _fft3_kernel(m_ref, b_ref, p_ref, x0_ref, bias_ref, *rest):
    o_ref = rest[-1]
    m = _split(m_ref[...])
    bias = bias_ref[...]
    for r in range(FFT_RT):
        x = jnp.concatenate([b_ref[0, :, r, :], b_ref[1, :, r, :]], axis=0)
        y = _dot3(m, _split(x))
        o_ref[:, r, :] = (y + p_ref[:, r, :] * bias) * x0_ref[:, r, :]


def _fft_stage3(consts, b, pv, xv, bias, lead_block):
    n1 = consts["n1"]
    half = n1 // 2
    seq = pl.BlockSpec((half, FFT_RT, D), lambda j, s: (lead_block + s, j, 0))
    in_specs = [pl.BlockSpec((None, half, 2 * n1), lambda j, s: (s, 0, 0)),
                pl.BlockSpec((2, n1, FFT_RT, D), lambda j, s: (0, 0, j, 0)),
                seq, seq, pl.BlockSpec((1, D), lambda j, s: (0, 0))]
    return pl.pallas_call(
        _fft3_kernel,
        grid=(FFT_N2 // FFT_RT, 2),
        in_specs=in_specs,
        out_specs=pl.BlockSpec((half, FFT_RT, D), lambda j, s: (s, j, 0)),
        out_shape=jax.ShapeDtypeStruct((n1, FFT_N2, D), F32),
        compiler_params=_params("arbitrary", "arbitrary"),
        name="fft3",
    )(jnp.asarray(consts["inv"]), b, pv, xv, bias.reshape(1, D))


def _fft_ctx_kernel(p_ref, x0_ref, kern_ref, twr_ref, twi_ref, f2_ref, f2inv_ref, bias_ref, o_ref, kf_ref,
                    *, n1, sig, ker, inv):
    f2 = _split(f2_ref[...])

    @pl.when(pl.program_id(0) == 0)
    def _():
        a = _lin_rows(ker, [kern_ref[r] for r in range(n1)])
        for k in range(n1):
            kf_ref[0, k], kf_ref[1, k] = _twiddle_dft(a[k], a[n1 + k], twr_ref[k], twi_ref[k], f2)

    f2inv = _split(f2inv_ref[...])
    a = _lin_rows(sig, [p_ref[r] for r in range(n1)])
    br, bi = [], []
    for k in range(n1):
        twr, twi = twr_ref[k], twi_ref[k]
        yr, yi = _twiddle_dft(a[k], a[n1 + k], twr, twi, f2)
        wr, wi = _filter_idft_twiddle(yr, yi, kf_ref[0, k], kf_ref[1, k], twr, twi, f2inv)
        br.append(wr)
        bi.append(wi)
    out = _lin_rows(inv, br + bi)
    bias = bias_ref[...]
    for m, y in enumerate(out):
        o_ref[m] = (y + p_ref[m] * bias) * x0_ref[m]


def _fft_ctx(consts, pv, xv, kern, bias):
    n1 = consts["n1"]
    inv = np.concatenate([consts["inv"][0], consts["inv"][1]], axis=0)
    pair = pl.BlockSpec((n1, FFT_N2, D), lambda q: (q, 0, 0))
    full = lambda shp: pl.BlockSpec(shp, lambda q: (0,) * len(shp))
    return pl.pallas_call(
        functools.partial(_fft_ctx_kernel, n1=n1, sig=consts["sig"], ker=consts["ker"], inv=inv),
        grid=(NCTX // 2,),
        in_specs=[pair, pair, full((n1, FFT_N2, D)), full((n1, FFT_N2, 1)), full((n1, FFT_N2, 1)),
                  full((2 * FFT_N2, 2 * FFT_N2)), full((2 * FFT_N2, 2 * FFT_N2)), full((1, D))],
        out_specs=pair,
        out_shape=jax.ShapeDtypeStruct((TOK_CTX // FFT_N2, FFT_N2, D), F32),
        scratch_shapes=[pltpu.VMEM((2, n1, FFT_N2, D), F32)],
        compiler_params=_params("arbitrary"),
        name="fft_ctx",
    )(pv, xv, kern, jnp.asarray(consts["twr"]), jnp.asarray(consts["twi"]), jnp.asarray(consts["f2"]),
      jnp.asarray(consts["f2inv"]), bias.reshape(1, D))


def _hy_long_conv(p, x0, filt_w, bias):
    slabs = (NTOK // FFT_N2, FFT_N2, D)
    pv, xv = p.reshape(slabs), x0.reshape(slabs)
    cc = _fft_consts(LCTX)
    y_ctx = _fft_ctx(cc, pv, xv, _hy_filter(LCTX, *filt_w).reshape(cc["n1"], FFT_N2, D), bias)
    lc = _fft_consts(LLAT)
    n1 = lc["n1"]
    kern = _hy_filter(LLAT, *filt_w).reshape(n1, FFT_N2, D)
    kf = _fft_stage2(lc, _fft_stage1(lc, lc["ker"], [kern], [0]), None)
    lat0 = TOK_CTX // FFT_N2 // (n1 // 2)
    a = _fft_stage1(lc, lc["sig"], [pv, pv], [lat0, lat0 + 1])
    b = _fft_stage2(lc, a, kf)
    y_lat = _fft_stage3(lc, b, pv, xv, bias, lat0)
    return y_ctx.reshape(TOK_CTX, D), y_lat.reshape(TOK_LAT, D)


FFN_TM = 1024
FFN_TF = 256
FFN_HALO = 128


FFN_SR = 32


FFN_CH = 256
FFN_NF = D_FF // FFN_TF
FFN_TILES = NTOK // FFN_TM * FFN_NF


def _ffn_tile(t):
    return jnp.minimum(t, FFN_TILES - 1) // FFN_NF, t % FFN_NF


def _ffn_act_kernel(xc_ref, xp_ref, xn_ref, g_ref, sh_ref, sc_ref, wg_ref, wu_ref, cw_ref, cb_ref,
                    o_ref, h_ref, *slots):
    t = pl.program_id(0)
    i, f = _ffn_tile(t)
    ext = FFN_TM + 2 * FFN_HALO
    is_lat = i >= TOK_CTX // FFN_TM

    @pl.when(t == 0)
    def _():
        for scr in slots[4:]:
            scr[...] = jnp.zeros(scr.shape, F32)

    @pl.when(f == 0)
    def _():
        def norm(x):
            y = x * lax.rsqrt(jnp.mean(x * x, axis=-1, keepdims=True) + EPS) * g_ref[...]
            return (y * (1.0 + sc_ref[...]) + sh_ref[...]).astype(BF16)
        seq_pos = (i * FFN_TM - TOK_CTX) & (LLAT - 1)
        keep_prev = jnp.logical_and(is_lat, seq_pos != 0)
        keep_next = jnp.logical_and(is_lat, seq_pos != LLAT - FFN_TM)
        h_ref[0:FFN_HALO, :] = jnp.where(keep_prev, norm(xp_ref[...]), 0.0).astype(BF16)
        h_ref[FFN_HALO:FFN_HALO + FFN_TM, :] = norm(xc_ref[...])
        h_ref[FFN_HALO + FFN_TM:ext, :] = jnp.where(keep_next, norm(xn_ref[...]), 0.0).astype(BF16)

    def project_gate(scr, lo, hi):
        gate_scr, left_scr, right_scr, _ = scr
        gate = _dot(h_ref[lo:hi, :], wg_ref[...])
        period = jnp.where(is_lat, GRID_W, LCTX)
        r = lax.broadcasted_iota(jnp.int32, (hi - lo, 1), 0)
        col = (r + lo + FFN_HALO) & (period - 1)
        gate_scr[lo:hi, :] = gate
        left_scr[lo:hi, :] = jnp.where(col == 0, 0.0, pltpu.roll(gate, 1, 0))
        right_scr[lo:hi, :] = jnp.where(col == period - 1, 0.0, pltpu.roll(gate, hi - lo - 1, 0))

    def project_up(scr, lo, hi):
        scr[3][lo - FFN_HALO:hi - FFN_HALO, :] = _dot(h_ref[lo:hi, :], wu_ref[...])

    def convolve(scr, strips):
        gate_scr, left_scr, right_scr, up_scr = scr
        ip, _ = _ffn_tile(jnp.maximum(t - 1, 0))
        kidx = lax.broadcasted_iota(jnp.int32, (9, 1), 0)
        mid_row = jnp.logical_and(kidx >= 3, kidx < 6)
        w = jnp.where(jnp.logical_or(ip >= TOK_CTX // FFN_TM, mid_row), cw_ref[...], 0.0)
        bias = cb_ref[...]
        for s in strips:
            z = bias
            for k in range(3):
                lo = s * FFN_SR + FFN_HALO + (k - 1) * GRID_W
                z = z + left_scr[lo:lo + FFN_SR, :] * w[3 * k:3 * k + 1] \
                    + gate_scr[lo:lo + FFN_SR, :] * w[3 * k + 1:3 * k + 2] \
                    + right_scr[lo:lo + FFN_SR, :] * w[3 * k + 2:3 * k + 3]
            out = slice(s * FFN_SR, (s + 1) * FFN_SR)
            o_ref[out, :] = ((0.5 * z) * (1.0 + jnp.tanh(0.5 * z)) * up_scr[out, :]).astype(BF16)

    def step(cur, prev):
        edges = [0] + list(range(FFN_HALO, FFN_HALO + FFN_TM + 1, FFN_CH)) + [ext]
        pieces = []
        for lo, hi in zip(edges[:-1], edges[1:]):
            pieces.append(functools.partial(project_gate, cur, lo, hi))
            if FFN_HALO <= lo < FFN_HALO + FFN_TM:
                pieces.append(functools.partial(project_up, cur, lo, hi))
        nst = FFN_TM // FFN_SR
        for c, piece in enumerate(pieces):
            piece()
            convolve(prev, range(c * nst // len(pieces), (c + 1) * nst // len(pieces)))

    pl.when(t % 2 == 0)(lambda: step(slots[:4], slots[4:]))
    pl.when(t % 2 == 1)(lambda: step(slots[4:], slots[:4]))


def _ffn_act(x, gain, shift, scale, w_gu, conv_w, conv_b):
    hb = FFN_TM // FFN_HALO
    nhb = NTOK // FFN_HALO
    ext = FFN_TM + 2 * FFN_HALO
    row = lambda t: _ffn_tile(t)[0]
    col = lambda t: _ffn_tile(t)[1]
    prev = lambda t: _ffn_tile(jnp.maximum(t - 1, 0))
    vec = pl.BlockSpec((None, 1, D), lambda t: (row(t) * FFN_TM // ROWBLK, 0, 0))
    in_specs = [pl.BlockSpec((FFN_TM, D), lambda t: (row(t), 0)),
                pl.BlockSpec((FFN_HALO, D), lambda t: (jnp.maximum(row(t) * hb - 1, 0), 0)),
                pl.BlockSpec((FFN_HALO, D), lambda t: (jnp.minimum((row(t) + 1) * hb, nhb - 1), 0)),
                pl.BlockSpec((1, D), lambda t: (0, 0)), vec, vec,
                pl.BlockSpec((D, FFN_TF), lambda t: (0, col(t))),
                pl.BlockSpec((D, FFN_TF), lambda t: (0, FFN_NF + col(t))),
                pl.BlockSpec((9, FFN_TF), lambda t: (0, prev(t)[1])),
                pl.BlockSpec((1, FFN_TF), lambda t: (0, prev(t)[1]))]
    return pl.pallas_call(
        _ffn_act_kernel,
        grid=(FFN_TILES + 1,),
        in_specs=in_specs,
        out_specs=pl.BlockSpec((FFN_TM, FFN_TF), lambda t: prev(t)),
        out_shape=jax.ShapeDtypeStruct((NTOK, D_FF), BF16),
        scratch_shapes=[pltpu.VMEM((ext, D), BF16)]
        + 2 * (3 * [pltpu.VMEM((ext, FFN_TF), F32)] + [pltpu.VMEM((FFN_TM, FFN_TF), F32)]),
        compiler_params=_params("arbitrary"),
        name="ffn_act",
    )(x, x, x, gain.reshape(1, D), shift, scale, w_gu, w_gu, conv_w.reshape(9, D_FF), conv_b.reshape(1, D_FF))


def _ab_w_in_cols(w):
    qkv_w, gate_c0 = 3 * HW, 3 * HW + HW
    rest = jnp.concatenate([w[:, qkv_w:gate_c0], w[:, gate_c0 + 4 * HEADS:], w[:, gate_c0:gate_c0 + 4 * HEADS],
                            jnp.zeros((D, LANE - 4 * HEADS), F32)], axis=1)
    return w[:, :qkv_w].astype(BF16), rest.astype(BF16)


def _ab_mixer(qkv, proj, a_log, dt_bias, ret_decay, s_delta0, s_ret0):
    gates = _gates(proj, a_log, dt_bias)
    gates_t = gates[:, :2 * HEADS].reshape(NTOK // DP_RB, DP_NC, CHUNK, 2 * HEADS).transpose(0, 3, 1, 2)
    u, wq, ak, gl = _delta_prep(qkv, gates, gates_t)
    odf, odb, orf, orb, sfin, rfin = _ab_scan(u, wq, ak, gl, proj, ret_decay, s_delta0, s_ret0)
    return (odf, odb, orf, orb), sfin[:NCTX], rfin[:NCTX]


def kernel(x_prompt, x_sample, state_delta, state_ret, c, c_ctx, mod_w, mod_b, norm1, norm2, ab_w_in, ab_conv, ab_a_log, ab_dt_bias, ab_norm_a, ab_norm_b, ab_ret_decay, ab_w_out, hy_w_in, hy_b_in, hy_conv_w, hy_conv_b, hy_f_w1, hy_f_b1, hy_f_freq1, hy_f_w2, hy_f_b2, hy_f_freq2, hy_f_w3, hy_f_bias, hy_w_out, hy_b_out, ffn_w_gate, ffn_w_up, ffn_conv, ffn_conv_b, ffn_w_down, final_norm):
    x = jnp.concatenate([x_prompt.reshape(TOK_CTX, D), x_sample.reshape(TOK_LAT, D)], axis=0)
    cvec = jnp.concatenate([c_ctx[None], c, jnp.zeros((8 - 1 - NLAT, D), F32)], axis=0)
    mod = _mod_all(cvec, mod_w, mod_b)
    new_delta, new_ret = [], []
    for l in range(DEPTH):
        m = [mod[l, :NGROUP, k * D:(k + 1) * D].reshape(NGROUP, 1, D) for k in range(6)]
        j = l // 2
        if l % 2 == 0:
            w_qkv, w_rest = _ab_w_in_cols(ab_w_in[j])
            qkv = _ab_qkv(x, norm1[l], m[0], m[1], w_qkv, ab_conv[j])
            proj = _nmm(x, norm1[l], m[0], m[1], w_rest, None, AB_N // 3, "ab_in")
            heads, sd, sr = _ab_mixer(qkv, proj, ab_a_log[j], ab_dt_bias[j], ab_ret_decay[j],
                                      state_delta[:, j], state_ret[:, j])
            new_delta.append(sd)
            new_ret.append(sr)
            x = _ab_out(*heads, proj, ab_norm_a[j], ab_norm_b[j], ab_w_out, j, x, m[2])
        else:
            x0, p = _hy_in(x, norm1[l], m[0], m[1], hy_w_in[j].astype(BF16), hy_b_in[j], hy_conv_w[j], hy_conv_b[j])
            filt_w = (hy_f_w1[j], hy_f_b1[j], hy_f_freq1[j], hy_f_w2[j], hy_f_b2[j], hy_f_freq2[j], hy_f_w3[j])
            y = _hy_long_conv(p, x0, filt_w, hy_f_bias[j])
            x = _mmr(y, hy_w_out, j, hy_b_out[j], x, m[2], None, "hy_out")
        w_gu = jnp.concatenate([ffn_w_gate[l], ffn_w_up[l]], axis=1).astype(BF16)
        act = _ffn_act(x, norm2[l], m[3], m[4], w_gu, ffn_conv[l], ffn_conv_b[l])
        x = _mmr(act, ffn_w_down, l, None, x, m[5], final_norm if l == DEPTH - 1 else None, "ffn_out")
    y_prompt = x[:TOK_CTX].reshape(NCTX, LCTX, D)
    y_sample = x[TOK_CTX:].reshape(NLAT, LLAT, D)
    return (y_prompt, y_sample, jnp.stack(new_delta, axis=1), jnp.stack(new_ret, axis=1))
```
